```python
import math
import jax
import jax.numpy as jnp
from jax import lax
import numpy as np

D_MODEL = 1024
BATCH = 2
SEQ = 8192
DEPTH = 4

N_MIXERS = 4
N_PER_MIXER = tuple((DEPTH - m + N_MIXERS - 1) // N_MIXERS for m in range(N_MIXERS))
PLE_DIM = 256
ROPE_THETA = 10000.0
EPS = 1e-6
Q_BLOCK = 128
D_FF = (-((-8 * D_MODEL) // 3) + 255) // 256 * 256

GLA_HEADS = 4
GLA_DK = D_MODEL // 2
GLA_DV = D_MODEL
GLA_HK = GLA_DK // GLA_HEADS
GLA_HV = GLA_DV // GLA_HEADS
GLA_RANK = 16
GLA_TAU = 16.0
GLA_CHUNK = 64
GLA_IN = 2 * GLA_DK + 2 * GLA_DV + 2 * GLA_RANK

DIFF_HEAD_DIM = 64
DIFF_HEADS = D_MODEL // (2 * DIFF_HEAD_DIM)
DIFF_IN = 3 * D_MODEL

SSD_D_INNER = 2 * D_MODEL
SSD_HEAD_DIM = 64
SSD_HEADS = SSD_D_INNER // SSD_HEAD_DIM
SSD_GROUPS = 8
SSD_STATE = 128
SSD_CONV = 5
SSD_CHUNK = 128
SSD_CONV_DIM = SSD_D_INNER + 2 * SSD_GROUPS * SSD_STATE
SSD_IN = SSD_D_INNER + SSD_CONV_DIM + 2 * SSD_HEADS

DIL_PAIRS = ((128, 1), (512, 4), (2048, 16))
DIL_GROUPS = len(DIL_PAIRS)
DIL_HEADS = 16
DIL_HEAD_DIM = 64
DIL_WIDTH = DIL_HEADS * DIL_HEAD_DIM
DIL_IN = 3 * DIL_GROUPS * DIL_WIDTH

kernel_name = 'hybrid_bidir_interleaved_encoder'

F32 = jnp.float32


def split_cols(x, sizes):
    out, start = [], 0
    for s in sizes:
        out.append(x[..., start:start + s])
        start += s
    return out


def rms_norm(x, g):
    xf = x.astype(F32)
    y = xf * lax.rsqrt(jnp.mean(xf * xf, axis=-1, keepdims=True) + EPS)
    return (y * g.astype(F32)).astype(x.dtype)


def rope(x, pos):
    hd = x.shape[-1]
    half = hd // 2
    inv = ROPE_THETA ** (-jnp.arange(half, dtype=F32) * 2.0 / hd)
    ang = pos.astype(F32)[:, None] * inv[None, :]
    cos = jnp.cos(ang)[:, None, :]
    sin = jnp.sin(ang)[:, None, :]
    xf = x.astype(F32)
    x1, x2 = xf[..., :half], xf[..., half:]
    return jnp.concatenate([x1 * cos - x2 * sin, x2 * cos + x1 * sin], axis=-1).astype(x.dtype)


def swiglu(h, w_in, w_out):
    g, u = split_cols(h @ w_in, [D_FF, D_FF])
    return (jax.nn.silu(g) * u) @ w_out


def gla_chunked(q, k, v, log_a, strict):
    bsz, nh, seq, dk = q.shape
    dv = v.shape[-1]
    c = GLA_CHUNK
    n = seq // c
    qf = q.astype(F32).reshape(bsz, nh, n, c, dk)
    kf = k.astype(F32).reshape(bsz, nh, n, c, dk)
    vf = v.astype(F32).reshape(bsz, nh, n, c, dv)
    b = jnp.cumsum(log_a.astype(F32).reshape(bsz, nh, n, c, dk), axis=3)
    b_last = b[:, :, :, -1:, :]
    q_in = qf * jnp.exp(b)
    scores = jnp.einsum('bhncd,bhnjd->bhncj', q_in, kf * jnp.exp(-b))
    mask = jnp.tril(jnp.ones((c, c), dtype=bool), -1 if strict else 0)
    scores = jnp.where(mask, scores, 0.0)
    o_intra = jnp.einsum('bhncj,bhnje->bhnce', scores, vf)
    u = jnp.einsum('bhncd,bhnce->bhnde', kf * jnp.exp(b_last - b), vf)
    decay = jnp.exp(b_last[:, :, :, 0, :])

    def step(s_prev, inp):
        u_n, a_n = inp
        return s_prev * a_n[..., None] + u_n, s_prev

    s0 = jnp.zeros((bsz, nh, dk, dv), F32)
    _, s_prev = lax.scan(step, s0, (jnp.moveaxis(u, 2, 0), jnp.moveaxis(decay, 2, 0)))
    s_prev = jnp.moveaxis(s_prev, 0, 2)
    o_inter = jnp.einsum('bhncd,bhnde->bhnce', q_in, s_prev)
    return (o_intra + o_inter).reshape(bsz, nh, seq, dv)


def gla_mixer(h, w_in, w_gate_f, b_gate_f, w_gate_b, b_gate_b, g_out, w_out):
    bsz, seq, _ = h.shape
    q, k, v, r, zf, zb = split_cols(h @ w_in, [GLA_DK, GLA_DK, GLA_DV, GLA_DV, GLA_RANK, GLA_RANK])

    def heads(t, hd):
        return t.reshape(bsz, seq, GLA_HEADS, hd).transpose(0, 2, 1, 3)

    qh = heads(q, GLA_HK) * (GLA_HK ** -0.5)
    kh = heads(k, GLA_HK)
    vh = heads(v, GLA_HV)
    la_f = heads(jax.nn.log_sigmoid((zf @ w_gate_f + b_gate_f).astype(F32)) / GLA_TAU, GLA_HK)
    la_b = heads(jax.nn.log_sigmoid((zb @ w_gate_b + b_gate_b).astype(F32)) / GLA_TAU, GLA_HK)

    def flip(t):
        return jnp.flip(t, axis=2)

    o = gla_chunked(qh, kh, vh, la_f, False) + flip(
        gla_chunked(flip(qh), flip(kh), flip(vh), flip(la_b), True))
    o = rms_norm(o.transpose(0, 2, 1, 3).astype(h.dtype), g_out).reshape(bsz, seq, GLA_DV)
    return (o * jax.nn.silu(r)) @ w_out


def diff_attn_mixer(h, pos, w_in, g_q, g_k, lam_q1, lam_k1, lam_q2, lam_k2, g_sub, w_out, lambda_init):
    bsz, seq, _ = h.shape
    nh, dh = DIFF_HEADS, DIFF_HEAD_DIM
    q, k, v = split_cols(h @ w_in, [D_MODEL, D_MODEL, D_MODEL])
    q = rms_norm(q.reshape(bsz, seq, nh, 2, dh), g_q)
    k = rms_norm(k.reshape(bsz, seq, nh, 2, dh), g_k)
    q = rope(q.reshape(bsz, seq, 2 * nh, dh), pos).reshape(bsz, seq, nh, 2, dh)
    k = rope(k.reshape(bsz, seq, 2 * nh, dh), pos).reshape(bsz, seq, nh, 2, dh)
    lam = (jnp.exp(jnp.sum(lam_q1.astype(F32) * lam_k1.astype(F32)))
           - jnp.exp(jnp.sum(lam_q2.astype(F32) * lam_k2.astype(F32))) + lambda_init)
    nb = seq // Q_BLOCK
    qb = q.reshape(bsz, nb, Q_BLOCK, nh, 2, dh).transpose(1, 0, 3, 4, 2, 5)
    kt = k.transpose(0, 2, 3, 1, 4)
    vt = v.reshape(bsz, seq, nh, 2 * dh).transpose(0, 2, 1, 3)
    scale = dh ** -0.5

    def block(q_blk):
        s = jnp.einsum('bhtqd,bhtkd->bhtqk', q_blk, kt).astype(F32) * scale
        pr = jax.nn.softmax(s, axis=-1)
        a = pr[:, :, 0] - lam * pr[:, :, 1]
        return jnp.einsum('bhqk,bhkd->bhqd', a.astype(vt.dtype), vt)

    o = lax.map(block, qb)
    o = o.transpose(1, 0, 3, 2, 4).reshape(bsz, seq, nh, 2 * dh)
    o = rms_norm(o, g_sub) * (1.0 - lambda_init)
    return o.reshape(bsz, seq, D_MODEL) @ w_out


def depthwise_conv_centered(x, w, b):
    width, ch = w.shape
    y = lax.conv_general_dilated(
        x, w[:, None, :].astype(x.dtype), window_strides=(1,),
        padding=[((width - 1) // 2, width // 2)],
        dimension_numbers=('NWC', 'WIO', 'NWC'), feature_group_count=ch)
    return y + b


def ssd_chunked(x, dt, a, bm, cm, strict):
    bsz, seq, nh, hp = x.shape
    ng, ns = bm.shape[2], bm.shape[3]
    hg = nh // ng
    c = SSD_CHUNK
    nc = seq // c
    xdt = (x.astype(F32) * dt[..., None]).reshape(bsz, nc, c, ng, hg, hp)
    acum = jnp.cumsum((dt * a).reshape(bsz, nc, c, ng, hg), axis=2)
    bc = bm.astype(F32).reshape(bsz, nc, c, ng, ns)
    cc = cm.astype(F32).reshape(bsz, nc, c, ng, ns)
    mask = jnp.tril(jnp.ones((c, c), dtype=bool), -1 if strict else 0)[:, :, None, None]
    diff = acum[:, :, :, None] - acum[:, :, None, :]
    lmat = jnp.where(mask, jnp.exp(jnp.where(mask, diff, 0.0)), 0.0)
    cb = jnp.einsum('bnigs,bnjgs->bnijg', cc, bc)
    y_diag = jnp.einsum('bnijg,bnijgh,bnjghp->bnighp', cb, lmat, xdt)
    a_last = acum[:, :, -1]
    decay_states = jnp.exp(a_last[:, :, None] - acum)
    states = jnp.einsum('bnjgs,bnjgh,bnjghp->bnghps', bc, decay_states, xdt)

    def step(s_prev, inp):
        st, dec = inp
        return s_prev * dec[..., None, None] + st, s_prev

    s0 = jnp.zeros((bsz, ng, hg, hp, ns), F32)
    _, prev = lax.scan(step, s0, (jnp.moveaxis(states, 1, 0), jnp.moveaxis(jnp.exp(a_last), 1, 0)))
    prev = jnp.moveaxis(prev, 0, 1)
    y_off = jnp.einsum('bnigs,bnigh,bnghps->bnighp', cc, jnp.exp(acum), prev)
    return (y_diag + y_off).reshape(bsz, seq, nh, hp)


def ssd_mixer(h, w_in, conv_w, conv_b, dt_bias_f, a_log_f, dt_bias_b, a_log_b, d_skip, g_norm, w_out):
    bsz, seq, _ = h.shape
    z, xbc, dt_f, dt_b = split_cols(h @ w_in, [SSD_D_INNER, SSD_CONV_DIM, SSD_HEADS, SSD_HEADS])
    xbc = jax.nn.silu(depthwise_conv_centered(xbc, conv_w, conv_b))
    xs, bm, cm = split_cols(xbc, [SSD_D_INNER, SSD_GROUPS * SSD_STATE, SSD_GROUPS * SSD_STATE])
    xs = xs.reshape(bsz, seq, SSD_HEADS, SSD_HEAD_DIM)
    bm = bm.reshape(bsz, seq, SSD_GROUPS, SSD_STATE)
    cm = cm.reshape(bsz, seq, SSD_GROUPS, SSD_STATE)
    dtf = jax.nn.softplus((dt_f + dt_bias_f).astype(F32))
    dtb = jax.nn.softplus((dt_b + dt_bias_b).astype(F32))
    af = -jnp.exp(a_log_f.astype(F32))
    ab = -jnp.exp(a_log_b.astype(F32))

    def flip(t):
        return jnp.flip(t, axis=1)

    y = ssd_chunked(xs, dtf, af, bm, cm, False) + flip(
        ssd_chunked(flip(xs), flip(dtb), ab, flip(bm), flip(cm), True))
    y = y + xs.astype(F32) * d_skip.astype(F32)[:, None]
    y = y.astype(h.dtype).reshape(bsz, seq, SSD_D_INNER) * jax.nn.silu(z)
    y = rms_norm(y.reshape(bsz, seq, SSD_GROUPS, SSD_D_INNER // SSD_GROUPS),
                 g_norm.reshape(SSD_GROUPS, SSD_D_INNER // SSD_GROUPS))
    return y.reshape(bsz, seq, SSD_D_INNER) @ w_out


def dilated_group(q, k, v, window, dilation):
    bsz, seq, nh, hd = q.shape
    n_side = (window // 2) // dilation
    offsets = dilation * jnp.arange(-n_side, n_side + 1, dtype=jnp.int32)
    nb = seq // Q_BLOCK
    qb = q.reshape(bsz, nb, Q_BLOCK, nh, hd).swapaxes(0, 1)
    starts = jnp.arange(nb, dtype=jnp.int32) * Q_BLOCK
    scale = hd ** -0.5

    def block(args):
        q_blk, start = args
        idx = (start + jnp.arange(Q_BLOCK, dtype=jnp.int32))[:, None] + offsets[None, :]
        valid = (idx >= 0) & (idx < seq)
        idx = jnp.clip(idx, 0, seq - 1)
        kg = k[:, idx]
        vg = v[:, idx]
        s = jnp.einsum('bqhd,bqmhd->bqhm', q_blk, kg).astype(F32) * scale
        s = jnp.where(valid[None, :, None, :], s, -jnp.inf)
        m = jnp.max(s, axis=-1, keepdims=True)
        e = jnp.exp(s - m)
        l = jnp.sum(e, axis=-1, keepdims=True)
        o = jnp.einsum('bqhm,bqmhd->bqhd', (e / l).astype(vg.dtype), vg)
        return o, (m + jnp.log(l))[..., 0]

    o, lse = lax.map(block, (qb, starts))
    return (o.swapaxes(0, 1).reshape(bsz, seq, nh, hd),
            lse.swapaxes(0, 1).reshape(bsz, seq, nh))


def dilated_mixer(h, pos, w_in, g_q, g_k, w_out):
    bsz, seq, _ = h.shape
    qkv = (h @ w_in).reshape(bsz, seq, 3, DIL_GROUPS, DIL_HEADS, DIL_HEAD_DIM)
    q = rms_norm(qkv[:, :, 0], g_q[:, None, :])
    k = rms_norm(qkv[:, :, 1], g_k[:, None, :])
    v = qkv[:, :, 2]
    flat = (bsz, seq, DIL_GROUPS * DIL_HEADS, DIL_HEAD_DIM)
    q = rope(q.reshape(flat), pos).reshape(bsz, seq, DIL_GROUPS, DIL_HEADS, DIL_HEAD_DIM)
    k = rope(k.reshape(flat), pos).reshape(bsz, seq, DIL_GROUPS, DIL_HEADS, DIL_HEAD_DIM)
    outs, lses = [], []
    for gi, (window, dilation) in enumerate(DIL_PAIRS):
        o, lse = dilated_group(q[:, :, gi], k[:, :, gi], v[:, :, gi], window, dilation)
        outs.append(o.astype(F32))
        lses.append(lse)
    wts = jax.nn.softmax(jnp.stack(lses, axis=0), axis=0)
    o = jnp.sum(wts[..., None] * jnp.stack(outs, axis=0), axis=0)
    return o.astype(h.dtype).reshape(bsz, seq, DIL_WIDTH) @ w_out


def setup_inputs(seed: int = 0) -> dict:
    key = jax.random.key(seed)
    ks = iter(jax.random.split(key, 48))
    n_a, n_b, n_c, n_d = N_PER_MIXER

    def nrm(shape, std):
        return jax.random.normal(next(ks), shape, F32) * std

    def dense(shape):
        return nrm(shape, shape[-2] ** -0.5)

    def gain(shape):
        return 1.0 + nrm(shape, 0.05)

    x = nrm((BATCH, SEQ, D_MODEL), 1.0)
    p = nrm((DEPTH, BATCH, SEQ, PLE_DIM), 1.0)

    def dt_bias(shape):
        dt = jnp.exp(jax.random.uniform(next(ks), shape, F32, math.log(1e-3), math.log(1e-1)))
        return dt + jnp.log(-jnp.expm1(-dt))

    def a_log(shape):
        return jnp.log(jax.random.uniform(next(ks), shape, F32, 1.0, 16.0))

    return {
        'x': x,
        'p': p,
        'norm_mix': gain((DEPTH, D_MODEL)),
        'norm_ffn': gain((DEPTH, D_MODEL)),
        'ffn_w_in': dense((DEPTH, D_MODEL, 2 * D_FF)),
        'ffn_w_out': dense((DEPTH, D_FF, D_MODEL)),
        'ple_norm': gain((DEPTH, D_MODEL)),
        'ple_w_gate': dense((DEPTH, D_MODEL, D_MODEL)),
        'ple_w_proj': dense((DEPTH, PLE_DIM, D_MODEL)),
        'gla_w_in': dense((n_a, D_MODEL, GLA_IN)),
        'gla_w_gate_f': dense((n_a, GLA_RANK, GLA_DK)),
        'gla_b_gate_f': nrm((n_a, GLA_DK), 0.01),
        'gla_w_gate_b': dense((n_a, GLA_RANK, GLA_DK)),
        'gla_b_gate_b': nrm((n_a, GLA_DK), 0.01),
        'gla_g_out': gain((n_a, GLA_HV)),
        'gla_w_out': dense((n_a, GLA_DV, D_MODEL)),
        'diff_w_in': dense((n_b, D_MODEL, DIFF_IN)),
        'diff_g_q': gain((n_b, 2, DIFF_HEAD_DIM)),
        'diff_g_k': gain((n_b, 2, DIFF_HEAD_DIM)),
        'diff_lam_q1': nrm((n_b, DIFF_HEAD_DIM), 0.1),
        'diff_lam_k1': nrm((n_b, DIFF_HEAD_DIM), 0.1),
        'diff_lam_q2': nrm((n_b, DIFF_HEAD_DIM), 0.1),
        'diff_lam_k2': nrm((n_b, DIFF_HEAD_DIM), 0.1),
        'diff_g_sub': gain((n_b, 2 * DIFF_HEAD_DIM)),
        'diff_w_out': dense((n_b, D_MODEL, D_MODEL)),
        'ssd_w_in': dense((n_c, D_MODEL, SSD_IN)),
        'ssd_conv_w': nrm((n_c, SSD_CONV, SSD_CONV_DIM), SSD_CONV ** -0.5),
        'ssd_conv_b': nrm((n_c, SSD_CONV_DIM), 0.01),
        'ssd_dt_bias_f': dt_bias((n_c, SSD_HEADS)),
        'ssd_a_log_f': a_log((n_c, SSD_HEADS)),
        'ssd_dt_bias_b': dt_bias((n_c, SSD_HEADS)),
        'ssd_a_log_b': a_log((n_c, SSD_HEADS)),
        'ssd_d': gain((n_c, SSD_HEADS)),
        'ssd_g_norm': gain((n_c, SSD_D_INNER)),
        'ssd_w_out': dense((n_c, SSD_D_INNER, D_MODEL)),
        'dil_w_in': dense((n_d, D_MODEL, DIL_IN)),
        'dil_g_q': gain((n_d, DIL_GROUPS, DIL_HEAD_DIM)),
        'dil_g_k': gain((n_d, DIL_GROUPS, DIL_HEAD_DIM)),
        'dil_w_out': dense((n_d, DIL_WIDTH, D_MODEL)),
    }


def reference(x, p, norm_mix, norm_ffn, ffn_w_in, ffn_w_out, ple_norm, ple_w_gate, ple_w_proj,
              gla_w_in, gla_w_gate_f, gla_b_gate_f, gla_w_gate_b, gla_b_gate_b, gla_g_out, gla_w_out,
              diff_w_in, diff_g_q, diff_g_k, diff_lam_q1, diff_lam_k1, diff_lam_q2, diff_lam_k2,
              diff_g_sub, diff_w_out,
              ssd_w_in, ssd_conv_w, ssd_conv_b, ssd_dt_bias_f, ssd_a_log_f, ssd_dt_bias_b,
              ssd_a_log_b, ssd_d, ssd_g_norm, ssd_w_out,
              dil_w_in, dil_g_q, dil_g_k, dil_w_out):
    seq = x.shape[1]
    pos = jnp.arange(seq, dtype=jnp.int32)
    h = x
    for i in range(DEPTH):
        kind = i % N_MIXERS
        j = i // N_MIXERS
        hn = rms_norm(h, norm_mix[i])
        if kind == 0:
            mix = gla_mixer(hn, gla_w_in[j], gla_w_gate_f[j], gla_b_gate_f[j], gla_w_gate_b[j],
                            gla_b_gate_b[j], gla_g_out[j], gla_w_out[j])
        elif kind == 1:
            lambda_init = 0.8 - 0.6 * math.exp(-0.3 * i)
            mix = diff_attn_mixer(hn, pos, diff_w_in[j], diff_g_q[j], diff_g_k[j], diff_lam_q1[j],
                                  diff_lam_k1[j], diff_lam_q2[j], diff_lam_k2[j], diff_g_sub[j],
                                  diff_w_out[j], lambda_init)
        elif kind == 2:
            mix = ssd_mixer(hn, ssd_w_in[j], ssd_conv_w[j], ssd_conv_b[j], ssd_dt_bias_f[j],
                            ssd_a_log_f[j], ssd_dt_bias_b[j], ssd_a_log_b[j], ssd_d[j],
                            ssd_g_norm[j], ssd_w_out[j])
        else:
            mix = dilated_mixer(hn, pos, dil_w_in[j], dil_g_q[j], dil_g_k[j], dil_w_out[j])
        h = h + mix
        h = h + swiglu(rms_norm(h, norm_ffn[i]), ffn_w_in[i], ffn_w_out[i])
        gate = jax.nn.sigmoid(rms_norm(h, ple_norm[i]) @ ple_w_gate[i])
        h = h + gate * (p[i] @ ple_w_proj[i])
    return h
```

```python
import functools
import math

import jax
import jax.numpy as jnp
from jax import lax
from jax.experimental import pallas as pl
from jax.experimental.pallas import tpu as pltpu

F32 = jnp.float32
BF16 = jnp.bfloat16

D_MODEL = 1024
DEPTH = 4
PLE_DIM = 256
ROPE_THETA = 10000.0
EPS = 1e-6
D_FF = 2816
LOG2E = math.log2(math.e)
NEG_BIG = -1e30

GLA_HEADS = 4
GLA_DK = 512
GLA_DV = 1024
GLA_HK = 128
GLA_HV = 256
GLA_RANK = 16
GLA_TAU = 16.0
GLA_CHUNK = 64

DIFF_HEADS = 8
DIFF_HD = 64

SSD_INNER = 2048
SSD_P = 64
SSD_HEADS = 32
SSD_GROUPS = 8
SSD_N = 128
SSD_CONV = 5
SSD_CHUNK = 128
SSD_CONV_DIM = 4096

DIL_PAIRS = ((128, 1), (512, 4), (2048, 16))
DIL_HEADS = 16
DIL_HD = 64
DIL_WIDTH = 1024
DIL_SIDE = 64

LANES = 128


def _cparams(sem, vmem_mb=None):
    kw = dict(dimension_semantics=sem)
    if vmem_mb is not None:
        kw["vmem_limit_bytes"] = vmem_mb << 20
    return pltpu.CompilerParams(**kw)


def _dot(a, b):
    return jnp.dot(a, b, preferred_element_type=F32)


def _dot_nt(a, b):
    return lax.dot_general(a, b, (((1,), (1,)), ((), ())), preferred_element_type=F32)


def _dot_tn(a, b):
    return lax.dot_general(a, b, (((0,), (0,)), ((), ())), preferred_element_type=F32)


def _dot_split(a_bf16, x):
    hi = x.astype(BF16)
    lo = (x - hi.astype(F32)).astype(BF16)
    return _dot(a_bf16, hi) + _dot(a_bf16, lo)


def _rms_rows(x, g):
    ms = jnp.mean(x * x, axis=-1, keepdims=True)
    return x * lax.rsqrt(ms + EPS) * g


def _silu(x):
    return x * jax.nn.sigmoid(x)


def _nm_kernel(x_ref, g_ref, w_ref, o_ref, xn_ref):
    @pl.when(pl.program_id(1) == 0)
    def _():
        xn_ref[...] = _rms_rows(x_ref[...], g_ref[...]).astype(BF16)

    o_ref[...] = _dot(xn_ref[...], w_ref[...]).astype(o_ref.dtype)


def norm_matmul(x, g, w, out_dtype, tm=512, tn=None):
    t, k = x.shape
    n = w.shape[1]
    tn = n if tn is None else tn
    return pl.pallas_call(
        _nm_kernel,
        grid=(t // tm, n // tn),
        in_specs=[pl.BlockSpec((tm, k), lambda i, j: (i, 0)),
                  pl.BlockSpec((1, k), lambda i, j: (0, 0)),
                  pl.BlockSpec((k, tn), lambda i, j: (0, j))],
        out_specs=pl.BlockSpec((tm, tn), lambda i, j: (i, j)),
        out_shape=jax.ShapeDtypeStruct((t, n), out_dtype),
        scratch_shapes=[pltpu.VMEM((tm, k), BF16)],
        compiler_params=_cparams(("parallel", "arbitrary")),
        name="norm_matmul",
    )(x, g.reshape(1, k), w)


def _rope_tables(seq):
    half = DIFF_HD // 2
    inv = ROPE_THETA ** (-jnp.arange(half, dtype=F32) * 2.0 / DIFF_HD)
    ang = jnp.arange(seq, dtype=jnp.int32).astype(F32)[:, None] * inv[None, :]
    cos = jnp.cos(ang)
    sin = jnp.sin(ang)
    return jnp.tile(cos, (1, 4)), jnp.tile(jnp.concatenate([-sin, sin], axis=1), (1, 2))


def _seg64_ones():
    r = jnp.arange(LANES)
    return (r[:, None] // 64 == r[None, :] // 64).astype(BF16)


def _nm_rope_kernel(x_ref, g_ref, w_ref, hg_ref, cos_ref, sin_ref, bd_ref, o_ref, xn_ref, *,
                    n_rope, tn):
    j = pl.program_id(1)

    @pl.when(j == 0)
    def _():
        xn_ref[...] = _rms_rows(x_ref[...], g_ref[...]).astype(BF16)

    acc = _dot(xn_ref[...], w_ref[...])

    @pl.when(j < n_rope)
    def _():
        tm = acc.shape[0]
        lane = lax.broadcasted_iota(jnp.int32, (tm, LANES), 1)
        first = (lane % 64) < 32
        cos = cos_ref[...]
        sin = sin_ref[...]
        bd = bd_ref[...]
        for c in range(tn // LANES):
            sl = slice(c * LANES, (c + 1) * LANES)
            xc = acc[:, sl]
            ss = _dot((xc * xc).astype(BF16), bd)
            xs = xc * lax.rsqrt(ss * (1.0 / 64) + EPS) * hg_ref[:, sl]
            partner = jnp.where(first, pltpu.roll(xs, 96, 1), pltpu.roll(xs, 32, 1))
            o_ref[:, sl] = (xs * cos + partner * sin).astype(o_ref.dtype)

    @pl.when(j >= n_rope)
    def _():
        o_ref[...] = acc.astype(o_ref.dtype)


def norm_matmul_rope(x, g, w, head_gain, n_rope_cols, seq, tm=512, tn=1024):
    t, k = x.shape
    n = w.shape[1]
    cos, sin = _rope_tables(seq)
    nsb = seq // tm
    kern = functools.partial(_nm_rope_kernel, n_rope=n_rope_cols // tn, tn=tn)
    return pl.pallas_call(
        kern,
        grid=(t // tm, n // tn),
        in_specs=[pl.BlockSpec((tm, k), lambda i, j: (i, 0)),
                  pl.BlockSpec((1, k), lambda i, j: (0, 0)),
                  pl.BlockSpec((k, tn), lambda i, j: (0, j)),
                  pl.BlockSpec((1, tn), lambda i, j: (0, j)),
                  pl.BlockSpec((tm, LANES), lambda i, j: (i % nsb, 0)),
                  pl.BlockSpec((tm, LANES), lambda i, j: (i % nsb, 0)),
                  pl.BlockSpec((LANES, LANES), lambda i, j: (0, 0))],
        out_specs=pl.BlockSpec((tm, tn), lambda i, j: (i, j)),
        out_shape=jax.ShapeDtypeStruct((t, n), BF16),
        scratch_shapes=[pltpu.VMEM((tm, k), BF16)],
        compiler_params=_cparams(("parallel", "arbitrary")),
        name="norm_matmul_rope",
    )(x, g.reshape(1, k), w, head_gain.reshape(1, n), cos, sin, _seg64_ones())


def _mmres_kernel(a_ref, w_ref, h_ref, o_ref):
    o_ref[...] = h_ref[...] + _dot(a_ref[...], w_ref[...])


def matmul_residual(a, w, h, tm=512):
    t, k = a.shape
    n = w.shape[1]
    return pl.pallas_call(
        _mmres_kernel,
        grid=(t // tm,),
        in_specs=[pl.BlockSpec((tm, k), lambda i: (i, 0)),
                  pl.BlockSpec((k, n), lambda i: (0, 0)),
                  pl.BlockSpec((tm, n), lambda i: (i, 0))],
        out_specs=pl.BlockSpec((tm, n), lambda i: (i, 0)),
        out_shape=jax.ShapeDtypeStruct((t, n), F32),
        compiler_params=_cparams(("parallel",)),
        name="matmul_residual",
    )(a, w, h)


def _ffn_kernel(x_ref, g_ref, wg_ref, wu_ref, wo_ref, o_ref, xn_ref):
    @pl.when(pl.program_id(1) == 0)
    def _():
        x = x_ref[...]
        xn_ref[...] = _rms_rows(x, g_ref[...]).astype(BF16)
        o_ref[...] = x

    xn = xn_ref[...]
    gate = _dot(xn, wg_ref[...])
    up = _dot(xn, wu_ref[...])
    a = (_silu(gate) * up).astype(BF16)
    o_ref[...] += _dot(a, wo_ref[...])


def ffn_residual(h, g, w_in, w_out, tm=1024, tf=256):
    t, k = h.shape
    nf = D_FF // tf
    return pl.pallas_call(
        _ffn_kernel,
        grid=(t // tm, nf),
        in_specs=[pl.BlockSpec((tm, k), lambda i, f: (i, 0)),
                  pl.BlockSpec((1, k), lambda i, f: (0, 0)),
                  pl.BlockSpec((k, tf), lambda i, f: (0, f)),
                  pl.BlockSpec((k, tf), lambda i, f: (0, nf + f)),
                  pl.BlockSpec((tf, k), lambda i, f: (f, 0))],
        out_specs=pl.BlockSpec((tm, k), lambda i, f: (i, 0)),
        out_shape=jax.ShapeDtypeStruct((t, k), F32),
        scratch_shapes=[pltpu.VMEM((tm, k), BF16)],
        compiler_params=_cparams(("parallel", "arbitrary")),
        name="ffn",
    )(h, g.reshape(1, k), w_in, w_in, w_out)


def _ple_kernel(x_ref, g_ref, wg_ref, p_ref, wp_ref, o_ref):
    x = x_ref[...]
    xn = _rms_rows(x, g_ref[...]).astype(BF16)
    gate = jax.nn.sigmoid(_dot(xn, wg_ref[...]))
    proj = _dot(p_ref[...].astype(BF16), wp_ref[...])
    o_ref[...] = x + gate * proj


def ple_residual(h, g, w_gate, p, w_proj, tm=512):
    t, k = h.shape
    return pl.pallas_call(
        _ple_kernel,
        grid=(t // tm,),
        in_specs=[pl.BlockSpec((tm, k), lambda i: (i, 0)),
                  pl.BlockSpec((1, k), lambda i: (0, 0)),
                  pl.BlockSpec((k, k), lambda i: (0, 0)),
                  pl.BlockSpec((tm, PLE_DIM), lambda i: (i, 0)),
                  pl.BlockSpec((PLE_DIM, k), lambda i: (0, 0))],
        out_specs=pl.BlockSpec((tm, k), lambda i: (i, 0)),
        out_shape=jax.ShapeDtypeStruct((t, k), F32),
        compiler_params=_cparams(("parallel",)),
        name="ple",
    )(h, g.reshape(1, k), w_gate, p, w_proj)


def _gla_kernel(q_ref, k_ref, v_ref, z_ref, wg_ref, bg_ref, cm_ref, mk_ref, o_ref,
                st_ref, la_ref, *, nchunk):
    d = pl.program_id(0)

    @pl.when(pl.program_id(3) == 0)
    def _():
        st_ref[...] = jnp.zeros_like(st_ref)

    x = _dot(z_ref[...].astype(BF16), wg_ref[...]) + bg_ref[...]
    la_ref[...] = (jnp.minimum(x, 0.0) - jnp.log(1.0 + jnp.exp(-jnp.abs(x)))) * (1.0 / GLA_TAU)
    cm = cm_ref[...]
    mk = mk_ref[...]
    c = GLA_CHUNK
    for cc in range(nchunk):
        ci = cc + d * (nchunk - 1 - 2 * cc)
        rows = pl.ds(pl.multiple_of(ci * c, c), c)
        la = la_ref[rows, :]
        b = _dot_split(cm, la)
        tot = jnp.sum(la, axis=0, keepdims=True)
        qc = q_ref[rows, :].astype(F32)
        kc = k_ref[rows, :].astype(F32)
        vc = v_ref[rows, :]
        q_in = (qc * jnp.exp(b) * (GLA_HK ** -0.5)).astype(BF16)
        k_out = (kc * jnp.exp(-b)).astype(BF16)
        k_dec = (kc * jnp.exp(tot - b)).astype(BF16)
        sc = (_dot_nt(q_in, k_out) * mk).astype(BF16)
        st = st_ref[...]
        o = _dot(sc, vc) + _dot_nt(q_in, st.astype(BF16))
        o_ref[rows, :] = o.astype(o_ref.dtype)
        st_ref[...] = st * jnp.exp(tot) + _dot_tn(vc, k_dec)


def gla_core(proj, z, wg, bg, bsz, seq, rows=512):
    t = bsz * seq
    nblk = seq // rows
    c = GLA_CHUNK
    ii = jnp.arange(c)
    lower = ii[:, None] >= ii[None, :]
    cm = jnp.stack([lower, lower.T]).astype(BF16)
    mk = jnp.stack([lower, ii[:, None] < ii[None, :]]).astype(F32)

    def rb(d, b, i):
        return b * nblk + i + d * (nblk - 1 - 2 * i)

    kern = functools.partial(_gla_kernel, nchunk=rows // c)
    return pl.pallas_call(
        kern,
        grid=(2, bsz, GLA_HEADS, nblk),
        in_specs=[pl.BlockSpec((rows, GLA_HK), lambda d, b, h, i: (rb(d, b, i), h)),
                  pl.BlockSpec((rows, GLA_HK), lambda d, b, h, i: (rb(d, b, i), 4 + h)),
                  pl.BlockSpec((rows, GLA_HV), lambda d, b, h, i: (rb(d, b, i), 4 + h)),
                  pl.BlockSpec((rows, LANES), lambda d, b, h, i: (rb(d, b, i), 0)),
                  pl.BlockSpec((None, LANES, GLA_HK), lambda d, b, h, i: (d, 0, h)),
                  pl.BlockSpec((None, 1, GLA_HK), lambda d, b, h, i: (d, 0, h)),
                  pl.BlockSpec((None, c, c), lambda d, b, h, i: (d, 0, 0)),
                  pl.BlockSpec((None, c, c), lambda d, b, h, i: (d, 0, 0))],
        out_specs=pl.BlockSpec((None, rows, GLA_HV), lambda d, b, h, i: (d, rb(d, b, i), h)),
        out_shape=jax.ShapeDtypeStruct((2, t, GLA_DV), BF16),
        scratch_shapes=[pltpu.VMEM((GLA_HV, GLA_HK), F32), pltpu.VMEM((rows, GLA_HK), F32)],
        compiler_params=_cparams(("parallel", "parallel", "parallel", "arbitrary")),
        name="gla_core",
    )(proj, proj, proj, z, wg, bg, cm, mk)


def _gla_out_kernel(of_ref, ob_ref, r_ref, g_ref, w_ref, h_ref, o_ref):
    o = of_ref[...].astype(F32) + ob_ref[...].astype(F32)
    parts = []
    for hh in range(GLA_HEADS):
        oh = o[:, hh * GLA_HV:(hh + 1) * GLA_HV]
        ms = jnp.mean(oh * oh, axis=-1, keepdims=True)
        parts.append(oh * lax.rsqrt(ms + EPS))
    on = jnp.concatenate(parts, axis=1) * g_ref[...]
    a = (on * _silu(r_ref[...].astype(F32))).astype(BF16)
    o_ref[...] = h_ref[...] + _dot(a, w_ref[...])


def gla_out(o2, proj, g_out, w_out, h, tm=512):
    t, n = h.shape
    return pl.pallas_call(
        _gla_out_kernel,
        grid=(t // tm,),
        in_specs=[pl.BlockSpec((None, tm, GLA_DV), lambda i: (0, i, 0)),
                  pl.BlockSpec((None, tm, GLA_DV), lambda i: (1, i, 0)),
                  pl.BlockSpec((tm, GLA_DV), lambda i: (i, 2)),
                  pl.BlockSpec((1, GLA_DV), lambda i: (0, 0)),
                  pl.BlockSpec((GLA_DV, n), lambda i: (0, 0)),
                  pl.BlockSpec((tm, n), lambda i: (i, 0))],
        out_specs=pl.BlockSpec((tm, n), lambda i: (i, 0)),
        out_shape=jax.ShapeDtypeStruct((t, n), F32),
        compiler_params=_cparams(("parallel",)),
        name="gla_out",
    )(o2, o2, proj, jnp.tile(g_out, GLA_HEADS).reshape(1, GLA_DV), w_out, h)


def gla_layer(h, norm_g, w_in, w_gate_f, b_gate_f, w_gate_b, b_gate_b, g_out, w_out, bsz, seq):
    n_main = 2 * GLA_DK + 2 * GLA_DV
    proj = norm_matmul(h, norm_g, w_in[:, :n_main].astype(BF16), BF16, tn=1024)
    wz = jnp.pad(w_in[:, n_main:], ((0, 0), (0, LANES - 2 * GLA_RANK))).astype(BF16)
    z = norm_matmul(h, norm_g, wz, F32)
    wg = jnp.zeros((2, LANES, GLA_DK), F32)
    wg = wg.at[0, :GLA_RANK].set(w_gate_f).at[1, GLA_RANK:2 * GLA_RANK].set(w_gate_b).astype(BF16)
    bg = jnp.stack([b_gate_f, b_gate_b]).reshape(2, 1, GLA_DK)
    o2 = gla_core(proj, z, wg, bg, bsz, seq)
    return gla_out(o2, proj, g_out, w_out.astype(BF16), h)


def _diff_kernel(q_ref, k_ref, v_ref, lam_ref, gs_ref, o_ref, qz_ref, m_ref, acc_ref, *,
                 tk, lambda_init):
    tq = q_ref.shape[0]
    q = q_ref[...]
    lane = lax.broadcasted_iota(jnp.int32, (tq, LANES), 1)
    zero = jnp.zeros_like(q)
    qz_ref[0] = jnp.where(lane < DIFF_HD, q, zero)
    qz_ref[1] = jnp.where(lane >= DIFF_HD, q, zero)
    m_ref[...] = jnp.full_like(m_ref, NEG_BIG)
    acc_ref[...] = jnp.zeros_like(acc_ref)
    ones = jnp.ones((tk, LANES), BF16)

    def body(ki, carry):
        rows = pl.ds(pl.multiple_of(ki * tk, tk), tk)
        kb = k_ref[rows, :]
        vaug = jnp.concatenate([v_ref[rows, :], ones], axis=1)
        for t in range(2):
            s = _dot_nt(qz_ref[t], kb)
            m_prev = m_ref[t]
            m_new = jnp.maximum(m_prev, jnp.max(s, axis=1, keepdims=True))
            alpha = jnp.exp2(m_prev - m_new)
            p = jnp.exp2(s - pltpu.repeat(m_new, tk // LANES, axis=1))
            acc_ref[t] = pltpu.repeat(alpha, 2, axis=1) * acc_ref[t] + _dot(p.astype(BF16), vaug)
            m_ref[t] = m_new
        return carry

    lax.fori_loop(0, k_ref.shape[0] // tk, body, 0)

    lam4 = lam_ref[...]
    lam = (jnp.exp(jnp.sum(lam4[0:1] * lam4[1:2], axis=1, keepdims=True))
           - jnp.exp(jnp.sum(lam4[2:3] * lam4[3:4], axis=1, keepdims=True)) + lambda_init)
    a1 = acc_ref[0]
    a2 = acc_ref[1]
    o = a1[:, :LANES] / a1[:, LANES:] - lam * (a2[:, :LANES] / a2[:, LANES:])
    ms = jnp.mean(o * o, axis=-1, keepdims=True)
    o_ref[...] = (o * lax.rsqrt(ms + EPS) * gs_ref[...] * (1.0 - lambda_init)).astype(o_ref.dtype)


def diff_core(proj, lam4, g_sub, lambda_init, bsz, seq, tq=512, tk=512):
    t = bsz * seq
    nq = seq // tq
    kern = functools.partial(_diff_kernel, tk=tk, lambda_init=lambda_init)
    return pl.pallas_call(
        kern,
        grid=(bsz, DIFF_HEADS, nq),
        in_specs=[pl.BlockSpec((tq, LANES), lambda b, h, i: (b * nq + i, h)),
                  pl.BlockSpec((seq, LANES), lambda b, h, i: (b, 8 + h)),
                  pl.BlockSpec((seq, LANES), lambda b, h, i: (b, 16 + h)),
                  pl.BlockSpec((4, DIFF_HD), lambda b, h, i: (0, 0)),
                  pl.BlockSpec((1, LANES), lambda b, h, i: (0, 0))],
        out_specs=pl.BlockSpec((tq, LANES), lambda b, h, i: (b * nq + i, h)),
        out_shape=jax.ShapeDtypeStruct((t, D_MODEL), BF16),
        scratch_shapes=[pltpu.VMEM((2, tq, LANES), BF16),
                        pltpu.VMEM((2, tq, LANES), F32),
                        pltpu.VMEM((2, tq, 2 * LANES), F32)],
        compiler_params=_cparams(("parallel", "parallel", "arbitrary")),
        name="diff_core",
    )(proj, proj, proj, lam4, g_sub.reshape(1, LANES))


def diff_layer(h, norm_g, layer_idx, w_in, g_q, g_k, lam_q1, lam_k1, lam_q2, lam_k2, g_sub, w_out,
               bsz, seq):
    lambda_init = 0.8 - 0.6 * math.exp(-0.3 * layer_idx)
    qscale = (DIFF_HD ** -0.5) * LOG2E
    head_gain = jnp.concatenate([jnp.tile(g_q.reshape(-1), DIFF_HEADS) * qscale,
                                 jnp.tile(g_k.reshape(-1), DIFF_HEADS),
                                 jnp.ones((D_MODEL,), F32)])
    proj = norm_matmul_rope(h, norm_g, w_in.astype(BF16), head_gain, 2 * D_MODEL, seq)
    lam4 = jnp.stack([lam_q1, lam_k1, lam_q2, lam_k2])
    o = diff_core(proj, lam4, g_sub, lambda_init, bsz, seq)
    return matmul_residual(o, w_out.astype(BF16), h)


def _conv_kernel(prev_ref, cur_ref, next_ref, w_ref, b_ref, o_ref, buf_ref, *, halo):
    i = pl.program_id(1)
    ts = cur_ref.shape[0]
    prev = prev_ref[...].astype(F32)
    nxt = next_ref[...].astype(F32)
    buf_ref[0:halo, :] = jnp.where(i == 0, 0.0, prev)
    buf_ref[halo:halo + ts, :] = cur_ref[...].astype(F32)
    buf_ref[halo + ts:, :] = jnp.where(i == pl.num_programs(1) - 1, 0.0, nxt)
    w = w_ref[...]
    y = b_ref[...] + w[2:3] * buf_ref[halo:halo + ts, :]
    for k in (0, 1, 3, 4):
        y = y + w[k:k + 1] * buf_ref[pl.ds(halo + k - 2, ts), :]
    o_ref[...] = _silu(y).astype(o_ref.dtype)


def ssd_conv(zx, conv_w, conv_b, bsz, seq, ts=512, tc=512):
    t = bsz * seq
    halo = 16
    nrb = seq // ts
    hb = ts // halo
    nhb = seq // halo
    off = SSD_INNER // tc
    kern = functools.partial(_conv_kernel, halo=halo)
    wpad = jnp.pad(conv_w, ((0, 8 - SSD_CONV), (0, 0)))
    return pl.pallas_call(
        kern,
        grid=(bsz, nrb, SSD_CONV_DIM // tc),
        in_specs=[pl.BlockSpec((halo, tc), lambda b, i, c: (b * nhb + jnp.maximum(i * hb - 1, 0), off + c)),
                  pl.BlockSpec((ts, tc), lambda b, i, c: (b * nrb + i, off + c)),
                  pl.BlockSpec((halo, tc), lambda b, i, c: (b * nhb + jnp.minimum((i + 1) * hb, nhb - 1), off + c)),
                  pl.BlockSpec((8, tc), lambda b, i, c: (0, c)),
                  pl.BlockSpec((1, tc), lambda b, i, c: (0, c))],
        out_specs=pl.BlockSpec((ts, tc), lambda b, i, c: (b * nrb + i, c)),
        out_shape=jax.ShapeDtypeStruct((t, SSD_CONV_DIM), BF16),
        scratch_shapes=[pltpu.VMEM((ts + 2 * halo, tc), F32)],
        compiler_params=_cparams(("parallel", "parallel", "parallel")),
        name="ssd_conv",
    )(zx, zx, zx, wpad, conv_b.reshape(1, -1))


def _ssd_kernel(x_ref, b_ref, c_ref, dt_ref, bias_ref, aexp_ref, e2_ref, e5_ref, selt_ref,
                cmr_ref, cmc_ref, mk_ref, o_ref, st_ref, *, nchunk):
    d = pl.program_id(0)

    @pl.when(pl.program_id(3) == 0)
    def _():
        st_ref[...] = jnp.zeros_like(st_ref)

    c = SSD_CHUNK
    cmr = cmr_ref[...]
    cmc = cmc_ref[...]
    mk = mk_ref[...]
    e2 = e2_ref[...]
    e5 = e5_ref[...]
    selt = selt_ref[...]
    hp = 4 * SSD_P
    lane = lax.broadcasted_iota(jnp.int32, (c, hp), 1)
    for cc in range(nchunk):
        ci = cc + d * (nchunk - 1 - 2 * cc)
        rows = pl.ds(pl.multiple_of(ci * c, c), c)
        xr = dt_ref[rows, :] + bias_ref[...]
        dt = jnp.maximum(xr, 0.0) + jnp.log(1.0 + jnp.exp(-jnp.abs(xr)))
        hi = dt.astype(BF16)
        lo = (dt - hi.astype(F32)).astype(BF16)
        dt2 = _dot(hi, e2) + _dot(lo, e2)
        adt = dt * aexp_ref[...]
        ahi = adt.astype(BF16)
        alo = (adt - ahi.astype(F32)).astype(BF16)
        ad2 = _dot(ahi, e2) + _dot(alo, e2)
        ad5 = _dot(ahi, e5) + _dot(alo, e5)
        adr = _dot_nt(selt, ahi) + _dot_nt(selt, alo)
        ac2 = _dot_split(cmr, ad2)
        ac5 = _dot_split(cmr, ad5)
        rhi = adr.astype(BF16)
        rlo = (adr - rhi.astype(F32)).astype(BF16)
        acr = _dot(rhi, cmc) + _dot(rlo, cmc)
        tot2 = jnp.sum(ad2, axis=0, keepdims=True)

        x = x_ref[rows, :].astype(F32)
        bm = b_ref[rows, :]
        cmat = c_ref[rows, :]
        xdt = x * dt2
        xdt_b = xdt.astype(BF16)
        cb = _dot_nt(cmat, bm) * mk
        y = _dot(cmat, st_ref[...].astype(BF16)) * jnp.exp(ac2)
        for hh in range(4):
            col = ac5[:, hh * LANES:(hh + 1) * LANES]
            row = acr[hh:hh + 1, :]
            m = (cb * jnp.exp(jnp.minimum(col - row, 0.0))).astype(BF16)
            xh = jnp.where((lane >= hh * SSD_P) & (lane < (hh + 1) * SSD_P), xdt_b,
                           jnp.zeros_like(xdt_b))
            y = y + _dot(m, xh)
        o_ref[rows, :] = y.astype(o_ref.dtype)
        xw = (xdt * jnp.exp(tot2 - ac2)).astype(BF16)
        st_ref[...] = st_ref[...] * jnp.exp(tot2) + _dot_tn(bm, xw)


def ssd_core(xbc, dtr, dt_bias, a_neg, bsz, seq, rows=256):
    t = bsz * seq
    nblk = seq // rows
    c = SSD_CHUNK
    ii = jnp.arange(c)
    lower = ii[:, None] >= ii[None, :]
    cmr = jnp.stack([lower, lower.T]).astype(BF16)
    cmc = jnp.stack([lower.T, lower]).astype(BF16)
    mk = jnp.stack([lower, ii[:, None] < ii[None, :]]).astype(F32)
    src = (jnp.arange(2)[:, None, None] * SSD_HEADS + 4 * jnp.arange(SSD_GROUPS)[None, :, None]
           + jnp.arange(4)[None, None, :])
    lanes = jnp.arange(LANES)
    onehot = (lanes[None, None, None, :] == src[..., None]).astype(BF16)
    e2 = jnp.repeat(onehot, SSD_P, axis=2).transpose(0, 1, 3, 2)
    e5 = jnp.repeat(onehot, LANES, axis=2).transpose(0, 1, 3, 2)
    selt = jnp.pad(onehot, ((0, 0), (0, 0), (0, 4), (0, 0)))
    e2 = e2.reshape(2 * SSD_GROUPS, LANES, 4 * SSD_P)
    e5 = e5.reshape(2 * SSD_GROUPS, LANES, 4 * LANES)
    selt = selt.reshape(2 * SSD_GROUPS, 8, LANES)
    bias = jnp.pad(dt_bias, (0, LANES - 2 * SSD_HEADS)).reshape(1, LANES)
    aexp = jnp.pad(a_neg, (0, LANES - 2 * SSD_HEADS)).reshape(1, LANES)

    def rb(d, b, i):
        return b * nblk + i + d * (nblk - 1 - 2 * i)

    kern = functools.partial(_ssd_kernel, nchunk=rows // c)
    hp = 4 * SSD_P
    return pl.pallas_call(
        kern,
        grid=(2, bsz, SSD_GROUPS, nblk),
        in_specs=[pl.BlockSpec((rows, hp), lambda d, b, g, i: (rb(d, b, i), g)),
                  pl.BlockSpec((rows, SSD_N), lambda d, b, g, i: (rb(d, b, i), 16 + g)),
                  pl.BlockSpec((rows, SSD_N), lambda d, b, g, i: (rb(d, b, i), 24 + g)),
                  pl.BlockSpec((rows, LANES), lambda d, b, g, i: (rb(d, b, i), 0)),
                  pl.BlockSpec((1, LANES), lambda d, b, g, i: (0, 0)),
                  pl.BlockSpec((1, LANES), lambda d, b, g, i: (0, 0)),
                  pl.BlockSpec((None, LANES, hp), lambda d, b, g, i: (d * SSD_GROUPS + g, 0, 0)),
                  pl.BlockSpec((None, LANES, 4 * LANES), lambda d, b, g, i: (d * SSD_GROUPS + g, 0, 0)),
                  pl.BlockSpec((None, 8, LANES), lambda d, b, g, i: (d * SSD_GROUPS + g, 0, 0)),
                  pl.BlockSpec((None, c, c), lambda d, b, g, i: (d, 0, 0)),
                  pl.BlockSpec((None, c, c), lambda d, b, g, i: (d, 0, 0)),
                  pl.BlockSpec((None, c, c), lambda d, b, g, i: (d, 0, 0))],
        out_specs=pl.BlockSpec((None, rows, hp), lambda d, b, g, i: (d, rb(d, b, i), g)),
        out_shape=jax.ShapeDtypeStruct((2, t, SSD_INNER), BF16),
        scratch_shapes=[pltpu.VMEM((SSD_N, hp), F32)],
        compiler_params=_cparams(("parallel", "parallel", "parallel", "arbitrary")),
        name="ssd_core",
    )(xbc, xbc, xbc, dtr, bias, aexp, e2, e5, selt, cmr, cmc, mk)


def _ssd_out_kernel(yf_ref, yb_ref, x_ref, z_ref, dsk_ref, g_ref, w_ref, h_ref, o_ref):
    y = yf_ref[...].astype(F32) + yb_ref[...].astype(F32) + x_ref[...].astype(F32) * dsk_ref[...]
    y = y * _silu(z_ref[...].astype(F32))
    gw = SSD_INNER // SSD_GROUPS
    parts = []
    for gg in range(SSD_GROUPS):
        yg = y[:, gg * gw:(gg + 1) * gw]
        ms = jnp.mean(yg * yg, axis=-1, keepdims=True)
        parts.append(yg * lax.rsqrt(ms + EPS))
    a = (jnp.concatenate(parts, axis=1) * g_ref[...]).astype(BF16)
    o_ref[...] = h_ref[...] + _dot(a, w_ref[...])


def ssd_out(y2, xbc, zx, d_skip, g_norm, w_out, h, tm=512):
    t, n = h.shape
    return pl.pallas_call(
        _ssd_out_kernel,
        grid=(t // tm,),
        in_specs=[pl.BlockSpec((None, tm, SSD_INNER), lambda i: (0, i, 0)),
                  pl.BlockSpec((None, tm, SSD_INNER), lambda i: (1, i, 0)),
                  pl.BlockSpec((tm, SSD_INNER), lambda i: (i, 0)),
                  pl.BlockSpec((tm, SSD_INNER), lambda i: (i, 0)),
                  pl.BlockSpec((1, SSD_INNER), lambda i: (0, 0)),
                  pl.BlockSpec((1, SSD_INNER), lambda i: (0, 0)),
                  pl.BlockSpec((SSD_INNER, n), lambda i: (0, 0)),
                  pl.BlockSpec((tm, n), lambda i: (i, 0))],
        out_specs=pl.BlockSpec((tm, n), lambda i: (i, 0)),
        out_shape=jax.ShapeDtypeStruct((t, n), F32),
        compiler_params=_cparams(("parallel",)),
        name="ssd_out",
    )(y2, y2, xbc, zx, jnp.repeat(d_skip, SSD_P).reshape(1, SSD_INNER),
      g_norm.reshape(1, SSD_INNER), w_out, h)


def ssd_layer(h, norm_g, w_in, conv_w, conv_b, dt_bias_f, a_log_f, dt_bias_b, a_log_b, d_skip,
              g_norm, w_out, bsz, seq):
    n_main = SSD_INNER + SSD_CONV_DIM
    zx = norm_matmul(h, norm_g, w_in[:, :n_main].astype(BF16), BF16, tn=1024)
    wdt = jnp.pad(w_in[:, n_main:], ((0, 0), (0, LANES - 2 * SSD_HEADS))).astype(BF16)
    dtr = norm_matmul(h, norm_g, wdt, F32)
    xbc = ssd_conv(zx, conv_w, conv_b, bsz, seq)
    dt_bias = jnp.concatenate([dt_bias_f, dt_bias_b])
    a_neg = -jnp.exp(jnp.concatenate([a_log_f, a_log_b]))
    y2 = ssd_core(xbc, dtr, dt_bias, a_neg, bsz, seq)
    return ssd_out(y2, xbc, zx, d_skip, g_norm, w_out.astype(BF16), h)


def _dil_kernel(q_ref, kp_ref, kc_ref, kn_ref, vp_ref, vc_ref, vn_ref, o_ref, lse_ref):
    ai = pl.program_id(2)
    na = pl.num_programs(2)
    tq = q_ref.shape[0]
    side = DIL_SIDE
    kcat = jnp.concatenate([kp_ref[...], kc_ref[...], kn_ref[...]], axis=0)
    vcat = jnp.concatenate([vp_ref[...], vc_ref[...], vn_ref[...]], axis=0)
    nk = tq + 2 * side
    ii = lax.broadcasted_iota(jnp.int32, (tq, nk), 0)
    jj = lax.broadcasted_iota(jnp.int32, (tq, nk), 1)
    kpos = ai * tq - side + jj
    valid = (jj >= ii) & (jj <= ii + 2 * side) & (kpos >= 0) & (kpos < na * tq)
    lane = lax.broadcasted_iota(jnp.int32, (tq, LANES), 1)
    lo_q = lane < DIL_HD
    lane_k = lax.broadcasted_iota(jnp.int32, (nk, LANES), 1)
    lo_k = lane_k < DIL_HD
    one = jnp.ones((nk, LANES), BF16)
    for hp in range(DIL_HEADS // 2):
        sl = slice(hp * LANES, (hp + 1) * LANES)
        q = q_ref[:, sl]
        k = kcat[:, sl]
        v = vcat[:, sl]
        zq = jnp.zeros_like(q)
        accs = []
        ms = []
        for half in range(2):
            qm = jnp.where(lo_q if half == 0 else ~lo_q, q, zq)
            s = jnp.where(valid, _dot_nt(qm, k), NEG_BIG)
            m = jnp.max(s, axis=1, keepdims=True)
            p = jnp.exp2(s - m).astype(BF16)
            vm = jnp.where(lo_k if half == 0 else ~lo_k, v, one)
            accs.append(_dot(p, vm))
            ms.append(m)
        num = jnp.where(lo_q, accs[0], accs[1])
        den = jnp.where(lo_q, pltpu.roll(accs[0], DIL_HD, 1), pltpu.roll(accs[1], DIL_HD, 1))
        mm = jnp.where(lo_q, ms[0], ms[1])
        o_ref[:, sl] = (num / den).astype(o_ref.dtype)
        lse_ref[:, sl] = (mm * (1.0 / LOG2E)) + jnp.log(den)


def dil_core(proj, gi, dilation, bsz, seq, tq=128):
    t = bsz * seq
    a = seq // dilation
    na = a // tq
    side = DIL_SIDE
    hs = tq // side
    nhb = a // side
    width = 3 * 3 * DIL_WIDTH
    p3 = proj.reshape(bsz * a, dilation * width)
    cpb = width // DIL_WIDTH
    qc, kc, vc = gi, 3 + gi, 6 + gi

    def cur(col):
        return pl.BlockSpec((tq, DIL_WIDTH), lambda b, r, i: (b * na + i, r * cpb + col))

    def prev(col):
        return pl.BlockSpec((side, DIL_WIDTH),
                            lambda b, r, i: (b * nhb + jnp.maximum(i * hs - 1, 0), r * cpb + col))

    def nxt(col):
        return pl.BlockSpec((side, DIL_WIDTH),
                            lambda b, r, i: (b * nhb + jnp.minimum((i + 1) * hs, nhb - 1), r * cpb + col))

    out_spec = pl.BlockSpec((tq, DIL_WIDTH), lambda b, r, i: (b * na + i, r))
    o, lse = pl.pallas_call(
        _dil_kernel,
        grid=(bsz, dilation, na),
        in_specs=[cur(qc), prev(kc), cur(kc), nxt(kc), prev(vc), cur(vc), nxt(vc)],
        out_specs=[out_spec, out_spec],
        out_shape=[jax.ShapeDtypeStruct((bsz * a, dilation * DIL_WIDTH), BF16),
                   jax.ShapeDtypeStruct((bsz * a, dilation * DIL_WIDTH), F32)],
        compiler_params=_cparams(("parallel", "parallel", "parallel")),
        name=f"dil_core_{dilation}",
    )(p3, p3, p3, p3, p3, p3, p3)
    return o.reshape(t, DIL_WIDTH), lse.reshape(t, DIL_WIDTH)


def _dil_out_kernel(o0_ref, o1_ref, o2_ref, l0_ref, l1_ref, l2_ref, w_ref, h_ref, o_ref):
    l0 = l0_ref[...]
    l1 = l1_ref[...]
    l2 = l2_ref[...]
    mx = jnp.maximum(jnp.maximum(l0, l1), l2)
    e0 = jnp.exp(l0 - mx)
    e1 = jnp.exp(l1 - mx)
    e2 = jnp.exp(l2 - mx)
    num = (e0 * o0_ref[...].astype(F32) + e1 * o1_ref[...].astype(F32)
           + e2 * o2_ref[...].astype(F32))
    a = (num / (e0 + e1 + e2)).astype(BF16)
    o_ref[...] = h_ref[...] + _dot(a, w_ref[...])


def dil_out(os, lses, w_out, h, tm=512):
    t, n = h.shape
    row = pl.BlockSpec((tm, n), lambda i: (i, 0))
    return pl.pallas_call(
        _dil_out_kernel,
        grid=(t // tm,),
        in_specs=[row, row, row, row, row, row, pl.BlockSpec((DIL_WIDTH, n), lambda i: (0, 0)), row],
        out_specs=row,
        out_shape=jax.ShapeDtypeStruct((t, n), F32),
        compiler_params=_cparams(("parallel",)),
        name="dil_out",
    )(*os, *lses, w_out, h)


def dil_layer(h, norm_g, w_in, g_q, g_k, w_out, bsz, seq):
    qscale = (DIL_HD ** -0.5) * LOG2E
    head_gain = jnp.concatenate([jnp.tile(g_q, (1, DIL_HEADS)).reshape(-1) * qscale,
                                 jnp.tile(g_k, (1, DIL_HEADS)).reshape(-1),
                                 jnp.ones((3 * DIL_WIDTH,), F32)])
    proj = norm_matmul_rope(h, norm_g, w_in.astype(BF16), head_gain, 6 * DIL_WIDTH, seq)
    os, lses = [], []
    for gi, (_, dilation) in enumerate(DIL_PAIRS):
        o, lse = dil_core(proj, gi, dilation, bsz, seq)
        os.append(o)
        lses.append(lse)
    return dil_out(os, lses, w_out.astype(BF16), h)


def kernel(x, p, norm_mix, norm_ffn, ffn_w_in, ffn_w_out, ple_norm, ple_w_gate, ple_w_proj, gla_w_in, gla_w_gate_f, gla_b_gate_f, gla_w_gate_b, gla_b_gate_b, gla_g_out, gla_w_out, diff_w_in, diff_g_q, diff_g_k, diff_lam_q1, diff_lam_k1, diff_lam_q2, diff_lam_k2, diff_g_sub, diff_w_out, ssd_w_in, ssd_conv_w, ssd_conv_b, ssd_dt_bias_f, ssd_a_log_f, ssd_dt_bias_b, ssd_a_log_b, ssd_d, ssd_g_norm, ssd_w_out, dil_w_in, dil_g_q, dil_g_k, dil_w_out):
    bsz, seq, dm = x.shape
    t = bsz * seq
    h = x.reshape(t, dm)
    for i in range(DEPTH):
        kind = i % 4
        j = i // 4
        if kind == 0:
            h = gla_layer(h, norm_mix[i], gla_w_in[j], gla_w_gate_f[j], gla_b_gate_f[j],
                          gla_w_gate_b[j], gla_b_gate_b[j], gla_g_out[j], gla_w_out[j], bsz, seq)
        elif kind == 1:
            h = diff_layer(h, norm_mix[i], i, diff_w_in[j], diff_g_q[j], diff_g_k[j], diff_lam_q1[j],
                           diff_lam_k1[j], diff_lam_q2[j], diff_lam_k2[j], diff_g_sub[j],
                           diff_w_out[j], bsz, seq)
        elif kind == 2:
            h = ssd_layer(h, norm_mix[i], ssd_w_in[j], ssd_conv_w[j], ssd_conv_b[j],
                          ssd_dt_bias_f[j], ssd_a_log_f[j], ssd_dt_bias_b[j], ssd_a_log_b[j],
                          ssd_d[j], ssd_g_norm[j], ssd_w_out[j], bsz, seq)
        else:
            h = dil_layer(h, norm_mix[i], dil_w_in[j], dil_g_q[j], dil_g_k[j], dil_w_out[j], bsz, seq)
        h = ffn_residual(h, norm_ffn[i], ffn_w_in[i].astype(BF16), ffn_w_out[i].astype(BF16))
        h = ple_residual(h, ple_norm[i], ple_w_gate[i].astype(BF16), p[i].reshape(t, PLE_DIM),
                         ple_w_proj[i].astype(BF16))
    return h.reshape(bsz, seq, dm)
```

```python
import functools
import math

import jax
import jax.numpy as jnp
from jax import lax
from jax.experimental import pallas as pl
from jax.experimental.pallas import tpu as pltpu

F32 = jnp.float32
BF16 = jnp.bfloat16

D_MODEL = 1024
DEPTH = 4
PLE_DIM = 256
ROPE_THETA = 10000.0
EPS = 1e-6
D_FF = 2816
LOG2E = math.log2(math.e)
NEG_BIG = -1e30

GLA_HEADS = 4
GLA_DK = 512
GLA_DV = 1024
GLA_HK = 128
GLA_HV = 256
GLA_RANK = 16
GLA_TAU = 16.0
GLA_CHUNK = 64

DIFF_HEADS = 8
DIFF_HD = 64

SSD_INNER = 2048
SSD_P = 64
SSD_HEADS = 32
SSD_GROUPS = 8
SSD_N = 128
SSD_CONV = 5
SSD_CHUNK = 128
SSD_CONV_DIM = 4096

DIL_PAIRS = ((128, 1), (512, 4), (2048, 16))
DIL_HEADS = 16
DIL_HD = 64
DIL_WIDTH = 1024
DIL_SIDE = 64

LANES = 128


def _cparams(sem, vmem_mb=None):
    kw = dict(dimension_semantics=sem)
    if vmem_mb is not None:
        kw["vmem_limit_bytes"] = vmem_mb << 20
    return pltpu.CompilerParams(**kw)


def _dot(a, b):
    return jnp.dot(a, b, preferred_element_type=F32)


def _dot_nt(a, b):
    return lax.dot_general(a, b, (((1,), (1,)), ((), ())), preferred_element_type=F32)


def _dot_tn(a, b):
    return lax.dot_general(a, b, (((0,), (0,)), ((), ())), preferred_element_type=F32)


def _dot_split(a_bf16, x):
    hi = x.astype(BF16)
    lo = (x - hi.astype(F32)).astype(BF16)
    return _dot(a_bf16, hi) + _dot(a_bf16, lo)


def _rms_rows(x, g):
    ms = jnp.mean(x * x, axis=-1, keepdims=True)
    return x * lax.rsqrt(ms + EPS) * g


def _silu(x):
    return x * jax.nn.sigmoid(x)


def _nm_kernel(x_ref, g_ref, w_ref, o_ref, xn_ref):
    @pl.when(pl.program_id(1) == 0)
    def _():
        xn_ref[...] = _rms_rows(x_ref[...], g_ref[...]).astype(BF16)

    o_ref[...] = _dot(xn_ref[...], w_ref[...]).astype(o_ref.dtype)


def norm_matmul(x, g, w, out_dtype, tm=512, tn=None):
    t, k = x.shape
    n = w.shape[1]
    tn = n if tn is None else tn
    return pl.pallas_call(
        _nm_kernel,
        grid=(t // tm, n // tn),
        in_specs=[pl.BlockSpec((tm, k), lambda i, j: (i, 0)),
                  pl.BlockSpec((1, k), lambda i, j: (0, 0)),
                  pl.BlockSpec((k, tn), lambda i, j: (0, j))],
        out_specs=pl.BlockSpec((tm, tn), lambda i, j: (i, j)),
        out_shape=jax.ShapeDtypeStruct((t, n), out_dtype),
        scratch_shapes=[pltpu.VMEM((tm, k), BF16)],
        compiler_params=_cparams(("parallel", "arbitrary")),
        name="norm_matmul",
    )(x, g.reshape(1, k), w)


def _rope_tables(seq):
    half = DIFF_HD // 2
    inv = ROPE_THETA ** (-jnp.arange(half, dtype=F32) * 2.0 / DIFF_HD)
    ang = jnp.arange(seq, dtype=jnp.int32).astype(F32)[:, None] * inv[None, :]
    cos = jnp.cos(ang)
    sin = jnp.sin(ang)
    return jnp.tile(cos, (1, 4)), jnp.tile(jnp.concatenate([-sin, sin], axis=1), (1, 2))


def _seg64_ones():
    r = jnp.arange(LANES)
    return (r[:, None] // 64 == r[None, :] // 64).astype(BF16)


def _nm_rope_kernel(x_ref, g_ref, w_ref, hg_ref, cos_ref, sin_ref, bd_ref, o_ref, xn_ref, *xs_refs,
                    n_rope, tn, dil):
    j = pl.program_id(1)
    tm = x_ref.shape[0]
    n = tm // dil

    @pl.when(j == 0)
    def _():
        if dil == 1:
            xn_ref[...] = _rms_rows(x_ref[...], g_ref[...]).astype(BF16)
        else:
            xs_ref = xs_refs[0]
            nch = x_ref.shape[1] // LANES
            for c in range(nch):
                xs_ref[c] = x_ref[:, c * LANES:(c + 1) * LANES]
            for r in range(dil):
                xr = jnp.concatenate([xs_ref[c, pl.ds(r, n, stride=dil), :] for c in range(nch)], axis=1)
                xn_ref[r * n:(r + 1) * n, :] = _rms_rows(xr, g_ref[...]).astype(BF16)

    acc = _dot(xn_ref[...], w_ref[...])

    def put(sl, val):
        if len(o_ref.shape) == 2:
            o_ref[:, sl] = val
        else:
            for r in range(dil):
                o_ref[r, :, sl] = val[r * n:(r + 1) * n]

    @pl.when(j < n_rope)
    def _():
        lane = lax.broadcasted_iota(jnp.int32, (tm, LANES), 1)
        first = (lane % 64) < 32
        cos = cos_ref[...]
        sin = sin_ref[...]
        bd = bd_ref[...]
        for c in range(tn // LANES):
            sl = slice(c * LANES, (c + 1) * LANES)
            xc = acc[:, sl]
            ss = _dot((xc * xc).astype(BF16), bd)
            xs = xc * lax.rsqrt(ss * (1.0 / 64) + EPS) * hg_ref[:, sl]
            partner = jnp.where(first, pltpu.roll(xs, 96, 1), pltpu.roll(xs, 32, 1))
            put(sl, (xs * cos + partner * sin).astype(o_ref.dtype))

    @pl.when(j >= n_rope)
    def _():
        put(slice(0, tn), acc.astype(o_ref.dtype))


DIL_SUPER = 2048


def norm_matmul_rope(x, g, w, head_gain, n_rope_cols, seq, dil=None, tm=512, tn=1024):
    t, k = x.shape
    n = w.shape[1]
    cos, sin = _rope_tables(seq)
    nsb = seq // tm
    d = 1 if dil is None else dil
    if d > 1:
        cos, sin = (tb.reshape(nsb, tm // d, d, LANES).transpose(0, 2, 1, 3).reshape(seq, LANES)
                    for tb in (cos, sin))
    kern = functools.partial(_nm_rope_kernel, n_rope=n_rope_cols // tn, tn=tn, dil=d)
    if dil is None:
        out_spec = pl.BlockSpec((tm, tn), lambda i, j: (i, j))
        out_shape = jax.ShapeDtypeStruct((t, n), BF16)
    else:
        per = DIL_SUPER // tm
        out_spec = pl.BlockSpec((None, d, tm // d, tn), lambda i, j: (i // per, 0, i % per, j))
        out_shape = jax.ShapeDtypeStruct((t // DIL_SUPER, d, DIL_SUPER // d, n), BF16)
    scratch = [pltpu.VMEM((tm, k), BF16)]
    if d > 1:
        scratch.append(pltpu.VMEM((k // LANES, tm, LANES), F32))
    return pl.pallas_call(
        kern,
        grid=(t // tm, n // tn),
        in_specs=[pl.BlockSpec((tm, k), lambda i, j: (i, 0)),
                  pl.BlockSpec((1, k), lambda i, j: (0, 0)),
                  pl.BlockSpec((k, tn), lambda i, j: (0, j)),
                  pl.BlockSpec((1, tn), lambda i, j: (0, j)),
                  pl.BlockSpec((tm, LANES), lambda i, j: (i % nsb, 0)),
                  pl.BlockSpec((tm, LANES), lambda i, j: (i % nsb, 0)),
                  pl.BlockSpec((LANES, LANES), lambda i, j: (0, 0))],
        out_specs=out_spec,
        out_shape=out_shape,
        scratch_shapes=scratch,
        compiler_params=_cparams(("parallel", "arbitrary")),
        name="norm_matmul_rope",
    )(x, g.reshape(1, k), w, head_gain.reshape(1, n), cos, sin, _seg64_ones())


def _mmres_kernel(a_ref, w_ref, h_ref, o_ref):
    o_ref[...] = h_ref[...] + _dot(a_ref[...], w_ref[...])


def matmul_residual(a, w, h, tm=512):
    t, k = a.shape
    n = w.shape[1]
    return pl.pallas_call(
        _mmres_kernel,
        grid=(t // tm,),
        in_specs=[pl.BlockSpec((tm, k), lambda i: (i, 0)),
                  pl.BlockSpec((k, n), lambda i: (0, 0)),
                  pl.BlockSpec((tm, n), lambda i: (i, 0))],
        out_specs=pl.BlockSpec((tm, n), lambda i: (i, 0)),
        out_shape=jax.ShapeDtypeStruct((t, n), F32),
        compiler_params=_cparams(("parallel",)),
        name="matmul_residual",
    )(a, w, h)


def _ffn_kernel(x_ref, g_ref, wg_ref, wu_ref, wo_ref, o_ref, xn_ref):
    @pl.when(pl.program_id(1) == 0)
    def _():
        x = x_ref[...]
        xn_ref[...] = _rms_rows(x, g_ref[...]).astype(BF16)
        o_ref[...] = x

    xn = xn_ref[...]
    gate = _dot(xn, wg_ref[...])
    up = _dot(xn, wu_ref[...])
    a = (_silu(gate) * up).astype(BF16)
    o_ref[...] += _dot(a, wo_ref[...])


def ffn_residual(h, g, w_in, w_out, tm=1024, tf=256):
    t, k = h.shape
    nf = D_FF // tf
    return pl.pallas_call(
        _ffn_kernel,
        grid=(t // tm, nf),
        in_specs=[pl.BlockSpec((tm, k), lambda i, f: (i, 0)),
                  pl.BlockSpec((1, k), lambda i, f: (0, 0)),
                  pl.BlockSpec((k, tf), lambda i, f: (0, f)),
                  pl.BlockSpec((k, tf), lambda i, f: (0, nf + f)),
                  pl.BlockSpec((tf, k), lambda i, f: (f, 0))],
        out_specs=pl.BlockSpec((tm, k), lambda i, f: (i, 0)),
        out_shape=jax.ShapeDtypeStruct((t, k), F32),
        scratch_shapes=[pltpu.VMEM((tm, k), BF16)],
        compiler_params=_cparams(("parallel", "arbitrary")),
        name="ffn",
    )(h, g.reshape(1, k), w_in, w_in, w_out)


def _ple_kernel(x_ref, g_ref, wg_ref, p_ref, wp_ref, o_ref):
    x = x_ref[...]
    xn = _rms_rows(x, g_ref[...]).astype(BF16)
    gate = jax.nn.sigmoid(_dot(xn, wg_ref[...]))
    proj = _dot(p_ref[...].astype(BF16), wp_ref[...])
    o_ref[...] = x + gate * proj


def ple_residual(h, g, w_gate, p, w_proj, tm=512):
    t, k = h.shape
    return pl.pallas_call(
        _ple_kernel,
        grid=(t // tm,),
        in_specs=[pl.BlockSpec((tm, k), lambda i: (i, 0)),
                  pl.BlockSpec((1, k), lambda i: (0, 0)),
                  pl.BlockSpec((k, k), lambda i: (0, 0)),
                  pl.BlockSpec((tm, PLE_DIM), lambda i: (i, 0)),
                  pl.BlockSpec((PLE_DIM, k), lambda i: (0, 0))],
        out_specs=pl.BlockSpec((tm, k), lambda i: (i, 0)),
        out_shape=jax.ShapeDtypeStruct((t, k), F32),
        compiler_params=_cparams(("parallel",)),
        name="ple",
    )(h, g.reshape(1, k), w_gate, p, w_proj)


def _gla_kernel(q_ref, k_ref, v_ref, z_ref, wg_ref, bg_ref, cm_ref, mk_ref, o_ref,
                st_ref, la_ref, *, nchunk):
    d = pl.program_id(0)

    @pl.when(pl.program_id(3) == 0)
    def _():
        st_ref[...] = jnp.zeros_like(st_ref)

    x = _dot(z_ref[...].astype(BF16), wg_ref[...]) + bg_ref[...]
    la_ref[...] = (jnp.minimum(x, 0.0) - jnp.log(1.0 + jnp.exp(-jnp.abs(x)))) * (1.0 / GLA_TAU)
    cm = cm_ref[...]
    mk = mk_ref[...]
    c = GLA_CHUNK
    for cc in range(nchunk):
        ci = cc + d * (nchunk - 1 - 2 * cc)
        rows = pl.ds(pl.multiple_of(ci * c, c), c)
        la = la_ref[rows, :]
        b = _dot_split(cm, la)
        tot = jnp.sum(la, axis=0, keepdims=True)
        qc = q_ref[rows, :].astype(F32)
        kc = k_ref[rows, :].astype(F32)
        vc = v_ref[rows, :]
        q_in = (qc * jnp.exp(b) * (GLA_HK ** -0.5)).astype(BF16)
        k_out = (kc * jnp.exp(-b)).astype(BF16)
        k_dec = (kc * jnp.exp(tot - b)).astype(BF16)
        sc = (_dot_nt(q_in, k_out) * mk).astype(BF16)
        st = st_ref[...]
        o = _dot(sc, vc) + _dot_nt(q_in, st.astype(BF16))
        o_ref[rows, :] = o.astype(o_ref.dtype)
        st_ref[...] = st * jnp.exp(tot) + _dot_tn(vc, k_dec)


def gla_core(proj, z, wg, bg, bsz, seq, rows=512):
    t = bsz * seq
    nblk = seq // rows
    c = GLA_CHUNK
    ii = jnp.arange(c)
    lower = ii[:, None] >= ii[None, :]
    cm = jnp.stack([lower, lower.T]).astype(BF16)
    mk = jnp.stack([lower, ii[:, None] < ii[None, :]]).astype(F32)

    def rb(d, b, i):
        return b * nblk + i + d * (nblk - 1 - 2 * i)

    kern = functools.partial(_gla_kernel, nchunk=rows // c)
    return pl.pallas_call(
        kern,
        grid=(2, bsz, GLA_HEADS, nblk),
        in_specs=[pl.BlockSpec((rows, GLA_HK), lambda d, b, h, i: (rb(d, b, i), h)),
                  pl.BlockSpec((rows, GLA_HK), lambda d, b, h, i: (rb(d, b, i), 4 + h)),
                  pl.BlockSpec((rows, GLA_HV), lambda d, b, h, i: (rb(d, b, i), 4 + h)),
                  pl.BlockSpec((rows, LANES), lambda d, b, h, i: (rb(d, b, i), 0)),
                  pl.BlockSpec((None, LANES, GLA_HK), lambda d, b, h, i: (d, 0, h)),
                  pl.BlockSpec((None, 1, GLA_HK), lambda d, b, h, i: (d, 0, h)),
                  pl.BlockSpec((None, c, c), lambda d, b, h, i: (d, 0, 0)),
                  pl.BlockSpec((None, c, c), lambda d, b, h, i: (d, 0, 0))],
        out_specs=pl.BlockSpec((None, rows, GLA_HV), lambda d, b, h, i: (d, rb(d, b, i), h)),
        out_shape=jax.ShapeDtypeStruct((2, t, GLA_DV), BF16),
        scratch_shapes=[pltpu.VMEM((GLA_HV, GLA_HK), F32), pltpu.VMEM((rows, GLA_HK), F32)],
        compiler_params=_cparams(("parallel", "parallel", "parallel", "arbitrary")),
        name="gla_core",
    )(proj, proj, proj, z, wg, bg, cm, mk)


def _gla_out_kernel(of_ref, ob_ref, r_ref, g_ref, w_ref, h_ref, o_ref):
    o = of_ref[...].astype(F32) + ob_ref[...].astype(F32)
    parts = []
    for hh in range(GLA_HEADS):
        oh = o[:, hh * GLA_HV:(hh + 1) * GLA_HV]
        ms = jnp.mean(oh * oh, axis=-1, keepdims=True)
        parts.append(oh * lax.rsqrt(ms + EPS))
    on = jnp.concatenate(parts, axis=1) * g_ref[...]
    a = (on * _silu(r_ref[...].astype(F32))).astype(BF16)
    o_ref[...] = h_ref[...] + _dot(a, w_ref[...])


def gla_out(o2, proj, g_out, w_out, h, tm=512):
    t, n = h.shape
    return pl.pallas_call(
        _gla_out_kernel,
        grid=(t // tm,),
        in_specs=[pl.BlockSpec((None, tm, GLA_DV), lambda i: (0, i, 0)),
                  pl.BlockSpec((None, tm, GLA_DV), lambda i: (1, i, 0)),
                  pl.BlockSpec((tm, GLA_DV), lambda i: (i, 2)),
                  pl.BlockSpec((1, GLA_DV), lambda i: (0, 0)),
                  pl.BlockSpec((GLA_DV, n), lambda i: (0, 0)),
                  pl.BlockSpec((tm, n), lambda i: (i, 0))],
        out_specs=pl.BlockSpec((tm, n), lambda i: (i, 0)),
        out_shape=jax.ShapeDtypeStruct((t, n), F32),
        compiler_params=_cparams(("parallel",)),
        name="gla_out",
    )(o2, o2, proj, jnp.tile(g_out, GLA_HEADS).reshape(1, GLA_DV), w_out, h)


def gla_layer(h, norm_g, w_in, w_gate_f, b_gate_f, w_gate_b, b_gate_b, g_out, w_out, bsz, seq):
    n_main = 2 * GLA_DK + 2 * GLA_DV
    proj = norm_matmul(h, norm_g, w_in[:, :n_main].astype(BF16), BF16, tn=1024)
    wz = jnp.pad(w_in[:, n_main:], ((0, 0), (0, LANES - 2 * GLA_RANK))).astype(BF16)
    z = norm_matmul(h, norm_g, wz, F32)
    wg = jnp.zeros((2, LANES, GLA_DK), F32)
    wg = wg.at[0, :GLA_RANK].set(w_gate_f).at[1, GLA_RANK:2 * GLA_RANK].set(w_gate_b).astype(BF16)
    bg = jnp.stack([b_gate_f, b_gate_b]).reshape(2, 1, GLA_DK)
    o2 = gla_core(proj, z, wg, bg, bsz, seq)
    return gla_out(o2, proj, g_out, w_out.astype(BF16), h)


def _diff_kernel(q_ref, k_ref, v_ref, lam_ref, gs_ref, o_ref, qz_ref, m_ref, acc_ref, *,
                 tk, lambda_init):
    tq = q_ref.shape[0]
    q = q_ref[...]
    lane = lax.broadcasted_iota(jnp.int32, (tq, LANES), 1)
    zero = jnp.zeros_like(q)
    qz_ref[0] = jnp.where(lane < DIFF_HD, q, zero)
    qz_ref[1] = jnp.where(lane >= DIFF_HD, q, zero)
    m_ref[...] = jnp.full_like(m_ref, NEG_BIG)
    acc_ref[...] = jnp.zeros_like(acc_ref)
    ones = jnp.ones((tk, LANES), BF16)

    def body(ki, carry):
        rows = pl.ds(pl.multiple_of(ki * tk, tk), tk)
        kb = k_ref[rows, :]
        vaug = jnp.concatenate([v_ref[rows, :], ones], axis=1)
        for t in range(2):
            s = _dot_nt(qz_ref[t], kb)
            m_prev = m_ref[t]
            m_new = jnp.maximum(m_prev, jnp.max(s, axis=1, keepdims=True))
            alpha = jnp.exp2(m_prev - m_new)
            p = jnp.exp2(s - pltpu.repeat(m_new, tk // LANES, axis=1))
            acc_ref[t] = pltpu.repeat(alpha, 2, axis=1) * acc_ref[t] + _dot(p.astype(BF16), vaug)
            m_ref[t] = m_new
        return carry

    lax.fori_loop(0, k_ref.shape[0] // tk, body, 0)
    _diff_finish(acc_ref, lam_ref, gs_ref, o_ref, lambda_init)


def _diff_finish(acc_ref, lam_ref, gs_ref, o_ref, lambda_init):
    lam4 = lam_ref[...]
    lam = (jnp.exp(jnp.sum(lam4[0:1] * lam4[1:2], axis=1, keepdims=True))
           - jnp.exp(jnp.sum(lam4[2:3] * lam4[3:4], axis=1, keepdims=True)) + lambda_init)
    a1 = acc_ref[0]
    a2 = acc_ref[1]
    o = a1[:, :LANES] / a1[:, LANES:] - lam * (a2[:, :LANES] / a2[:, LANES:])
    ms = jnp.mean(o * o, axis=-1, keepdims=True)
    o_ref[...] = (o * lax.rsqrt(ms + EPS) * gs_ref[...] * (1.0 - lambda_init)).astype(o_ref.dtype)


def _diff_bounded_kernel(q_ref, k_ref, v_ref, lam_ref, gs_ref, o_ref, qz_ref, acc_ref, *,
                         tk, lambda_init):
    tq = q_ref.shape[0]
    q = q_ref[...]
    lane = lax.broadcasted_iota(jnp.int32, (tq, LANES), 1)
    zero = jnp.zeros_like(q)
    qz_ref[0] = jnp.where(lane < DIFF_HD, q, zero)
    qz_ref[1] = jnp.where(lane >= DIFF_HD, q, zero)
    acc_ref[...] = jnp.zeros_like(acc_ref)
    ones = jnp.ones((tk, LANES), BF16)

    def body(ki, carry):
        rows = pl.ds(pl.multiple_of(ki * tk, tk), tk)
        kb = k_ref[rows, :]
        vaug = jnp.concatenate([v_ref[rows, :], ones], axis=1)
        for t in range(2):
            p = jnp.exp2(_dot_nt(qz_ref[t], kb)).astype(BF16)
            acc_ref[t] += _dot(p, vaug)
        return carry

    lax.fori_loop(0, k_ref.shape[0] // tk, body, 0, unroll=4)
    _diff_finish(acc_ref, lam_ref, gs_ref, o_ref, lambda_init)


DIFF_SAFE_LOG2 = 60.0


def diff_core(proj, lam4, g_sub, lambda_init, score_bound, bsz, seq, tq=512, tk=512):
    t = bsz * seq
    nq = seq // tq
    in_specs = [pl.BlockSpec((tq, LANES), lambda b, h, i: (b * nq + i, h)),
                pl.BlockSpec((seq, LANES), lambda b, h, i: (b, 8 + h)),
                pl.BlockSpec((seq, LANES), lambda b, h, i: (b, 16 + h)),
                pl.BlockSpec((4, DIFF_HD), lambda b, h, i: (0, 0)),
                pl.BlockSpec((1, LANES), lambda b, h, i: (0, 0))]
    out_spec = pl.BlockSpec((tq, LANES), lambda b, h, i: (b * nq + i, h))
    out_shape = jax.ShapeDtypeStruct((t, D_MODEL), BF16)
    args = (proj, proj, proj, lam4, g_sub.reshape(1, LANES))

    def online(*a):
        return pl.pallas_call(
            functools.partial(_diff_kernel, tk=tk, lambda_init=lambda_init),
            grid=(bsz, DIFF_HEADS, nq), in_specs=in_specs, out_specs=out_spec, out_shape=out_shape,
            scratch_shapes=[pltpu.VMEM((2, tq, LANES), BF16),
                            pltpu.VMEM((2, tq, LANES), F32),
                            pltpu.VMEM((2, tq, 2 * LANES), F32)],
            compiler_params=_cparams(("parallel", "parallel", "arbitrary")),
            name="diff_core_online",
        )(*a)

    def bounded(*a):
        return pl.pallas_call(
            functools.partial(_diff_bounded_kernel, tk=tk, lambda_init=lambda_init),
            grid=(bsz, DIFF_HEADS, nq), in_specs=in_specs, out_specs=out_spec, out_shape=out_shape,
            scratch_shapes=[pltpu.VMEM((2, tq, LANES), BF16),
                            pltpu.VMEM((2, tq, 2 * LANES), F32)],
            compiler_params=_cparams(("parallel", "parallel", "arbitrary")),
            name="diff_core_bounded",
        )(*a)

    return lax.cond(score_bound <= DIFF_SAFE_LOG2, bounded, online, *args)


def diff_layer(h, norm_g, layer_idx, w_in, g_q, g_k, lam_q1, lam_k1, lam_q2, lam_k2, g_sub, w_out,
               bsz, seq):
    lambda_init = 0.8 - 0.6 * math.exp(-0.3 * layer_idx)
    qscale = (DIFF_HD ** -0.5) * LOG2E
    head_gain = jnp.concatenate([jnp.tile(g_q.reshape(-1), DIFF_HEADS) * qscale,
                                 jnp.tile(g_k.reshape(-1), DIFF_HEADS),
                                 jnp.ones((D_MODEL,), F32)])
    proj = norm_matmul_rope(h, norm_g, w_in.astype(BF16), head_gain, 2 * D_MODEL, seq)
    lam4 = jnp.stack([lam_q1, lam_k1, lam_q2, lam_k2])
    score_bound = 1.05 * DIFF_HD * qscale * jnp.max(jnp.abs(g_q)) * jnp.max(jnp.abs(g_k))
    o = diff_core(proj, lam4, g_sub, lambda_init, score_bound, bsz, seq)
    return matmul_residual(o, w_out.astype(BF16), h)


def _conv_kernel(prev_ref, cur_ref, next_ref, w_ref, b_ref, o_ref, buf_ref, *, halo):
    i = pl.program_id(1)
    ts = cur_ref.shape[0]
    prev = prev_ref[...].astype(F32)
    nxt = next_ref[...].astype(F32)
    buf_ref[0:halo, :] = jnp.where(i == 0, 0.0, prev)
    buf_ref[halo:halo + ts, :] = cur_ref[...].astype(F32)
    buf_ref[halo + ts:, :] = jnp.where(i == pl.num_programs(1) - 1, 0.0, nxt)
    w = w_ref[...]
    y = b_ref[...] + w[2:3] * buf_ref[halo:halo + ts, :]
    for k in (0, 1, 3, 4):
        y = y + w[k:k + 1] * buf_ref[pl.ds(halo + k - 2, ts), :]
    o_ref[...] = _silu(y).astype(o_ref.dtype)


def ssd_conv(zx, conv_w, conv_b, bsz, seq, ts=512, tc=512):
    t = bsz * seq
    halo = 16
    nrb = seq // ts
    hb = ts // halo
    nhb = seq // halo
    off = SSD_INNER // tc
    kern = functools.partial(_conv_kernel, halo=halo)
    wpad = jnp.pad(conv_w, ((0, 8 - SSD_CONV), (0, 0)))
    return pl.pallas_call(
        kern,
        grid=(bsz, nrb, SSD_CONV_DIM // tc),
        in_specs=[pl.BlockSpec((halo, tc), lambda b, i, c: (b * nhb + jnp.maximum(i * hb - 1, 0), off + c)),
                  pl.BlockSpec((ts, tc), lambda b, i, c: (b * nrb + i, off + c)),
                  pl.BlockSpec((halo, tc), lambda b, i, c: (b * nhb + jnp.minimum((i + 1) * hb, nhb - 1), off + c)),
                  pl.BlockSpec((8, tc), lambda b, i, c: (0, c)),
                  pl.BlockSpec((1, tc), lambda b, i, c: (0, c))],
        out_specs=pl.BlockSpec((ts, tc), lambda b, i, c: (b * nrb + i, c)),
        out_shape=jax.ShapeDtypeStruct((t, SSD_CONV_DIM), BF16),
        scratch_shapes=[pltpu.VMEM((ts + 2 * halo, tc), F32)],
        compiler_params=_cparams(("parallel", "parallel", "parallel")),
        name="ssd_conv",
    )(zx, zx, zx, wpad, conv_b.reshape(1, -1))


def _ssd_kernel(x_ref, b_ref, c_ref, dt_ref, bias_ref, aexp_ref, e2_ref, e5_ref, selt_ref,
                cmr_ref, cmc_ref, mk_ref, o_ref, st_ref, *, nchunk):
    d = pl.program_id(0)

    @pl.when(pl.program_id(3) == 0)
    def _():
        st_ref[...] = jnp.zeros_like(st_ref)

    c = SSD_CHUNK
    cmr = cmr_ref[...]
    cmc = cmc_ref[...]
    mk = mk_ref[...]
    e2 = e2_ref[...]
    e5 = e5_ref[...]
    selt = selt_ref[...]
    hp = 4 * SSD_P
    lane = lax.broadcasted_iota(jnp.int32, (c, hp), 1)
    for cc in range(nchunk):
        ci = cc + d * (nchunk - 1 - 2 * cc)
        rows = pl.ds(pl.multiple_of(ci * c, c), c)
        xr = dt_ref[rows, :] + bias_ref[...]
        dt = jnp.maximum(xr, 0.0) + jnp.log(1.0 + jnp.exp(-jnp.abs(xr)))
        hi = dt.astype(BF16)
        lo = (dt - hi.astype(F32)).astype(BF16)
        dt2 = _dot(hi, e2) + _dot(lo, e2)
        adt = dt * aexp_ref[...]
        ahi = adt.astype(BF16)
        alo = (adt - ahi.astype(F32)).astype(BF16)
        ad2 = _dot(ahi, e2) + _dot(alo, e2)
        ad5 = _dot(ahi, e5) + _dot(alo, e5)
        adr = _dot_nt(selt, ahi) + _dot_nt(selt, alo)
        ac2 = _dot_split(cmr, ad2)
        ac5 = _dot_split(cmr, ad5)
        rhi = adr.astype(BF16)
        rlo = (adr - rhi.astype(F32)).astype(BF16)
        acr = _dot(rhi, cmc) + _dot(rlo, cmc)
        tot2 = jnp.sum(ad2, axis=0, keepdims=True)

        x = x_ref[rows, :].astype(F32)
        bm = b_ref[rows, :]
        cmat = c_ref[rows, :]
        xdt = x * dt2
        xdt_b = xdt.astype(BF16)
        cb = _dot_nt(cmat, bm) * mk
        y = _dot(cmat, st_ref[...].astype(BF16)) * jnp.exp(ac2)
        for hh in range(4):
            col = ac5[:, hh * LANES:(hh + 1) * LANES]
            row = acr[hh:hh + 1, :]
            m = (cb * jnp.exp(jnp.minimum(col - row, 0.0))).astype(BF16)
            xh = jnp.where((lane >= hh * SSD_P) & (lane < (hh + 1) * SSD_P), xdt_b,
                           jnp.zeros_like(xdt_b))
            y = y + _dot(m, xh)
        o_ref[rows, :] = y.astype(o_ref.dtype)
        xw = (xdt * jnp.exp(tot2 - ac2)).astype(BF16)
        st_ref[...] = st_ref[...] * jnp.exp(tot2) + _dot_tn(bm, xw)


def ssd_core(xbc, dtr, dt_bias, a_neg, bsz, seq, rows=256):
    t = bsz * seq
    nblk = seq // rows
    c = SSD_CHUNK
    ii = jnp.arange(c)
    lower = ii[:, None] >= ii[None, :]
    cmr = jnp.stack([lower, lower.T]).astype(BF16)
    cmc = jnp.stack([lower.T, lower]).astype(BF16)
    mk = jnp.stack([lower, ii[:, None] < ii[None, :]]).astype(F32)
    src = (jnp.arange(2)[:, None, None] * SSD_HEADS + 4 * jnp.arange(SSD_GROUPS)[None, :, None]
           + jnp.arange(4)[None, None, :])
    lanes = jnp.arange(LANES)
    onehot = (lanes[None, None, None, :] == src[..., None]).astype(BF16)
    e2 = jnp.repeat(onehot, SSD_P, axis=2).transpose(0, 1, 3, 2)
    e5 = jnp.repeat(onehot, LANES, axis=2).transpose(0, 1, 3, 2)
    selt = jnp.pad(onehot, ((0, 0), (0, 0), (0, 4), (0, 0)))
    e2 = e2.reshape(2 * SSD_GROUPS, LANES, 4 * SSD_P)
    e5 = e5.reshape(2 * SSD_GROUPS, LANES, 4 * LANES)
    selt = selt.reshape(2 * SSD_GROUPS, 8, LANES)
    bias = jnp.pad(dt_bias, (0, LANES - 2 * SSD_HEADS)).reshape(1, LANES)
    aexp = jnp.pad(a_neg, (0, LANES - 2 * SSD_HEADS)).reshape(1, LANES)

    def rb(d, b, i):
        return b * nblk + i + d * (nblk - 1 - 2 * i)

    kern = functools.partial(_ssd_kernel, nchunk=rows // c)
    hp = 4 * SSD_P
    return pl.pallas_call(
        kern,
        grid=(2, bsz, SSD_GROUPS, nblk),
        in_specs=[pl.BlockSpec((rows, hp), lambda d, b, g, i: (rb(d, b, i), g)),
                  pl.BlockSpec((rows, SSD_N), lambda d, b, g, i: (rb(d, b, i), 16 + g)),
                  pl.BlockSpec((rows, SSD_N), lambda d, b, g, i: (rb(d, b, i), 24 + g)),
                  pl.BlockSpec((rows, LANES), lambda d, b, g, i: (rb(d, b, i), 0)),
                  pl.BlockSpec((1, LANES), lambda d, b, g, i: (0, 0)),
                  pl.BlockSpec((1, LANES), lambda d, b, g, i: (0, 0)),
                  pl.BlockSpec((None, LANES, hp), lambda d, b, g, i: (d * SSD_GROUPS + g, 0, 0)),
                  pl.BlockSpec((None, LANES, 4 * LANES), lambda d, b, g, i: (d * SSD_GROUPS + g, 0, 0)),
                  pl.BlockSpec((None, 8, LANES), lambda d, b, g, i: (d * SSD_GROUPS + g, 0, 0)),
                  pl.BlockSpec((None, c, c), lambda d, b, g, i: (d, 0, 0)),
                  pl.BlockSpec((None, c, c), lambda d, b, g, i: (d, 0, 0)),
                  pl.BlockSpec((None, c, c), lambda d, b, g, i: (d, 0, 0))],
        out_specs=pl.BlockSpec((None, rows, hp), lambda d, b, g, i: (d, rb(d, b, i), g)),
        out_shape=jax.ShapeDtypeStruct((2, t, SSD_INNER), BF16),
        scratch_shapes=[pltpu.VMEM((SSD_N, hp), F32)],
        compiler_params=_cparams(("parallel", "parallel", "parallel", "arbitrary")),
        name="ssd_core",
    )(xbc, xbc, xbc, dtr, bias, aexp, e2, e5, selt, cmr, cmc, mk)


def _ssd_out_kernel(yf_ref, yb_ref, x_ref, z_ref, dsk_ref, g_ref, w_ref, h_ref, o_ref):
    y = yf_ref[...].astype(F32) + yb_ref[...].astype(F32) + x_ref[...].astype(F32) * dsk_ref[...]
    y = y * _silu(z_ref[...].astype(F32))
    gw = SSD_INNER // SSD_GROUPS
    parts = []
    for gg in range(SSD_GROUPS):
        yg = y[:, gg * gw:(gg + 1) * gw]
        ms = jnp.mean(yg * yg, axis=-1, keepdims=True)
        parts.append(yg * lax.rsqrt(ms + EPS))
    a = (jnp.concatenate(parts, axis=1) * g_ref[...]).astype(BF16)
    o_ref[...] = h_ref[...] + _dot(a, w_ref[...])


def ssd_out(y2, xbc, zx, d_skip, g_norm, w_out, h, tm=512):
    t, n = h.shape
    return pl.pallas_call(
        _ssd_out_kernel,
        grid=(t // tm,),
        in_specs=[pl.BlockSpec((None, tm, SSD_INNER), lambda i: (0, i, 0)),
                  pl.BlockSpec((None, tm, SSD_INNER), lambda i: (1, i, 0)),
                  pl.BlockSpec((tm, SSD_INNER), lambda i: (i, 0)),
                  pl.BlockSpec((tm, SSD_INNER), lambda i: (i, 0)),
                  pl.BlockSpec((1, SSD_INNER), lambda i: (0, 0)),
                  pl.BlockSpec((1, SSD_INNER), lambda i: (0, 0)),
                  pl.BlockSpec((SSD_INNER, n), lambda i: (0, 0)),
                  pl.BlockSpec((tm, n), lambda i: (i, 0))],
        out_specs=pl.BlockSpec((tm, n), lambda i: (i, 0)),
        out_shape=jax.ShapeDtypeStruct((t, n), F32),
        compiler_params=_cparams(("parallel",)),
        name="ssd_out",
    )(y2, y2, xbc, zx, jnp.repeat(d_skip, SSD_P).reshape(1, SSD_INNER),
      g_norm.reshape(1, SSD_INNER), w_out, h)


def ssd_layer(h, norm_g, w_in, conv_w, conv_b, dt_bias_f, a_log_f, dt_bias_b, a_log_b, d_skip,
              g_norm, w_out, bsz, seq):
    n_main = SSD_INNER + SSD_CONV_DIM
    zx = norm_matmul(h, norm_g, w_in[:, :n_main].astype(BF16), BF16, tn=1024)
    wdt = jnp.pad(w_in[:, n_main:], ((0, 0), (0, LANES - 2 * SSD_HEADS))).astype(BF16)
    dtr = norm_matmul(h, norm_g, wdt, F32)
    xbc = ssd_conv(zx, conv_w, conv_b, bsz, seq)
    dt_bias = jnp.concatenate([dt_bias_f, dt_bias_b])
    a_neg = -jnp.exp(jnp.concatenate([a_log_f, a_log_b]))
    y2 = ssd_core(xbc, dtr, dt_bias, a_neg, bsz, seq)
    return ssd_out(y2, xbc, zx, d_skip, g_norm, w_out.astype(BF16), h)


def _dil_kernel(q_ref, kp_ref, kc_ref, kn_ref, vp_ref, vc_ref, vn_ref, o_ref, lse_ref):
    ai = pl.program_id(2)
    na = pl.num_programs(2)
    tq = q_ref.shape[0]
    side = DIL_SIDE
    kcat = jnp.concatenate([kp_ref[...], kc_ref[...], kn_ref[...]], axis=0)
    vcat = jnp.concatenate([vp_ref[...], vc_ref[...], vn_ref[...]], axis=0)
    nk = tq + 2 * side
    ii = lax.broadcasted_iota(jnp.int32, (tq, nk), 0)
    jj = lax.broadcasted_iota(jnp.int32, (tq, nk), 1)
    kpos = ai * tq - side + jj
    valid = (jj >= ii) & (jj <= ii + 2 * side) & (kpos >= 0) & (kpos < na * tq)
    lane = lax.broadcasted_iota(jnp.int32, (tq, LANES), 1)
    lo_q = lane < DIL_HD
    lane_k = lax.broadcasted_iota(jnp.int32, (nk, LANES), 1)
    lo_k = lane_k < DIL_HD
    one = jnp.ones((nk, LANES), BF16)
    for hp in range(DIL_HEADS // 2):
        sl = slice(hp * LANES, (hp + 1) * LANES)
        q = q_ref[:, sl]
        k = kcat[:, sl]
        v = vcat[:, sl]
        zq = jnp.zeros_like(q)
        accs = []
        ms = []
        for half in range(2):
            qm = jnp.where(lo_q if half == 0 else ~lo_q, q, zq)
            s = jnp.where(valid, _dot_nt(qm, k), NEG_BIG)
            m = jnp.max(s, axis=1, keepdims=True)
            p = jnp.exp2(s - m).astype(BF16)
            vm = jnp.where(lo_k if half == 0 else ~lo_k, v, one)
            accs.append(_dot(p, vm))
            ms.append(m)
        num = jnp.where(lo_q, accs[0], accs[1])
        den = jnp.where(lo_q, pltpu.roll(accs[0], DIL_HD, 1), pltpu.roll(accs[1], DIL_HD, 1))
        mm = jnp.where(lo_q, ms[0], ms[1])
        o_ref[:, sl] = (num / den).astype(o_ref.dtype)
        lse_ref[:, sl] = (mm * (1.0 / LOG2E)) + jnp.log(den)


def dil_core(pg, dilation, bsz, seq, tq=128):
    a = seq // dilation
    na = a // tq
    side = DIL_SIDE
    nst = seq // DIL_SUPER
    per = DIL_SUPER // dilation // tq
    per_h = DIL_SUPER // dilation // side
    nhb = a // side

    def cur(col):
        return pl.BlockSpec((None, None, tq, DIL_WIDTH),
                            lambda b, r, i: (b * nst + i // per, r, i % per, col))

    def halo(col, which):
        def imap(b, r, i):
            hb = (jnp.maximum(2 * i - 1, 0) if which == 0 else jnp.minimum(2 * i + 2, nhb - 1))
            return (b * nst + hb // per_h, r, hb % per_h, col)
        return pl.BlockSpec((None, None, side, DIL_WIDTH), imap)

    out_spec = pl.BlockSpec((None, None, tq, DIL_WIDTH), lambda b, r, i: (b * nst + i // per, r, i % per, 0))
    oshape = (bsz * nst, dilation, DIL_SUPER // dilation, DIL_WIDTH)
    return pl.pallas_call(
        _dil_kernel,
        grid=(bsz, dilation, na),
        in_specs=[cur(0), halo(1, 0), cur(1), halo(1, 1), halo(2, 0), cur(2), halo(2, 1)],
        out_specs=[out_spec, out_spec],
        out_shape=[jax.ShapeDtypeStruct(oshape, BF16), jax.ShapeDtypeStruct(oshape, F32)],
        compiler_params=_cparams(("parallel", "parallel", "parallel")),
        name=f"dil_core_{dilation}",
    )(pg, pg, pg, pg, pg, pg, pg)


def _dil_out_kernel(o0_ref, o1_ref, o2_ref, l0_ref, l1_ref, l2_ref, w_ref, h_ref, o_ref,
                    so_ref, sl_ref):
    tm = h_ref.shape[0]
    nch = DIL_WIDTH // LANES
    for gi, (og_ref, lg_ref) in enumerate(((o1_ref, l1_ref), (o2_ref, l2_ref))):
        d = DIL_PAIRS[gi + 1][1]
        n = tm // d
        for r in range(d):
            ov = og_ref[r].astype(F32)
            lv = lg_ref[r]
            for c in range(nch):
                so_ref[gi, c, pl.ds(r, n, stride=d), :] = ov[:, c * LANES:(c + 1) * LANES]
                sl_ref[gi, c, pl.ds(r, n, stride=d), :] = lv[:, c * LANES:(c + 1) * LANES]
    parts = []
    for c in range(nch):
        sl = slice(c * LANES, (c + 1) * LANES)
        l0 = l0_ref[0, :, sl]
        l1 = sl_ref[0, c]
        l2 = sl_ref[1, c]
        mx = jnp.maximum(jnp.maximum(l0, l1), l2)
        e0 = jnp.exp(l0 - mx)
        e1 = jnp.exp(l1 - mx)
        e2 = jnp.exp(l2 - mx)
        num = e0 * o0_ref[0, :, sl].astype(F32) + e1 * so_ref[0, c] + e2 * so_ref[1, c]
        parts.append((num / (e0 + e1 + e2)).astype(BF16))
    o_ref[...] = h_ref[...] + _dot(jnp.concatenate(parts, axis=1), w_ref[...])


def dil_out(os, lses, w_out, h, tm=512):
    t, n = h.shape
    per = DIL_SUPER // tm

    def grp(d):
        return pl.BlockSpec((None, d, tm // d, DIL_WIDTH), lambda i: (i // per, 0, i % per, 0))

    specs = [grp(d) for _, d in DIL_PAIRS]
    row = pl.BlockSpec((tm, n), lambda i: (i, 0))
    return pl.pallas_call(
        _dil_out_kernel,
        grid=(t // tm,),
        in_specs=specs + specs + [pl.BlockSpec((DIL_WIDTH, n), lambda i: (0, 0)), row],
        out_specs=row,
        out_shape=jax.ShapeDtypeStruct((t, n), F32),
        scratch_shapes=[pltpu.VMEM((2, DIL_WIDTH // LANES, tm, LANES), F32),
                        pltpu.VMEM((2, DIL_WIDTH // LANES, tm, LANES), F32)],
        compiler_params=_cparams(("parallel",)),
        name="dil_out",
    )(*os, *lses, w_out, h)


def dil_layer(h, norm_g, w_in, g_q, g_k, w_out, bsz, seq):
    qscale = (DIL_HD ** -0.5) * LOG2E
    w4 = w_in.reshape(D_MODEL, 3, len(DIL_PAIRS), DIL_WIDTH)
    os, lses = [], []
    for gi, (_, dilation) in enumerate(DIL_PAIRS):
        wg = w4[:, :, gi, :].reshape(D_MODEL, 3 * DIL_WIDTH).astype(BF16)
        head_gain = jnp.concatenate([jnp.tile(g_q[gi], DIL_HEADS) * qscale,
                                     jnp.tile(g_k[gi], DIL_HEADS),
                                     jnp.ones((DIL_WIDTH,), F32)])
        pg = norm_matmul_rope(h, norm_g, wg, head_gain, 2 * DIL_WIDTH, seq, dil=dilation)
        o, lse = dil_core(pg, dilation, bsz, seq)
        os.append(o)
        lses.append(lse)
    return dil_out(os, lses, w_out.astype(BF16), h)


def kernel(x, p, norm_mix, norm_ffn, ffn_w_in, ffn_w_out, ple_norm, ple_w_gate, ple_w_proj, gla_w_in, gla_w_gate_f, gla_b_gate_f, gla_w_gate_b, gla_b_gate_b, gla_g_out, gla_w_out, diff_w_in, diff_g_q, diff_g_k, diff_lam_q1, diff_lam_k1, diff_lam_q2, diff_lam_k2, diff_g_sub, diff_w_out, ssd_w_in, ssd_conv_w, ssd_conv_b, ssd_dt_bias_f, ssd_a_log_f, ssd_dt_bias_b, ssd_a_log_b, ssd_d, ssd_g_norm, ssd_w_out, dil_w_in, dil_g_q, dil_g_k, dil_w_out):
    bsz, seq, dm = x.shape
    t = bsz * seq
    h = x.reshape(t, dm)
    for i in range(DEPTH):
        kind = i % 4
        j = i // 4
        if kind == 0:
            h = gla_layer(h, norm_mix[i], gla_w_in[j], gla_w_gate_f[j], gla_b_gate_f[j],
                          gla_w_gate_b[j], gla_b_gate_b[j], gla_g_out[j], gla_w_out[j], bsz, seq)
        elif kind == 1:
            h = diff_layer(h, norm_mix[i], i, diff_w_in[j], diff_g_q[j], diff_g_k[j], diff_lam_q1[j],
                           diff_lam_k1[j], diff_lam_q2[j], diff_lam_k2[j], diff_g_sub[j],
                           diff_w_out[j], bsz, seq)
        elif kind == 2:
            h = ssd_layer(h, norm_mix[i], ssd_w_in[j], ssd_conv_w[j], ssd_conv_b[j],
                          ssd_dt_bias_f[j], ssd_a_log_f[j], ssd_dt_bias_b[j], ssd_a_log_b[j],
                          ssd_d[j], ssd_g_norm[j], ssd_w_out[j], bsz, seq)
        else:
            h = dil_layer(h, norm_mix[i], dil_w_in[j], dil_g_q[j], dil_g_k[j], dil_w_out[j], bsz, seq)
        h = ffn_residual(h, norm_ffn[i], ffn_w_in[i].astype(BF16), ffn_w_out[i].astype(BF16))
        h = ple_residual(h, ple_norm[i], ple_w_gate[i].astype(BF16), p[i].reshape(t, PLE_DIM),
                         ple_w_proj[i].astype(BF16))
    return h.reshape(bsz, seq, dm)
```

```python
import functools
import math

import jax
import jax.numpy as jnp
from jax import lax
from jax.experimental import pallas as pl
from jax.experimental.pallas import tpu as pltpu

F32 = jnp.float32
BF16 = jnp.bfloat16

D_MODEL = 1024
DEPTH = 4
PLE_DIM = 256
ROPE_THETA = 10000.0
EPS = 1e-6
D_FF = 2816
LOG2E = math.log2(math.e)
NEG_BIG = -1e30

GLA_HEADS = 4
GLA_DK = 512
GLA_DV = 1024
GLA_HK = 128
GLA_HV = 256
GLA_RANK = 16
GLA_TAU = 16.0
GLA_CHUNK = 64

DIFF_HEADS = 8
DIFF_HD = 64

SSD_INNER = 2048
SSD_P = 64
SSD_HEADS = 32
SSD_GROUPS = 8
SSD_N = 128
SSD_CONV = 5
SSD_CHUNK = 128
SSD_CONV_DIM = 4096

DIL_PAIRS = ((128, 1), (512, 4), (2048, 16))
DIL_HEADS = 16
DIL_HD = 64
DIL_WIDTH = 1024
DIL_SIDE = 64
DIL_SUPER = 2048

LANES = 128
ROPE_GROUP = 256
SAFE_LOG2 = 60.0


def _cparams(sem, vmem_mb=None):
    kw = dict(dimension_semantics=sem)
    if vmem_mb is not None:
        kw["vmem_limit_bytes"] = vmem_mb << 20
    return pltpu.CompilerParams(**kw)


def _dot(a, b):
    return jnp.dot(a, b, preferred_element_type=F32)


def _dot_nt(a, b):
    return lax.dot_general(a, b, (((1,), (1,)), ((), ())), preferred_element_type=F32)


def _dot_tn(a, b):
    return lax.dot_general(a, b, (((0,), (0,)), ((), ())), preferred_element_type=F32)


def _split(x):
    hi = x.astype(BF16)
    return hi, (x - hi.astype(F32)).astype(BF16)


def _dot_split(a_bf16, x):
    hi, lo = _split(x)
    return _dot(a_bf16, hi) + _dot(a_bf16, lo)


def _split_dot(x, b_bf16):
    hi, lo = _split(x)
    return _dot(hi, b_bf16) + _dot(lo, b_bf16)


def _rms_rows(x, g):
    ms = jnp.mean(x * x, axis=-1, keepdims=True)
    return x * lax.rsqrt(ms + EPS) * g


def _silu(x):
    return x * jax.nn.sigmoid(x)


def _softplus(x):
    return jnp.maximum(x, 0.0) + jnp.log(1.0 + jnp.exp(-jnp.abs(x)))


def _tri(c, upper=False):
    ii = lax.broadcasted_iota(jnp.int32, (c, c), 0)
    jj = lax.broadcasted_iota(jnp.int32, (c, c), 1)
    return (jj >= ii) if upper else (jj <= ii)


def _nm_kernel(x_ref, g_ref, w_ref, o_ref, xn_ref):
    @pl.when(pl.program_id(1) == 0)
    def _():
        xn_ref[...] = _rms_rows(x_ref[...], g_ref[...]).astype(BF16)

    o_ref[...] = _dot(xn_ref[...], w_ref[...]).astype(o_ref.dtype)


def norm_matmul(x, g, w, out_dtype, tm=512, tn=None):
    t, k = x.shape
    n = w.shape[1]
    tn = n if tn is None else tn
    return pl.pallas_call(
        _nm_kernel,
        grid=(t // tm, n // tn),
        in_specs=[pl.BlockSpec((tm, k), lambda i, j: (i, 0)),
                  pl.BlockSpec((1, k), lambda i, j: (0, 0)),
                  pl.BlockSpec((k, tn), lambda i, j: (0, j))],
        out_specs=pl.BlockSpec((tm, tn), lambda i, j: (i, j)),
        out_shape=jax.ShapeDtypeStruct((t, n), out_dtype),
        scratch_shapes=[pltpu.VMEM((tm, k), BF16)],
        compiler_params=_cparams(("parallel", "arbitrary")),
        name="norm_matmul",
    )(x, g.reshape(1, k), w)


def _rope_tables(seq):
    half = DIFF_HD // 2
    inv = ROPE_THETA ** (-jnp.arange(half, dtype=F32) * 2.0 / DIFF_HD)
    ang = jnp.arange(seq, dtype=jnp.int32).astype(F32)[:, None] * inv[None, :]
    return jnp.tile(jnp.cos(ang), (1, 4)), jnp.tile(jnp.sin(ang), (1, 4))


def _rope_col_perm(n_cols):
    half = DIFF_HD // 2
    g = jnp.arange(n_cols // ROPE_GROUP)[:, None, None, None] * ROPE_GROUP
    part = jnp.arange(2)[None, :, None, None] * half
    head = jnp.arange(4)[None, None, :, None] * DIFF_HD
    d = jnp.arange(half)[None, None, None, :]
    return (g + head + part + d).reshape(-1)


def _seg32_ones():
    r = jnp.arange(LANES)
    return (r[:, None] // 32 == r[None, :] // 32).astype(BF16)


def _head_lane_mask(shape, u):
    lane = lax.broadcasted_iota(jnp.int32, shape, len(shape) - 1)
    return (lane % LANES) // (DIFF_HD // 2) == u


def _nm_rope_kernel(x_ref, g_ref, w_ref, hg_ref, cos_ref, sin_ref, bd_ref, o_ref, xn_ref, *xs_refs,
                    n_rope, tn, dil):
    j = pl.program_id(1)
    tm = x_ref.shape[0]
    n = tm // dil

    @pl.when(j == 0)
    def _():
        if dil == 1:
            xn_ref[...] = _rms_rows(x_ref[...], g_ref[...]).astype(BF16)
        else:
            xs_ref = xs_refs[0]
            nch = x_ref.shape[1] // LANES
            for c in range(nch):
                xs_ref[c] = x_ref[:, c * LANES:(c + 1) * LANES]
            for r in range(dil):
                xr = jnp.concatenate([xs_ref[c, pl.ds(r, n, stride=dil), :] for c in range(nch)], axis=1)
                xn_ref[r * n:(r + 1) * n, :] = _rms_rows(xr, g_ref[...]).astype(BF16)

    def put(sl, val):
        if len(o_ref.shape) == 2:
            o_ref[:, sl] = val
        else:
            for r in range(dil):
                o_ref[r, :, sl] = val[r * n:(r + 1) * n]

    @pl.when(j < n_rope)
    def _():
        cos = cos_ref[...]
        sin = sin_ref[...]
        bd = bd_ref[...]
        xn = xn_ref[...]
        wide = 2 * ROPE_GROUP
        for cw in range(tn // wide):
            acc2 = _dot(xn, w_ref[:, cw * wide:(cw + 1) * wide])
            for c2 in range(2):
                c = 2 * cw + c2
                sl = slice(c * ROPE_GROUP, (c + 1) * ROPE_GROUP)
                a = acc2[:, c2 * ROPE_GROUP:c2 * ROPE_GROUP + LANES]
                b = acc2[:, c2 * ROPE_GROUP + LANES:(c2 + 1) * ROPE_GROUP]
                ss = _dot((a * a + b * b).astype(BF16), bd)
                inv = lax.rsqrt(ss * (1.0 / DIFF_HD) + EPS)
                a = a * inv * hg_ref[:, c * ROPE_GROUP:c * ROPE_GROUP + LANES]
                b = b * inv * hg_ref[:, c * ROPE_GROUP + LANES:(c + 1) * ROPE_GROUP]
                put(sl, jnp.concatenate([a * cos - b * sin, b * cos + a * sin], axis=1).astype(o_ref.dtype))

    @pl.when(j >= n_rope)
    def _():
        put(slice(0, tn), _dot(xn_ref[...], w_ref[...]).astype(o_ref.dtype))


def norm_matmul_rope(x, g, w, head_gain, n_rope_cols, seq, dil=None, tm=512, tn=1024):
    t, k = x.shape
    n = w.shape[1]
    perm = jnp.concatenate([_rope_col_perm(n_rope_cols), jnp.arange(n_rope_cols, n)])
    w = w[:, perm]
    head_gain = head_gain[perm]
    cos, sin = _rope_tables(seq)
    nsb = seq // tm
    d = 1 if dil is None else dil
    if d > 1:
        cos, sin = (tb.reshape(nsb, tm // d, d, LANES).transpose(0, 2, 1, 3).reshape(seq, LANES)
                    for tb in (cos, sin))
    kern = functools.partial(_nm_rope_kernel, n_rope=n_rope_cols // tn, tn=tn, dil=d)
    if dil is None:
        out_spec = pl.BlockSpec((tm, tn), lambda i, j: (i, j))
        out_shape = jax.ShapeDtypeStruct((t, n), BF16)
    else:
        per = DIL_SUPER // tm
        out_spec = pl.BlockSpec((None, d, tm // d, tn), lambda i, j: (i // per, 0, i % per, j))
        out_shape = jax.ShapeDtypeStruct((t // DIL_SUPER, d, DIL_SUPER // d, n), BF16)
    scratch = [pltpu.VMEM((tm, k), BF16)]
    if d > 1:
        scratch.append(pltpu.VMEM((k // LANES, tm, LANES), F32))
    return pl.pallas_call(
        kern,
        grid=(t // tm, n // tn),
        in_specs=[pl.BlockSpec((tm, k), lambda i, j: (i, 0)),
                  pl.BlockSpec((1, k), lambda i, j: (0, 0)),
                  pl.BlockSpec((k, tn), lambda i, j: (0, j)),
                  pl.BlockSpec((1, tn), lambda i, j: (0, j)),
                  pl.BlockSpec((tm, LANES), lambda i, j: (i % nsb, 0)),
                  pl.BlockSpec((tm, LANES), lambda i, j: (i % nsb, 0)),
                  pl.BlockSpec((LANES, LANES), lambda i, j: (0, 0))],
        out_specs=out_spec,
        out_shape=out_shape,
        scratch_shapes=scratch,
        compiler_params=_cparams(("parallel", "arbitrary")),
        name="norm_matmul_rope",
    )(x, g.reshape(1, k), w, head_gain.reshape(1, n), cos, sin, _seg32_ones())


def _score_bound(g_q, g_k, qscale):
    return 1.05 * DIFF_HD * qscale * jnp.max(jnp.abs(g_q)) * jnp.max(jnp.abs(g_k))


def _mmres_kernel(a_ref, w_ref, h_ref, o_ref):
    o_ref[...] = h_ref[...] + _dot(a_ref[...], w_ref[...])


def matmul_residual(a, w, h, tm=512):
    t, k = a.shape
    n = w.shape[1]
    return pl.pallas_call(
        _mmres_kernel,
        grid=(t // tm,),
        in_specs=[pl.BlockSpec((tm, k), lambda i: (i, 0)),
                  pl.BlockSpec((k, n), lambda i: (0, 0)),
                  pl.BlockSpec((tm, n), lambda i: (i, 0))],
        out_specs=pl.BlockSpec((tm, n), lambda i: (i, 0)),
        out_shape=jax.ShapeDtypeStruct((t, n), F32),
        compiler_params=_cparams(("parallel",)),
        name="matmul_residual",
    )(a, w, h)


def _ffn_kernel(x_ref, g_ref, wg_ref, wu_ref, wo_ref, o_ref, xn_ref):
    @pl.when(pl.program_id(1) == 0)
    def _():
        x = x_ref[...]
        xn_ref[...] = _rms_rows(x, g_ref[...]).astype(BF16)
        o_ref[...] = x

    xn = xn_ref[...]
    gate = _dot(xn, wg_ref[...])
    up = _dot(xn, wu_ref[...])
    a = (_silu(gate) * up).astype(BF16)
    o_ref[...] += _dot(a, wo_ref[...])


def ffn_residual(h, g, w_in, w_out, tm=1024, tf=256):
    t, k = h.shape
    nf = D_FF // tf
    return pl.pallas_call(
        _ffn_kernel,
        grid=(t // tm, nf),
        in_specs=[pl.BlockSpec((tm, k), lambda i, f: (i, 0)),
                  pl.BlockSpec((1, k), lambda i, f: (0, 0)),
                  pl.BlockSpec((k, tf), lambda i, f: (0, f)),
                  pl.BlockSpec((k, tf), lambda i, f: (0, nf + f)),
                  pl.BlockSpec((tf, k), lambda i, f: (f, 0))],
        out_specs=pl.BlockSpec((tm, k), lambda i, f: (i, 0)),
        out_shape=jax.ShapeDtypeStruct((t, k), F32),
        scratch_shapes=[pltpu.VMEM((tm, k), BF16)],
        compiler_params=_cparams(("parallel", "arbitrary")),
        name="ffn",
    )(h, g.reshape(1, k), w_in, w_in, w_out)


def _ple_kernel(x_ref, g_ref, wg_ref, p_ref, wp_ref, o_ref):
    x = x_ref[...]
    xn = _rms_rows(x, g_ref[...]).astype(BF16)
    gate = jax.nn.sigmoid(_dot(xn, wg_ref[...]))
    proj = _dot(p_ref[...].astype(BF16), wp_ref[...])
    o_ref[...] = x + gate * proj


def ple_residual(h, g, w_gate, p, w_proj, tm=512):
    t, k = h.shape
    return pl.pallas_call(
        _ple_kernel,
        grid=(t // tm,),
        in_specs=[pl.BlockSpec((tm, k), lambda i: (i, 0)),
                  pl.BlockSpec((1, k), lambda i: (0, 0)),
                  pl.BlockSpec((k, k), lambda i: (0, 0)),
                  pl.BlockSpec((tm, PLE_DIM), lambda i: (i, 0)),
                  pl.BlockSpec((PLE_DIM, k), lambda i: (0, 0))],
        out_specs=pl.BlockSpec((tm, k), lambda i: (i, 0)),
        out_shape=jax.ShapeDtypeStruct((t, k), F32),
        compiler_params=_cparams(("parallel",)),
        name="ple",
    )(h, g.reshape(1, k), w_gate, p, w_proj)


def _gla_kernel(qf_ref, kf_ref, vf_ref, zf_ref, wgf_ref, bgf_ref,
                qb_ref, kb_ref, vb_ref, zb_ref, wgb_ref, bgb_ref,
                of_ref, ob_ref, st_ref, la_ref, *, nchunk):
    @pl.when(pl.program_id(2) == 0)
    def _():
        st_ref[...] = jnp.zeros_like(st_ref)

    c = GLA_CHUNK
    for dirn, (z_ref, wg_ref, bg_ref) in enumerate(((zf_ref, wgf_ref, bgf_ref), (zb_ref, wgb_ref, bgb_ref))):
        x = _dot(z_ref[...].astype(BF16), wg_ref[...]) + bg_ref[...]
        la_ref[dirn] = (jnp.minimum(x, 0.0) - jnp.log(1.0 + jnp.exp(-jnp.abs(x)))) * (1.0 / GLA_TAU)
    cms = (_tri(c).astype(BF16), _tri(c, upper=True).astype(BF16))
    mks = (_tri(c), jnp.logical_not(_tri(c)))

    def chunk(dirn, ci, q_ref, k_ref, v_ref, o_ref):
        rs = slice(ci * c, (ci + 1) * c)
        la = la_ref[dirn, rs, :]
        b = _dot_split(cms[dirn], la)
        tot = jnp.sum(la, axis=0, keepdims=True)
        qc = q_ref[rs, :].astype(F32)
        kc = k_ref[rs, :].astype(F32)
        vc = v_ref[rs, :]
        q_in = (qc * jnp.exp(b) * (GLA_HK ** -0.5)).astype(BF16)
        k_out = (kc * jnp.exp(-b)).astype(BF16)
        k_dec = (kc * jnp.exp(tot - b)).astype(BF16)
        sc = jnp.where(mks[dirn], _dot_nt(q_in, k_out), 0.0).astype(BF16)
        st = st_ref[dirn]
        o = _dot(sc, vc) + _dot_nt(q_in, st.astype(BF16))
        o_ref[rs, :] = o.astype(o_ref.dtype)
        st_ref[dirn] = st * jnp.exp(tot) + _dot_tn(vc, k_dec)

    for cc in range(nchunk):
        chunk(0, cc, qf_ref, kf_ref, vf_ref, of_ref)
        chunk(1, nchunk - 1 - cc, qb_ref, kb_ref, vb_ref, ob_ref)


def gla_core(proj, z, wg, bg, bsz, seq, rows=512):
    t = bsz * seq
    nblk = seq // rows

    def side(bwd):
        def rb(b, i):
            return b * nblk + (nblk - 1 - i if bwd else i)
        d = int(bwd)
        return [pl.BlockSpec((rows, GLA_HK), lambda b, h, i: (rb(b, i), h)),
                pl.BlockSpec((rows, GLA_HK), lambda b, h, i: (rb(b, i), 4 + h)),
                pl.BlockSpec((rows, GLA_HV), lambda b, h, i: (rb(b, i), 4 + h)),
                pl.BlockSpec((rows, LANES), lambda b, h, i: (rb(b, i), 0)),
                pl.BlockSpec((None, LANES, GLA_HK), lambda b, h, i: (d, 0, h)),
                pl.BlockSpec((None, 1, GLA_HK), lambda b, h, i: (d, 0, h))]

    out_f = pl.BlockSpec((rows, GLA_HV), lambda b, h, i: (b * nblk + i, h))
    out_b = pl.BlockSpec((rows, GLA_HV), lambda b, h, i: (b * nblk + nblk - 1 - i, h))
    oshape = jax.ShapeDtypeStruct((t, GLA_DV), BF16)
    kern = functools.partial(_gla_kernel, nchunk=rows // GLA_CHUNK)
    return pl.pallas_call(
        kern,
        grid=(bsz, GLA_HEADS, nblk),
        in_specs=side(False) + side(True),
        out_specs=[out_f, out_b],
        out_shape=[oshape, oshape],
        scratch_shapes=[pltpu.VMEM((2, GLA_HV, GLA_HK), F32), pltpu.VMEM((2, rows, GLA_HK), F32)],
        compiler_params=_cparams(("parallel", "parallel", "arbitrary")),
        name="gla_core",
    )(proj, proj, proj, z, wg, bg, proj, proj, proj, z, wg, bg)


def _gla_out_kernel(of_ref, ob_ref, r_ref, g_ref, w_ref, h_ref, o_ref):
    o = of_ref[...].astype(F32) + ob_ref[...].astype(F32)
    parts = []
    for hh in range(GLA_HEADS):
        oh = o[:, hh * GLA_HV:(hh + 1) * GLA_HV]
        ms = jnp.mean(oh * oh, axis=-1, keepdims=True)
        parts.append(oh * lax.rsqrt(ms + EPS))
    on = jnp.concatenate(parts, axis=1) * g_ref[...]
    a = (on * _silu(r_ref[...].astype(F32))).astype(BF16)
    o_ref[...] = h_ref[...] + _dot(a, w_ref[...])


def gla_out(o_f, o_b, proj, g_out, w_out, h, tm=512):
    t, n = h.shape
    return pl.pallas_call(
        _gla_out_kernel,
        grid=(t // tm,),
        in_specs=[pl.BlockSpec((tm, GLA_DV), lambda i: (i, 0)),
                  pl.BlockSpec((tm, GLA_DV), lambda i: (i, 0)),
                  pl.BlockSpec((tm, GLA_DV), lambda i: (i, 2)),
                  pl.BlockSpec((1, GLA_DV), lambda i: (0, 0)),
                  pl.BlockSpec((GLA_DV, n), lambda i: (0, 0)),
                  pl.BlockSpec((tm, n), lambda i: (i, 0))],
        out_specs=pl.BlockSpec((tm, n), lambda i: (i, 0)),
        out_shape=jax.ShapeDtypeStruct((t, n), F32),
        compiler_params=_cparams(("parallel",)),
        name="gla_out",
    )(o_f, o_b, proj, jnp.tile(g_out, GLA_HEADS).reshape(1, GLA_DV), w_out, h)


def gla_layer(h, norm_g, w_in, w_gate_f, b_gate_f, w_gate_b, b_gate_b, g_out, w_out, bsz, seq):
    n_main = 2 * GLA_DK + 2 * GLA_DV
    proj = norm_matmul(h, norm_g, w_in[:, :n_main].astype(BF16), BF16, tn=1024)
    wz = jnp.pad(w_in[:, n_main:], ((0, 0), (0, LANES - 2 * GLA_RANK))).astype(BF16)
    z = norm_matmul(h, norm_g, wz, F32)
    wg = jnp.zeros((2, LANES, GLA_DK), F32)
    wg = wg.at[0, :GLA_RANK].set(w_gate_f).at[1, GLA_RANK:2 * GLA_RANK].set(w_gate_b).astype(BF16)
    bg = jnp.stack([b_gate_f, b_gate_b]).reshape(2, 1, GLA_DK)
    o_f, o_b = gla_core(proj, z, wg, bg, bsz, seq)
    return gla_out(o_f, o_b, proj, g_out, w_out.astype(BF16), h)


def _diff_kernel(q_ref, k_ref, v0_ref, v1_ref, lam_ref, gs_ref, o_ref, qz_ref, acc_ref, *m_refs,
                 tk, lambda_init, bounded):
    tq = q_ref.shape[0]
    q = q_ref[...]
    zero = jnp.zeros_like(q)
    for u in range(4):
        qz_ref[u] = jnp.where(_head_lane_mask(q.shape, u), q, zero)
    acc_ref[...] = jnp.zeros_like(acc_ref)
    if not bounded:
        m_ref = m_refs[0]
        m_ref[...] = jnp.full_like(m_ref, NEG_BIG)
    ones = jnp.ones((tk, LANES), BF16)

    def body(ki, carry):
        rows = pl.ds(pl.multiple_of(ki * tk, tk), tk)
        kb = k_ref[rows, :]
        vaug = [jnp.concatenate([v_ref[rows, :], ones], axis=1) for v_ref in (v0_ref, v1_ref)]
        for u in range(4):
            s = _dot_nt(qz_ref[u], kb)
            if bounded:
                acc_ref[u] += _dot(jnp.exp2(s).astype(BF16), vaug[u // 2])
            else:
                m_prev = m_ref[u]
                m_new = jnp.maximum(m_prev, jnp.max(s, axis=1, keepdims=True))
                alpha = jnp.exp2(m_prev - m_new)
                p = jnp.exp2(s - jnp.tile(m_new, (1, tk // LANES)))
                acc_ref[u] = jnp.tile(alpha, (1, 2)) * acc_ref[u] + _dot(p.astype(BF16), vaug[u // 2])
                m_ref[u] = m_new
        return carry

    lax.fori_loop(0, k_ref.shape[0] // tk, body, 0, unroll=2 if bounded else 1)

    lam4 = lam_ref[...]
    lam = (jnp.exp(jnp.sum(lam4[0:1] * lam4[1:2], axis=1, keepdims=True))
           - jnp.exp(jnp.sum(lam4[2:3] * lam4[3:4], axis=1, keepdims=True)) + lambda_init)
    for hh in range(2):
        a1 = acc_ref[2 * hh]
        a2 = acc_ref[2 * hh + 1]
        o = a1[:, :LANES] / a1[:, LANES:] - lam * (a2[:, :LANES] / a2[:, LANES:])
        ms = jnp.mean(o * o, axis=-1, keepdims=True)
        o_ref[:, hh * LANES:(hh + 1) * LANES] = (
            o * lax.rsqrt(ms + EPS) * gs_ref[...] * (1.0 - lambda_init)).astype(o_ref.dtype)


def diff_core(proj, lam4, g_sub, lambda_init, score_bound, bsz, seq, tq=512, tk=512):
    t = bsz * seq
    nq = seq // tq
    npair = DIFF_HEADS // 2
    in_specs = [pl.BlockSpec((tq, ROPE_GROUP), lambda b, h, i: (b * nq + i, h)),
                pl.BlockSpec((seq, ROPE_GROUP), lambda b, h, i: (b, npair + h)),
                pl.BlockSpec((seq, LANES), lambda b, h, i: (b, 16 + 2 * h)),
                pl.BlockSpec((seq, LANES), lambda b, h, i: (b, 17 + 2 * h)),
                pl.BlockSpec((4, DIFF_HD), lambda b, h, i: (0, 0)),
                pl.BlockSpec((1, LANES), lambda b, h, i: (0, 0))]
    out_spec = pl.BlockSpec((tq, ROPE_GROUP), lambda b, h, i: (b * nq + i, h))
    out_shape = jax.ShapeDtypeStruct((t, D_MODEL), BF16)
    args = (proj, proj, proj, proj, lam4, g_sub.reshape(1, LANES))

    def call(bounded):
        scratch = [pltpu.VMEM((4, tq, ROPE_GROUP), BF16), pltpu.VMEM((4, tq, 2 * LANES), F32)]
        if not bounded:
            scratch.append(pltpu.VMEM((4, tq, LANES), F32))
        return lambda *a: pl.pallas_call(
            functools.partial(_diff_kernel, tk=tk, lambda_init=lambda_init, bounded=bounded),
            grid=(bsz, npair, nq), in_specs=in_specs, out_specs=out_spec, out_shape=out_shape,
            scratch_shapes=scratch,
            compiler_params=_cparams(("parallel", "parallel", "arbitrary")),
            name="diff_core_bounded" if bounded else "diff_core_online",
        )(*a)

    return lax.cond(score_bound <= SAFE_LOG2, call(True), call(False), *args)


def diff_layer(h, norm_g, layer_idx, w_in, g_q, g_k, lam_q1, lam_k1, lam_q2, lam_k2, g_sub, w_out,
               bsz, seq):
    lambda_init = 0.8 - 0.6 * math.exp(-0.3 * layer_idx)
    qscale = (DIFF_HD ** -0.5) * LOG2E
    head_gain = jnp.concatenate([jnp.tile(g_q.reshape(-1), DIFF_HEADS) * qscale,
                                 jnp.tile(g_k.reshape(-1), DIFF_HEADS),
                                 jnp.ones((D_MODEL,), F32)])
    proj = norm_matmul_rope(h, norm_g, w_in.astype(BF16), head_gain, 2 * D_MODEL, seq)
    lam4 = jnp.stack([lam_q1, lam_k1, lam_q2, lam_k2])
    o = diff_core(proj, lam4, g_sub, lambda_init, _score_bound(g_q, g_k, qscale), bsz, seq)
    return matmul_residual(o, w_out.astype(BF16), h)


def _conv_kernel(prev_ref, cur_ref, next_ref, w_ref, b_ref, o_ref, buf_ref, *, halo):
    i = pl.program_id(1)
    ts = cur_ref.shape[0]
    prev = prev_ref[...].astype(F32)
    nxt = next_ref[...].astype(F32)
    buf_ref[0:halo, :] = jnp.where(i == 0, 0.0, prev)
    buf_ref[halo:halo + ts, :] = cur_ref[...].astype(F32)
    buf_ref[halo + ts:, :] = jnp.where(i == pl.num_programs(1) - 1, 0.0, nxt)
    w = w_ref[...]
    y = b_ref[...] + w[2:3] * buf_ref[halo:halo + ts, :]
    for k in (0, 1, 3, 4):
        y = y + w[k:k + 1] * buf_ref[pl.ds(halo + k - 2, ts), :]
    o_ref[...] = _silu(y).astype(o_ref.dtype)


def ssd_conv(zx, conv_w, conv_b, bsz, seq, ts=512, tc=512):
    t = bsz * seq
    halo = 16
    nrb = seq // ts
    hb = ts // halo
    nhb = seq // halo
    off = SSD_INNER // tc
    kern = functools.partial(_conv_kernel, halo=halo)
    wpad = jnp.pad(conv_w, ((0, 8 - SSD_CONV), (0, 0)))
    return pl.pallas_call(
        kern,
        grid=(bsz, nrb, SSD_CONV_DIM // tc),
        in_specs=[pl.BlockSpec((halo, tc), lambda b, i, c: (b * nhb + jnp.maximum(i * hb - 1, 0), off + c)),
                  pl.BlockSpec((ts, tc), lambda b, i, c: (b * nrb + i, off + c)),
                  pl.BlockSpec((halo, tc), lambda b, i, c: (b * nhb + jnp.minimum((i + 1) * hb, nhb - 1), off + c)),
                  pl.BlockSpec((8, tc), lambda b, i, c: (0, c)),
                  pl.BlockSpec((1, tc), lambda b, i, c: (0, c))],
        out_specs=pl.BlockSpec((ts, tc), lambda b, i, c: (b * nrb + i, c)),
        out_shape=jax.ShapeDtypeStruct((t, SSD_CONV_DIM), BF16),
        scratch_shapes=[pltpu.VMEM((ts + 2 * halo, tc), F32)],
        compiler_params=_cparams(("parallel", "parallel", "parallel")),
        name="ssd_conv",
    )(zx, zx, zx, wpad, conv_b.reshape(1, -1))


def _ssd_prep_kernel(dtr_ref, bias_ref, aneg_ref, dt_ref, ac_ref, at_ref):
    c = SSD_CHUNK
    lower = _tri(c).astype(BF16)
    upper = _tri(c, upper=True).astype(BF16)
    lane = lax.broadcasted_iota(jnp.int32, (c, LANES), 1)
    for cc in range(dtr_ref.shape[0] // c):
        rs = slice(cc * c, (cc + 1) * c)
        dt = _softplus(dtr_ref[rs, :] + bias_ref[...])
        adt = dt * aneg_ref[...]
        hi, lo = _split(adt)
        ac = jnp.where(lane < SSD_HEADS, _dot(lower, hi) + _dot(lower, lo), _dot(upper, hi) + _dot(upper, lo))
        dt_ref[rs, :] = dt
        ac_ref[rs, :] = ac
        act = ac.T
        for k in range(2 * SSD_GROUPS):
            at_ref[cc, k] = act[4 * k:4 * k + 4, :]


def ssd_prep(dtr, dt_bias, a_neg, rows=512):
    t = dtr.shape[0]
    c = SSD_CHUNK
    bias = jnp.pad(dt_bias, (0, LANES - 2 * SSD_HEADS)).reshape(1, LANES)
    aneg = jnp.pad(a_neg, (0, LANES - 2 * SSD_HEADS)).reshape(1, LANES)
    row = pl.BlockSpec((rows, LANES), lambda i: (i, 0))
    vec = pl.BlockSpec((1, LANES), lambda i: (0, 0))
    return pl.pallas_call(
        _ssd_prep_kernel,
        grid=(t // rows,),
        in_specs=[row, vec, vec],
        out_specs=[row, row, pl.BlockSpec((rows // c, 2 * SSD_GROUPS, 4, c), lambda i: (i, 0, 0, 0))],
        out_shape=[jax.ShapeDtypeStruct((t, LANES), F32), jax.ShapeDtypeStruct((t, LANES), F32),
                   jax.ShapeDtypeStruct((t // c, 2 * SSD_GROUPS, 4, c), F32)],
        compiler_params=_cparams(("parallel",)),
        name="ssd_prep",
    )(dtr, bias, aneg)


def _ssd_kernel(xf_ref, bf_ref, cf_ref, dtf_ref, acf_ref, atf_ref, e5f_ref,
                xb_ref, bb_ref, cb_ref, dtb_ref, acb_ref, atb_ref, e5b_ref,
                yf_ref, yb_ref, st_ref, *, nchunk):
    @pl.when(pl.program_id(2) == 0)
    def _():
        st_ref[...] = jnp.zeros_like(st_ref)

    c = SSD_CHUNK
    masks = (_tri(c), jnp.logical_not(_tri(c)))
    lane2 = lax.broadcasted_iota(jnp.int32, (c, 4 * SSD_P), 1)
    head_masks = [(lane2 >= hh * SSD_P) & (lane2 < (hh + 1) * SSD_P) for hh in range(4)]

    def narrow(v5):
        lo = lax.broadcasted_iota(jnp.int32, (v5.shape[0], LANES), 1) < SSD_P
        return jnp.concatenate([jnp.where(lo, v5[:, 0:LANES], v5[:, LANES:2 * LANES]),
                                jnp.where(lo, v5[:, 2 * LANES:3 * LANES], v5[:, 3 * LANES:])], axis=1)

    def chunk(dirn, ci, x_ref, b_ref, c_ref, dt_ref, ac_ref, at_ref, e5_ref, y_ref):
        rs = slice(ci * c, (ci + 1) * c)
        e5 = e5_ref[...]
        dt5 = _dot(dt_ref[rs, :].astype(BF16), e5)
        ac5 = _split_dot(ac_ref[rs, :], e5)
        acr = at_ref[ci]
        tot5 = ac5[c - 1:c, :] if dirn == 0 else ac5[0:1, :]
        dt2 = narrow(dt5)
        ac2 = narrow(ac5)
        tot2 = narrow(tot5)
        x = x_ref[rs, :].astype(F32)
        bm = b_ref[rs, :]
        cmat = c_ref[rs, :]
        xdt = x * dt2
        cbm = jnp.where(masks[dirn], _dot_nt(cmat, bm), 0.0)
        st = st_ref[dirn]
        ms, xs = [], []
        for hh in range(4):
            col = ac5[:, hh * LANES:(hh + 1) * LANES]
            row = acr[hh:hh + 1, :]
            ms.append((cbm * jnp.exp(jnp.minimum(col - row, 0.0))).astype(BF16))
            xs.append(jnp.where(head_masks[hh], xdt, 0.0).astype(BF16))
        y = (_dot(cmat, st.astype(BF16)) * jnp.exp(ac2)
             + _dot(jnp.concatenate(ms, axis=1), jnp.concatenate(xs, axis=0)))
        y_ref[rs, :] = y.astype(y_ref.dtype)
        xw = (xdt * jnp.exp(tot2 - ac2)).astype(BF16)
        st_ref[dirn] = st * jnp.exp(tot2) + _dot_tn(bm, xw)

    for cc in range(nchunk):
        chunk(0, cc, xf_ref, bf_ref, cf_ref, dtf_ref, acf_ref, atf_ref, e5f_ref, yf_ref)
        chunk(1, nchunk - 1 - cc, xb_ref, bb_ref, cb_ref, dtb_ref, acb_ref, atb_ref, e5b_ref, yb_ref)


def ssd_core(xbc, dt, ac, act, bsz, seq, rows=512):
    t = bsz * seq
    nblk = seq // rows
    c = SSD_CHUNK
    hp = 4 * SSD_P
    src = 4 * jnp.arange(2 * SSD_GROUPS)[:, None] + jnp.arange(4)[None, :]
    onehot = (jnp.arange(LANES)[None, None, :] == src[..., None]).astype(BF16)
    e5 = jnp.repeat(onehot, LANES, axis=1).transpose(0, 2, 1)

    def side(bwd):
        def rb(b, i):
            return b * nblk + (nblk - 1 - i if bwd else i)
        d = int(bwd)
        return [pl.BlockSpec((rows, hp), lambda b, g, i: (rb(b, i), g)),
                pl.BlockSpec((rows, SSD_N), lambda b, g, i: (rb(b, i), 16 + g)),
                pl.BlockSpec((rows, SSD_N), lambda b, g, i: (rb(b, i), 24 + g)),
                pl.BlockSpec((rows, LANES), lambda b, g, i: (rb(b, i), 0)),
                pl.BlockSpec((rows, LANES), lambda b, g, i: (rb(b, i), 0)),
                pl.BlockSpec((rows // c, None, 4, c), lambda b, g, i: (rb(b, i), d * SSD_GROUPS + g, 0, 0)),
                pl.BlockSpec((None, LANES, 4 * LANES), lambda b, g, i: (d * SSD_GROUPS + g, 0, 0))]

    out_f = pl.BlockSpec((rows, hp), lambda b, g, i: (b * nblk + i, g))
    out_b = pl.BlockSpec((rows, hp), lambda b, g, i: (b * nblk + nblk - 1 - i, g))
    oshape = jax.ShapeDtypeStruct((t, SSD_INNER), BF16)
    kern = functools.partial(_ssd_kernel, nchunk=rows // c)
    args = (xbc, xbc, xbc, dt, ac, act, e5)
    return pl.pallas_call(
        kern,
        grid=(bsz, SSD_GROUPS, nblk),
        in_specs=side(False) + side(True),
        out_specs=[out_f, out_b],
        out_shape=[oshape, oshape],
        scratch_shapes=[pltpu.VMEM((2, SSD_N, hp), F32)],
        compiler_params=_cparams(("parallel", "parallel", "arbitrary")),
        name="ssd_core",
    )(*args, *args)


def _ssd_out_kernel(yf_ref, yb_ref, x_ref, z_ref, dsk_ref, g_ref, w_ref, h_ref, o_ref):
    y = yf_ref[...].astype(F32) + yb_ref[...].astype(F32) + x_ref[...].astype(F32) * dsk_ref[...]
    y = y * _silu(z_ref[...].astype(F32))
    gw = SSD_INNER // SSD_GROUPS
    parts = []
    for gg in range(SSD_GROUPS):
        yg = y[:, gg * gw:(gg + 1) * gw]
        ms = jnp.mean(yg * yg, axis=-1, keepdims=True)
        parts.append(yg * lax.rsqrt(ms + EPS))
    a = (jnp.concatenate(parts, axis=1) * g_ref[...]).astype(BF16)
    o_ref[...] = h_ref[...] + _dot(a, w_ref[...])


def ssd_out(y_f, y_b, xbc, zx, d_skip, g_norm, w_out, h, tm=512):
    t, n = h.shape
    wide = pl.BlockSpec((tm, SSD_INNER), lambda i: (i, 0))
    vec = pl.BlockSpec((1, SSD_INNER), lambda i: (0, 0))
    return pl.pallas_call(
        _ssd_out_kernel,
        grid=(t // tm,),
        in_specs=[wide, wide, wide, wide, vec, vec,
                  pl.BlockSpec((SSD_INNER, n), lambda i: (0, 0)),
                  pl.BlockSpec((tm, n), lambda i: (i, 0))],
        out_specs=pl.BlockSpec((tm, n), lambda i: (i, 0)),
        out_shape=jax.ShapeDtypeStruct((t, n), F32),
        compiler_params=_cparams(("parallel",)),
        name="ssd_out",
    )(y_f, y_b, xbc, zx, jnp.repeat(d_skip, SSD_P).reshape(1, SSD_INNER),
      g_norm.reshape(1, SSD_INNER), w_out, h)


def ssd_layer(h, norm_g, w_in, conv_w, conv_b, dt_bias_f, a_log_f, dt_bias_b, a_log_b, d_skip,
              g_norm, w_out, bsz, seq):
    n_main = SSD_INNER + SSD_CONV_DIM
    zx = norm_matmul(h, norm_g, w_in[:, :n_main].astype(BF16), BF16, tn=1024)
    wdt = jnp.pad(w_in[:, n_main:], ((0, 0), (0, LANES - 2 * SSD_HEADS))).astype(BF16)
    dtr = norm_matmul(h, norm_g, wdt, F32)
    xbc = ssd_conv(zx, conv_w, conv_b, bsz, seq)
    dt_bias = jnp.concatenate([dt_bias_f, dt_bias_b])
    a_neg = -jnp.exp(jnp.concatenate([a_log_f, a_log_b]))
    dt, ac, act = ssd_prep(dtr, dt_bias, a_neg)
    y_f, y_b = ssd_core(xbc, dt, ac, act, bsz, seq)
    return ssd_out(y_f, y_b, xbc, zx, d_skip, g_norm, w_out.astype(BF16), h)


def _dil_kernel(q_ref, kp_ref, kc_ref, kn_ref, vp_ref, vc_ref, vn_ref, o_ref, lse_ref, *, bounded):
    ai = pl.program_id(2)
    na = pl.num_programs(2)
    tq = q_ref.shape[0]
    side = DIL_SIDE
    kcat = jnp.concatenate([kp_ref[...], kc_ref[...], kn_ref[...]], axis=0)
    vcat = jnp.concatenate([vp_ref[...], vc_ref[...], vn_ref[...]], axis=0)
    nk = tq + 2 * side
    ii = lax.broadcasted_iota(jnp.int32, (4 * tq, nk), 0) & (tq - 1)
    jj = lax.broadcasted_iota(jnp.int32, (4 * tq, nk), 1)
    kpos = ai * tq - side + jj
    valid = (jj >= ii) & (jj <= ii + 2 * side) & (kpos >= 0) & (kpos < na * tq)
    lane = lax.broadcasted_iota(jnp.int32, (tq, ROPE_GROUP), 1)
    for g4 in range(DIL_HEADS // 4):
        gsl = slice(g4 * ROPE_GROUP, (g4 + 1) * ROPE_GROUP)
        q4 = q_ref[:, gsl]
        zq = jnp.zeros_like(q4)
        qst = jnp.concatenate([jnp.where(_head_lane_mask(q4.shape, u), q4, zq) for u in range(4)], axis=0)
        s = jnp.where(valid, _dot_nt(qst, kcat[:, gsl]), NEG_BIG)
        if bounded:
            p = jnp.exp2(s)
        else:
            m = jnp.max(s, axis=1, keepdims=True)
            p = jnp.exp2(s - m)
        den = jnp.sum(p, axis=1, keepdims=True)
        r = _dot(p.astype(BF16), vcat[:, gsl])
        lse = jnp.log(den) if bounded else m * (1.0 / LOG2E) + jnp.log(den)
        out = jnp.zeros((tq, ROPE_GROUP), F32)
        lse4 = jnp.zeros((tq, ROPE_GROUP), F32)
        for u in range(4):
            rows = slice(u * tq, (u + 1) * tq)
            mine = (lane >= u * DIL_HD) & (lane < (u + 1) * DIL_HD)
            out = jnp.where(mine, r[rows] / den[rows], out)
            lse4 = jnp.where(mine, lse[rows], lse4)
        o_ref[:, gsl] = out.astype(o_ref.dtype)
        lse_ref[:, gsl] = lse4


def dil_core(pg, dilation, score_bound, bsz, seq, tq=128):
    a = seq // dilation
    na = a // tq
    side = DIL_SIDE
    nst = seq // DIL_SUPER
    per = DIL_SUPER // dilation // tq
    per_h = DIL_SUPER // dilation // side
    nhb = a // side

    def cur(col):
        return pl.BlockSpec((None, None, tq, DIL_WIDTH),
                            lambda b, r, i: (b * nst + i // per, r, i % per, col))

    def halo(col, which):
        def imap(b, r, i):
            hb = (jnp.maximum(2 * i - 1, 0) if which == 0 else jnp.minimum(2 * i + 2, nhb - 1))
            return (b * nst + hb // per_h, r, hb % per_h, col)
        return pl.BlockSpec((None, None, side, DIL_WIDTH), imap)

    out_spec = pl.BlockSpec((None, None, tq, DIL_WIDTH), lambda b, r, i: (b * nst + i // per, r, i % per, 0))
    oshape = (bsz * nst, dilation, DIL_SUPER // dilation, DIL_WIDTH)

    def call(bounded):
        return lambda x: pl.pallas_call(
            functools.partial(_dil_kernel, bounded=bounded),
            grid=(bsz, dilation, na),
            in_specs=[cur(0), halo(1, 0), cur(1), halo(1, 1), halo(2, 0), cur(2), halo(2, 1)],
            out_specs=[out_spec, out_spec],
            out_shape=[jax.ShapeDtypeStruct(oshape, BF16), jax.ShapeDtypeStruct(oshape, F32)],
            compiler_params=_cparams(("parallel", "parallel", "parallel")),
            name=f"dil_core_{dilation}" + ("_bounded" if bounded else "_online"),
        )(x, x, x, x, x, x, x)

    return lax.cond(score_bound <= SAFE_LOG2, call(True), call(False), pg)


def _dil_out_kernel(o0_ref, o1_ref, o2_ref, l0_ref, l1_ref, l2_ref, w_ref, h_ref, o_ref,
                    so_ref, sl_ref):
    tm = h_ref.shape[0]
    nch = DIL_WIDTH // LANES
    for gi, (og_ref, lg_ref) in enumerate(((o1_ref, l1_ref), (o2_ref, l2_ref))):
        d = DIL_PAIRS[gi + 1][1]
        n = tm // d
        for r in range(d):
            ov = og_ref[r].astype(F32)
            lv = lg_ref[r]
            for c in range(nch):
                so_ref[gi, c, pl.ds(r, n, stride=d), :] = ov[:, c * LANES:(c + 1) * LANES]
                sl_ref[gi, c, pl.ds(r, n, stride=d), :] = lv[:, c * LANES:(c + 1) * LANES]
    parts = []
    for c in range(nch):
        sl = slice(c * LANES, (c + 1) * LANES)
        l0 = l0_ref[0, :, sl]
        l1 = sl_ref[0, c]
        l2 = sl_ref[1, c]
        mx = jnp.maximum(jnp.maximum(l0, l1), l2)
        e0 = jnp.exp(l0 - mx)
        e1 = jnp.exp(l1 - mx)
        e2 = jnp.exp(l2 - mx)
        num = e0 * o0_ref[0, :, sl].astype(F32) + e1 * so_ref[0, c] + e2 * so_ref[1, c]
        parts.append((num / (e0 + e1 + e2)).astype(BF16))
    o_ref[...] = h_ref[...] + _dot(jnp.concatenate(parts, axis=1), w_ref[...])


def dil_out(os, lses, w_out, h, tm=512):
    t, n = h.shape
    per = DIL_SUPER // tm

    def grp(d):
        return pl.BlockSpec((None, d, tm // d, DIL_WIDTH), lambda i: (i // per, 0, i % per, 0))

    specs = [grp(d) for _, d in DIL_PAIRS]
    row = pl.BlockSpec((tm, n), lambda i: (i, 0))
    return pl.pallas_call(
        _dil_out_kernel,
        grid=(t // tm,),
        in_specs=specs + specs + [pl.BlockSpec((DIL_WIDTH, n), lambda i: (0, 0)), row],
        out_specs=row,
        out_shape=jax.ShapeDtypeStruct((t, n), F32),
        scratch_shapes=[pltpu.VMEM((2, DIL_WIDTH // LANES, tm, LANES), F32),
                        pltpu.VMEM((2, DIL_WIDTH // LANES, tm, LANES), F32)],
        compiler_params=_cparams(("parallel",)),
        name="dil_out",
    )(*os, *lses, w_out, h)


def dil_layer(h, norm_g, w_in, g_q, g_k, w_out, bsz, seq):
    qscale = (DIL_HD ** -0.5) * LOG2E
    w4 = w_in.reshape(D_MODEL, 3, len(DIL_PAIRS), DIL_WIDTH)
    os, lses = [], []
    for gi, (_, dilation) in enumerate(DIL_PAIRS):
        wg = w4[:, :, gi, :].reshape(D_MODEL, 3 * DIL_WIDTH).astype(BF16)
        head_gain = jnp.concatenate([jnp.tile(g_q[gi], DIL_HEADS) * qscale,
                                     jnp.tile(g_k[gi], DIL_HEADS),
                                     jnp.ones((DIL_WIDTH,), F32)])
        pg = norm_matmul_rope(h, norm_g, wg, head_gain, 2 * DIL_WIDTH, seq, dil=dilation)
        o, lse = dil_core(pg, dilation, _score_bound(g_q[gi], g_k[gi], qscale), bsz, seq)
        os.append(o)
        lses.append(lse)
    return dil_out(os, lses, w_out.astype(BF16), h)


def kernel(x, p, norm_mix, norm_ffn, ffn_w_in, ffn_w_out, ple_norm, ple_w_gate, ple_w_proj, gla_w_in, gla_w_gate_f, gla_b_gate_f, gla_w_gate_b, gla_b_gate_b, gla_g_out, gla_w_out, diff_w_in, diff_g_q, diff_g_k, diff_lam_q1, diff_lam_k1, diff_lam_q2, diff_lam_k2, diff_g_sub, diff_w_out, ssd_w_in, ssd_conv_w, ssd_conv_b, ssd_dt_bias_f, ssd_a_log_f, ssd_dt_bias_b, ssd_a_log_b, ssd_d, ssd_g_norm, ssd_w_out, dil_w_in, dil_g_q, dil_g_k, dil_w_out):
    bsz, seq, dm = x.shape
    t = bsz * seq
    h = x.reshape(t, dm)
    for i in range(DEPTH):
        kind = i % 4
        j = i // 4
        if kind == 0:
            h = gla_layer(h, norm_mix[i], gla_w_in[j], gla_w_gate_f[j], gla_b_gate_f[j],
                          gla_w_gate_b[j], gla_b_gate_b[j], gla_g_out[j], gla_w_out[j], bsz, seq)
        elif kind == 1:
            h = diff_layer(h, norm_mix[i], i, diff_w_in[j], diff_g_q[j], diff_g_k[j], diff_lam_q1[j],
                           diff_lam_k1[j], diff_lam_q2[j], diff_lam_k2[j], diff_g_sub[j],
                           diff_w_out[j], bsz, seq)
        elif kind == 2:
            h = ssd_layer(h, norm_mix[i], ssd_w_in[j], ssd_conv_w[j], ssd_conv_b[j],
                          ssd_dt_bias_f[j], ssd_a_log_f[j], ssd_dt_bias_b[j], ssd_a_log_b[j],
                          ssd_d[j], ssd_g_norm[j], ssd_w_out[j], bsz, seq)
        else:
            h = dil_layer(h, norm_mix[i], dil_w_in[j], dil_g_q[j], dil_g_k[j], dil_w_out[j], bsz, seq)
        h = ffn_residual(h, norm_ffn[i], ffn_w_in[i].astype(BF16), ffn_w_out[i].astype(BF16))
        h = ple_residual(h, ple_norm[i], ple_w_gate[i].astype(BF16), p[i].reshape(t, PLE_DIM),
                         ple_w_proj[i].astype(BF16))
    return h.reshape(bsz, seq, dm)
```

```python
import functools
import math

import jax
import jax.numpy as jnp
from jax import lax
from jax.experimental import pallas as pl
from jax.experimental.pallas import tpu as pltpu

F32 = jnp.float32
BF16 = jnp.bfloat16

D_MODEL = 1024
DEPTH = 4
PLE_DIM = 256
ROPE_THETA = 10000.0
EPS = 1e-6
D_FF = 2816
LOG2E = math.log2(math.e)
NEG_BIG = -1e30

GLA_HEADS = 4
GLA_DK = 512
GLA_DV = 1024
GLA_HK = 128
GLA_HV = 256
GLA_RANK = 16
GLA_TAU = 16.0
GLA_CHUNK = 64

DIFF_HEADS = 8
DIFF_HD = 64

SSD_INNER = 2048
SSD_P = 64
SSD_HEADS = 32
SSD_GROUPS = 8
SSD_N = 128
SSD_CONV = 5
SSD_CHUNK = 128
SSD_CONV_DIM = 4096

DIL_PAIRS = ((128, 1), (512, 4), (2048, 16))
DIL_HEADS = 16
DIL_HD = 64
DIL_WIDTH = 1024
DIL_SIDE = 64
DIL_SUPER = 2048

LANES = 128
ROPE_GROUP = 256
SAFE_LOG2 = 60.0


def _cparams(sem, vmem_mb=None):
    kw = dict(dimension_semantics=sem)
    if vmem_mb is not None:
        kw["vmem_limit_bytes"] = vmem_mb << 20
    return pltpu.CompilerParams(**kw)


def _dot(a, b):
    return jnp.dot(a, b, preferred_element_type=F32)


def _dot_nt(a, b):
    return lax.dot_general(a, b, (((1,), (1,)), ((), ())), preferred_element_type=F32)


def _dot_tn(a, b):
    return lax.dot_general(a, b, (((0,), (0,)), ((), ())), preferred_element_type=F32)


def _split(x):
    hi = x.astype(BF16)
    return hi, (x - hi.astype(F32)).astype(BF16)


def _dot_split(a_bf16, x):
    hi, lo = _split(x)
    return _dot(a_bf16, hi) + _dot(a_bf16, lo)


def _split_dot(x, b_bf16):
    hi, lo = _split(x)
    return _dot(hi, b_bf16) + _dot(lo, b_bf16)


def _rms_rows(x, g):
    ms = jnp.mean(x * x, axis=-1, keepdims=True)
    return x * lax.rsqrt(ms + EPS) * g


def _silu(x):
    return x * jax.nn.sigmoid(x)


def _softplus(x):
    return jnp.maximum(x, 0.0) + jnp.log(1.0 + jnp.exp(-jnp.abs(x)))


def _tri(c, upper=False):
    ii = lax.broadcasted_iota(jnp.int32, (c, c), 0)
    jj = lax.broadcasted_iota(jnp.int32, (c, c), 1)
    return (jj >= ii) if upper else (jj <= ii)


def _nm_kernel(x_ref, g_ref, w_ref, o_ref, xn_ref):
    @pl.when(pl.program_id(1) == 0)
    def _():
        xn_ref[...] = _rms_rows(x_ref[...], g_ref[...]).astype(BF16)

    o_ref[...] = _dot(xn_ref[...], w_ref[...]).astype(o_ref.dtype)


def norm_matmul(x, g, w, out_dtype, tm=512, tn=None):
    t, k = x.shape
    n = w.shape[1]
    tn = n if tn is None else tn
    return pl.pallas_call(
        _nm_kernel,
        grid=(t // tm, n // tn),
        in_specs=[pl.BlockSpec((tm, k), lambda i, j: (i, 0)),
                  pl.BlockSpec((1, k), lambda i, j: (0, 0)),
                  pl.BlockSpec((k, tn), lambda i, j: (0, j))],
        out_specs=pl.BlockSpec((tm, tn), lambda i, j: (i, j)),
        out_shape=jax.ShapeDtypeStruct((t, n), out_dtype),
        scratch_shapes=[pltpu.VMEM((tm, k), BF16)],
        compiler_params=_cparams(("parallel", "arbitrary")),
        name="norm_matmul",
    )(x, g.reshape(1, k), w)


def _rope_tables(seq):
    half = DIFF_HD // 2
    inv = ROPE_THETA ** (-jnp.arange(half, dtype=F32) * 2.0 / DIFF_HD)
    ang = jnp.arange(seq, dtype=jnp.int32).astype(F32)[:, None] * inv[None, :]
    return jnp.tile(jnp.cos(ang), (1, 4)), jnp.tile(jnp.sin(ang), (1, 4))


def _rope_col_perm(n_cols):
    half = DIFF_HD // 2
    g = jnp.arange(n_cols // ROPE_GROUP)[:, None, None, None] * ROPE_GROUP
    part = jnp.arange(2)[None, :, None, None] * half
    head = jnp.arange(4)[None, None, :, None] * DIFF_HD
    d = jnp.arange(half)[None, None, None, :]
    return (g + head + part + d).reshape(-1)


def _seg32_ones():
    r = jnp.arange(LANES)
    return (r[:, None] // 32 == r[None, :] // 32).astype(BF16)


def _head_lane_mask(shape, u):
    lane = lax.broadcasted_iota(jnp.int32, shape, len(shape) - 1)
    return (lane % LANES) // (DIFF_HD // 2) == u


def _nm_rope_kernel(x_ref, g_ref, w_ref, hg_ref, cos_ref, sin_ref, bd_ref, o_ref, xn_ref, *xs_refs,
                    n_rope, tn, dil):
    j = pl.program_id(1)
    tm = x_ref.shape[0]
    n = tm // dil

    @pl.when(j == 0)
    def _():
        if dil == 1:
            xn_ref[...] = _rms_rows(x_ref[...], g_ref[...]).astype(BF16)
        else:
            xs_ref = xs_refs[0]
            nch = x_ref.shape[1] // LANES
            for c in range(nch):
                xs_ref[c] = x_ref[:, c * LANES:(c + 1) * LANES]
            for r in range(dil):
                xr = jnp.concatenate([xs_ref[c, pl.ds(r, n, stride=dil), :] for c in range(nch)], axis=1)
                xn_ref[r * n:(r + 1) * n, :] = _rms_rows(xr, g_ref[...]).astype(BF16)

    def put(sl, val):
        if len(o_ref.shape) == 2:
            o_ref[:, sl] = val
        else:
            for r in range(dil):
                o_ref[r, :, sl] = val[r * n:(r + 1) * n]

    @pl.when(j < n_rope)
    def _():
        cos = cos_ref[...]
        sin = sin_ref[...]
        bd = bd_ref[...]
        xn = xn_ref[...]
        wide = 2 * ROPE_GROUP
        for cw in range(tn // wide):
            acc2 = _dot(xn, w_ref[:, cw * wide:(cw + 1) * wide])
            for c2 in range(2):
                c = 2 * cw + c2
                sl = slice(c * ROPE_GROUP, (c + 1) * ROPE_GROUP)
                a = acc2[:, c2 * ROPE_GROUP:c2 * ROPE_GROUP + LANES]
                b = acc2[:, c2 * ROPE_GROUP + LANES:(c2 + 1) * ROPE_GROUP]
                ss = _dot((a * a + b * b).astype(BF16), bd)
                inv = lax.rsqrt(ss * (1.0 / DIFF_HD) + EPS)
                a = a * inv * hg_ref[:, c * ROPE_GROUP:c * ROPE_GROUP + LANES]
                b = b * inv * hg_ref[:, c * ROPE_GROUP + LANES:(c + 1) * ROPE_GROUP]
                put(sl, jnp.concatenate([a * cos - b * sin, b * cos + a * sin], axis=1).astype(o_ref.dtype))

    @pl.when(j >= n_rope)
    def _():
        put(slice(0, tn), _dot(xn_ref[...], w_ref[...]).astype(o_ref.dtype))


def norm_matmul_rope(x, g, w, head_gain, n_rope_cols, seq, dil=None, tm=1024, tn=1024):
    t, k = x.shape
    n = w.shape[1]
    perm = jnp.concatenate([_rope_col_perm(n_rope_cols), jnp.arange(n_rope_cols, n)])
    w = w[:, perm]
    head_gain = head_gain[perm]
    cos, sin = _rope_tables(seq)
    nsb = seq // tm
    d = 1 if dil is None else dil
    if d > 1:
        cos, sin = (tb.reshape(nsb, tm // d, d, LANES).transpose(0, 2, 1, 3).reshape(seq, LANES)
                    for tb in (cos, sin))
    kern = functools.partial(_nm_rope_kernel, n_rope=n_rope_cols // tn, tn=tn, dil=d)
    if dil is None:
        out_spec = pl.BlockSpec((tm, tn), lambda i, j: (i, j))
        out_shape = jax.ShapeDtypeStruct((t, n), BF16)
    else:
        per = DIL_SUPER // tm
        out_spec = pl.BlockSpec((None, d, tm // d, tn), lambda i, j: (i // per, 0, i % per, j))
        out_shape = jax.ShapeDtypeStruct((t // DIL_SUPER, d, DIL_SUPER // d, n), BF16)
    scratch = [pltpu.VMEM((tm, k), BF16)]
    if d > 1:
        scratch.append(pltpu.VMEM((k // LANES, tm, LANES), F32))
    return pl.pallas_call(
        kern,
        grid=(t // tm, n // tn),
        in_specs=[pl.BlockSpec((tm, k), lambda i, j: (i, 0)),
                  pl.BlockSpec((1, k), lambda i, j: (0, 0)),
                  pl.BlockSpec((k, tn), lambda i, j: (0, j)),
                  pl.BlockSpec((1, tn), lambda i, j: (0, j)),
                  pl.BlockSpec((tm, LANES), lambda i, j: (i % nsb, 0)),
                  pl.BlockSpec((tm, LANES), lambda i, j: (i % nsb, 0)),
                  pl.BlockSpec((LANES, LANES), lambda i, j: (0, 0))],
        out_specs=out_spec,
        out_shape=out_shape,
        scratch_shapes=scratch,
        compiler_params=_cparams(("parallel", "arbitrary")),
        name="norm_matmul_rope",
    )(x, g.reshape(1, k), w, head_gain.reshape(1, n), cos, sin, _seg32_ones())


def _score_bound(g_q, g_k, qscale):
    return 1.05 * DIFF_HD * qscale * jnp.max(jnp.abs(g_q)) * jnp.max(jnp.abs(g_k))


def _mmres_kernel(a_ref, w_ref, h_ref, o_ref):
    o_ref[...] = h_ref[...] + _dot(a_ref[...], w_ref[...])


def matmul_residual(a, w, h, tm=512):
    t, k = a.shape
    n = w.shape[1]
    return pl.pallas_call(
        _mmres_kernel,
        grid=(t // tm,),
        in_specs=[pl.BlockSpec((tm, k), lambda i: (i, 0)),
                  pl.BlockSpec((k, n), lambda i: (0, 0)),
                  pl.BlockSpec((tm, n), lambda i: (i, 0))],
        out_specs=pl.BlockSpec((tm, n), lambda i: (i, 0)),
        out_shape=jax.ShapeDtypeStruct((t, n), F32),
        compiler_params=_cparams(("parallel",)),
        name="matmul_residual",
    )(a, w, h)


FFN_SPLIT = 1536


def _ffn_ple_kernel(x_ref, gf_ref, win_ref, wout_ref, gp_ref, wg_ref, p_ref, wp_ref, o_ref):
    x = x_ref[...]
    xn = _rms_rows(x, gf_ref[...]).astype(BF16)
    h2 = x
    for lo, hi in ((0, FFN_SPLIT), (FFN_SPLIT, D_FF)):
        gate = _dot(xn, win_ref[:, lo:hi])
        up = _dot(xn, win_ref[:, D_FF + lo:D_FF + hi])
        a = (_silu(gate) * up).astype(BF16)
        h2 = h2 + _dot(a, wout_ref[lo:hi, :])
    hn = _rms_rows(h2, gp_ref[...]).astype(BF16)
    gate2 = jax.nn.sigmoid(_dot(hn, wg_ref[...]))
    o_ref[...] = h2 + gate2 * _dot(p_ref[...].astype(BF16), wp_ref[...])


def ffn_ple_residual(h, g_ffn, w_in, w_out, g_ple, w_gate, p, w_proj, tm=512):
    t, k = h.shape

    def resident(shape):
        return pl.BlockSpec(shape, lambda i: (0, 0), pipeline_mode=pl.Buffered(1))

    return pl.pallas_call(
        _ffn_ple_kernel,
        grid=(t // tm,),
        in_specs=[pl.BlockSpec((tm, k), lambda i: (i, 0)),
                  resident((1, k)),
                  resident((k, 2 * D_FF)),
                  resident((D_FF, k)),
                  resident((1, k)),
                  resident((k, k)),
                  pl.BlockSpec((tm, PLE_DIM), lambda i: (i, 0)),
                  resident((PLE_DIM, k))],
        out_specs=pl.BlockSpec((tm, k), lambda i: (i, 0)),
        out_shape=jax.ShapeDtypeStruct((t, k), F32),
        compiler_params=_cparams(("parallel",)),
        name="ffn_ple",
    )(h, g_ffn.reshape(1, k), w_in, w_out, g_ple.reshape(1, k), w_gate, p, w_proj)


def _gla_kernel(qf_ref, kf_ref, vf_ref, zf_ref, qb_ref, kb_ref, vb_ref, zb_ref, wg_ref, bg_ref,
                of_ref, ob_ref, st_ref, la_ref, *, nchunk):
    @pl.when(pl.program_id(1) == 0)
    def _():
        st_ref[...] = jnp.zeros_like(st_ref)

    c = GLA_CHUNK
    for dirn, z_ref in enumerate((zf_ref, zb_ref)):
        x = _dot(z_ref[...].astype(BF16), wg_ref[dirn]) + bg_ref[dirn]
        la_ref[dirn] = (jnp.minimum(x, 0.0) - jnp.log(1.0 + jnp.exp(-jnp.abs(x)))) * (1.0 / GLA_TAU)
    cms = (_tri(c).astype(BF16), _tri(c, upper=True).astype(BF16))
    mks = (_tri(c), jnp.logical_not(_tri(c)))

    def chunk(dirn, hh, ci, q_ref, k_ref, v_ref, o_ref):
        rs = slice(ci * c, (ci + 1) * c)
        ks = slice(hh * GLA_HK, (hh + 1) * GLA_HK)
        vs = slice(hh * GLA_HV, (hh + 1) * GLA_HV)
        la = la_ref[dirn, rs, ks]
        b = _dot_split(cms[dirn], la)
        tot = jnp.sum(la, axis=0, keepdims=True)
        qc = q_ref[rs, ks].astype(F32)
        kc = k_ref[rs, ks].astype(F32)
        vc = v_ref[rs, vs]
        q_in = (qc * jnp.exp(b) * (GLA_HK ** -0.5)).astype(BF16)
        k_out = (kc * jnp.exp(-b)).astype(BF16)
        k_dec = (kc * jnp.exp(tot - b)).astype(BF16)
        sc = jnp.where(mks[dirn], _dot_nt(q_in, k_out), 0.0).astype(BF16)
        st = st_ref[dirn, hh]
        o = _dot(sc, vc) + _dot_nt(q_in, st.astype(BF16))
        o_ref[rs, vs] = o.astype(o_ref.dtype)
        st_ref[dirn, hh] = st * jnp.exp(tot) + _dot_tn(vc, k_dec)

    for cc in range(nchunk):
        for hh in range(GLA_HEADS):
            chunk(0, hh, cc, qf_ref, kf_ref, vf_ref, of_ref)
            chunk(1, hh, nchunk - 1 - cc, qb_ref, kb_ref, vb_ref, ob_ref)


def gla_core(proj, z, wg, bg, bsz, seq, rows=512):
    t = bsz * seq
    nblk = seq // rows

    def side(bwd):
        def rb(b, i):
            return b * nblk + (nblk - 1 - i if bwd else i)
        return [pl.BlockSpec((rows, GLA_DK), lambda b, i: (rb(b, i), 0)),
                pl.BlockSpec((rows, GLA_DK), lambda b, i: (rb(b, i), 1)),
                pl.BlockSpec((rows, GLA_DV), lambda b, i: (rb(b, i), 1)),
                pl.BlockSpec((rows, LANES), lambda b, i: (rb(b, i), 0))]

    out_f = pl.BlockSpec((rows, GLA_DV), lambda b, i: (b * nblk + i, 0))
    out_b = pl.BlockSpec((rows, GLA_DV), lambda b, i: (b * nblk + nblk - 1 - i, 0))
    oshape = jax.ShapeDtypeStruct((t, GLA_DV), BF16)
    kern = functools.partial(_gla_kernel, nchunk=rows // GLA_CHUNK)
    return pl.pallas_call(
        kern,
        grid=(bsz, nblk),
        in_specs=side(False) + side(True) + [pl.BlockSpec((2, LANES, GLA_DK), lambda b, i: (0, 0, 0)),
                                             pl.BlockSpec((2, 1, GLA_DK), lambda b, i: (0, 0, 0))],
        out_specs=[out_f, out_b],
        out_shape=[oshape, oshape],
        scratch_shapes=[pltpu.VMEM((2, GLA_HEADS, GLA_HV, GLA_HK), F32),
                        pltpu.VMEM((2, rows, GLA_DK), F32)],
        compiler_params=_cparams(("parallel", "arbitrary")),
        name="gla_core",
    )(proj, proj, proj, z, proj, proj, proj, z, wg, bg)


def _gla_out_kernel(of_ref, ob_ref, r_ref, g_ref, w_ref, h_ref, o_ref):
    o = of_ref[...].astype(F32) + ob_ref[...].astype(F32)
    parts = []
    for hh in range(GLA_HEADS):
        oh = o[:, hh * GLA_HV:(hh + 1) * GLA_HV]
        ms = jnp.mean(oh * oh, axis=-1, keepdims=True)
        parts.append(oh * lax.rsqrt(ms + EPS))
    on = jnp.concatenate(parts, axis=1) * g_ref[...]
    a = (on * _silu(r_ref[...].astype(F32))).astype(BF16)
    o_ref[...] = h_ref[...] + _dot(a, w_ref[...])


def gla_out(o_f, o_b, proj, g_out, w_out, h, tm=512):
    t, n = h.shape
    return pl.pallas_call(
        _gla_out_kernel,
        grid=(t // tm,),
        in_specs=[pl.BlockSpec((tm, GLA_DV), lambda i: (i, 0)),
                  pl.BlockSpec((tm, GLA_DV), lambda i: (i, 0)),
                  pl.BlockSpec((tm, GLA_DV), lambda i: (i, 2)),
                  pl.BlockSpec((1, GLA_DV), lambda i: (0, 0)),
                  pl.BlockSpec((GLA_DV, n), lambda i: (0, 0)),
                  pl.BlockSpec((tm, n), lambda i: (i, 0))],
        out_specs=pl.BlockSpec((tm, n), lambda i: (i, 0)),
        out_shape=jax.ShapeDtypeStruct((t, n), F32),
        compiler_params=_cparams(("parallel",)),
        name="gla_out",
    )(o_f, o_b, proj, jnp.tile(g_out, GLA_HEADS).reshape(1, GLA_DV), w_out, h)


def gla_layer(h, norm_g, w_in, w_gate_f, b_gate_f, w_gate_b, b_gate_b, g_out, w_out, bsz, seq):
    n_main = 2 * GLA_DK + 2 * GLA_DV
    proj = norm_matmul(h, norm_g, w_in[:, :n_main].astype(BF16), BF16, tm=1024, tn=1536)
    wz = jnp.pad(w_in[:, n_main:], ((0, 0), (0, LANES - 2 * GLA_RANK))).astype(BF16)
    z = norm_matmul(h, norm_g, wz, F32, tm=1024)
    wg = jnp.zeros((2, LANES, GLA_DK), F32)
    wg = wg.at[0, :GLA_RANK].set(w_gate_f).at[1, GLA_RANK:2 * GLA_RANK].set(w_gate_b).astype(BF16)
    bg = jnp.stack([b_gate_f, b_gate_b]).reshape(2, 1, GLA_DK)
    o_f, o_b = gla_core(proj, z, wg, bg, bsz, seq)
    return gla_out(o_f, o_b, proj, g_out, w_out.astype(BF16), h)


def _diff_kernel(q_ref, k_ref, v0_ref, v1_ref, lam_ref, gs_ref, o_ref, qz_ref, acc_ref, *m_refs,
                 tk, lambda_init, bounded):
    tq = q_ref.shape[0]
    q = q_ref[...]
    zero = jnp.zeros_like(q)
    for u in range(4):
        qz_ref[u] = jnp.where(_head_lane_mask(q.shape, u), q, zero)
    acc_ref[...] = jnp.zeros_like(acc_ref)
    if not bounded:
        m_ref = m_refs[0]
        m_ref[...] = jnp.full_like(m_ref, NEG_BIG)
    ones = jnp.ones((tk, LANES), BF16)

    def body(ki, carry):
        rows = pl.ds(pl.multiple_of(ki * tk, tk), tk)
        kb = k_ref[rows, :]
        vaug = [jnp.concatenate([v_ref[rows, :], ones], axis=1) for v_ref in (v0_ref, v1_ref)]
        for u in range(4):
            s = _dot_nt(qz_ref[u], kb)
            if bounded:
                acc_ref[u] += _dot(jnp.exp2(s).astype(BF16), vaug[u // 2])
            else:
                m_prev = m_ref[u]
                m_new = jnp.maximum(m_prev, jnp.max(s, axis=1, keepdims=True))
                alpha = jnp.exp2(m_prev - m_new)
                p = jnp.exp2(s - jnp.tile(m_new, (1, tk // LANES)))
                acc_ref[u] = jnp.tile(alpha, (1, 2)) * acc_ref[u] + _dot(p.astype(BF16), vaug[u // 2])
                m_ref[u] = m_new
        return carry

    lax.fori_loop(0, k_ref.shape[0] // tk, body, 0, unroll=4 if bounded else 1)

    lam4 = lam_ref[...]
    lam = (jnp.exp(jnp.sum(lam4[0:1] * lam4[1:2], axis=1, keepdims=True))
           - jnp.exp(jnp.sum(lam4[2:3] * lam4[3:4], axis=1, keepdims=True)) + lambda_init)
    for hh in range(2):
        a1 = acc_ref[2 * hh]
        a2 = acc_ref[2 * hh + 1]
        o = a1[:, :LANES] / a1[:, LANES:] - lam * (a2[:, :LANES] / a2[:, LANES:])
        ms = jnp.mean(o * o, axis=-1, keepdims=True)
        o_ref[:, hh * LANES:(hh + 1) * LANES] = (
            o * lax.rsqrt(ms + EPS) * gs_ref[...] * (1.0 - lambda_init)).astype(o_ref.dtype)


def diff_core(proj, lam4, g_sub, lambda_init, score_bound, bsz, seq, tq=512, tk=512):
    t = bsz * seq
    nq = seq // tq
    npair = DIFF_HEADS // 2
    in_specs = [pl.BlockSpec((tq, ROPE_GROUP), lambda b, h, i: (b * nq + i, h)),
                pl.BlockSpec((seq, ROPE_GROUP), lambda b, h, i: (b, npair + h)),
                pl.BlockSpec((seq, LANES), lambda b, h, i: (b, 16 + 2 * h)),
                pl.BlockSpec((seq, LANES), lambda b, h, i: (b, 17 + 2 * h)),
                pl.BlockSpec((4, DIFF_HD), lambda b, h, i: (0, 0)),
                pl.BlockSpec((1, LANES), lambda b, h, i: (0, 0))]
    out_spec = pl.BlockSpec((tq, ROPE_GROUP), lambda b, h, i: (b * nq + i, h))
    out_shape = jax.ShapeDtypeStruct((t, D_MODEL), BF16)
    args = (proj, proj, proj, proj, lam4, g_sub.reshape(1, LANES))

    def call(bounded):
        scratch = [pltpu.VMEM((4, tq, ROPE_GROUP), BF16), pltpu.VMEM((4, tq, 2 * LANES), F32)]
        if not bounded:
            scratch.append(pltpu.VMEM((4, tq, LANES), F32))
        return lambda *a: pl.pallas_call(
            functools.partial(_diff_kernel, tk=tk, lambda_init=lambda_init, bounded=bounded),
            grid=(bsz, npair, nq), in_specs=in_specs, out_specs=out_spec, out_shape=out_shape,
            scratch_shapes=scratch,
            compiler_params=_cparams(("parallel", "parallel", "arbitrary")),
            name="diff_core_bounded" if bounded else "diff_core_online",
        )(*a)

    return lax.cond(score_bound <= SAFE_LOG2, call(True), call(False), *args)


def diff_layer(h, norm_g, layer_idx, w_in, g_q, g_k, lam_q1, lam_k1, lam_q2, lam_k2, g_sub, w_out,
               bsz, seq):
    lambda_init = 0.8 - 0.6 * math.exp(-0.3 * layer_idx)
    qscale = (DIFF_HD ** -0.5) * LOG2E
    head_gain = jnp.concatenate([jnp.tile(g_q.reshape(-1), DIFF_HEADS) * qscale,
                                 jnp.tile(g_k.reshape(-1), DIFF_HEADS),
                                 jnp.ones((D_MODEL,), F32)])
    proj = norm_matmul_rope(h, norm_g, w_in.astype(BF16), head_gain, 2 * D_MODEL, seq)
    lam4 = jnp.stack([lam_q1, lam_k1, lam_q2, lam_k2])
    o = diff_core(proj, lam4, g_sub, lambda_init, _score_bound(g_q, g_k, qscale), bsz, seq)
    return matmul_residual(o, w_out.astype(BF16), h)


def _conv_kernel(prev_ref, cur_ref, next_ref, w_ref, b_ref, o_ref, buf_ref, *, halo):
    i = pl.program_id(1)
    ts = cur_ref.shape[0]
    prev = prev_ref[...].astype(F32)
    nxt = next_ref[...].astype(F32)
    buf_ref[0:halo, :] = jnp.where(i == 0, 0.0, prev)
    buf_ref[halo:halo + ts, :] = cur_ref[...].astype(F32)
    buf_ref[halo + ts:, :] = jnp.where(i == pl.num_programs(1) - 1, 0.0, nxt)
    w = w_ref[...]
    y = b_ref[...] + w[2:3] * buf_ref[halo:halo + ts, :]
    for k in (0, 1, 3, 4):
        y = y + w[k:k + 1] * buf_ref[pl.ds(halo + k - 2, ts), :]
    o_ref[...] = _silu(y).astype(o_ref.dtype)


def ssd_conv(zx, conv_w, conv_b, bsz, seq, ts=512, tc=512):
    t = bsz * seq
    halo = 16
    nrb = seq // ts
    hb = ts // halo
    nhb = seq // halo
    off = SSD_INNER // tc
    kern = functools.partial(_conv_kernel, halo=halo)
    wpad = jnp.pad(conv_w, ((0, 8 - SSD_CONV), (0, 0)))
    return pl.pallas_call(
        kern,
        grid=(bsz, nrb, SSD_CONV_DIM // tc),
        in_specs=[pl.BlockSpec((halo, tc), lambda b, i, c: (b * nhb + jnp.maximum(i * hb - 1, 0), off + c)),
                  pl.BlockSpec((ts, tc), lambda b, i, c: (b * nrb + i, off + c)),
                  pl.BlockSpec((halo, tc), lambda b, i, c: (b * nhb + jnp.minimum((i + 1) * hb, nhb - 1), off + c)),
                  pl.BlockSpec((8, tc), lambda b, i, c: (0, c)),
                  pl.BlockSpec((1, tc), lambda b, i, c: (0, c))],
        out_specs=pl.BlockSpec((ts, tc), lambda b, i, c: (b * nrb + i, c)),
        out_shape=jax.ShapeDtypeStruct((t, SSD_CONV_DIM), BF16),
        scratch_shapes=[pltpu.VMEM((ts + 2 * halo, tc), F32)],
        compiler_params=_cparams(("parallel", "parallel", "parallel")),
        name="ssd_conv",
    )(zx, zx, zx, wpad, conv_b.reshape(1, -1))


def _ssd_prep_kernel(dtr_ref, bias_ref, aneg_ref, dt_ref, ac_ref, at_ref):
    c = SSD_CHUNK
    lower = _tri(c).astype(BF16)
    upper = _tri(c, upper=True).astype(BF16)
    lane = lax.broadcasted_iota(jnp.int32, (c, LANES), 1)
    for cc in range(dtr_ref.shape[0] // c):
        rs = slice(cc * c, (cc + 1) * c)
        dt = _softplus(dtr_ref[rs, :] + bias_ref[...])
        adt = dt * aneg_ref[...]
        hi, lo = _split(adt)
        ac = jnp.where(lane < SSD_HEADS, _dot(lower, hi) + _dot(lower, lo), _dot(upper, hi) + _dot(upper, lo))
        dt_ref[rs, :] = dt
        ac_ref[rs, :] = ac
        act = ac.T
        for k in range(2 * SSD_GROUPS):
            at_ref[cc, k] = act[4 * k:4 * k + 4, :]


def ssd_prep(dtr, dt_bias, a_neg, rows=512):
    t = dtr.shape[0]
    c = SSD_CHUNK
    bias = jnp.pad(dt_bias, (0, LANES - 2 * SSD_HEADS)).reshape(1, LANES)
    aneg = jnp.pad(a_neg, (0, LANES - 2 * SSD_HEADS)).reshape(1, LANES)
    row = pl.BlockSpec((rows, LANES), lambda i: (i, 0))
    vec = pl.BlockSpec((1, LANES), lambda i: (0, 0))
    return pl.pallas_call(
        _ssd_prep_kernel,
        grid=(t // rows,),
        in_specs=[row, vec, vec],
        out_specs=[row, row, pl.BlockSpec((rows // c, 2 * SSD_GROUPS, 4, c), lambda i: (i, 0, 0, 0))],
        out_shape=[jax.ShapeDtypeStruct((t, LANES), F32), jax.ShapeDtypeStruct((t, LANES), F32),
                   jax.ShapeDtypeStruct((t // c, 2 * SSD_GROUPS, 4, c), F32)],
        compiler_params=_cparams(("parallel",)),
        name="ssd_prep",
    )(dtr, bias, aneg)


def _ssd_kernel(xf_ref, bf_ref, cf_ref, dtf_ref, acf_ref, atf_ref, e5f_ref,
                xb_ref, bb_ref, cb_ref, dtb_ref, acb_ref, atb_ref, e5b_ref,
                yf_ref, yb_ref, st_ref, *, nchunk):
    @pl.when(pl.program_id(2) == 0)
    def _():
        st_ref[...] = jnp.zeros_like(st_ref)

    c = SSD_CHUNK
    masks = (_tri(c), jnp.logical_not(_tri(c)))
    lane2 = lax.broadcasted_iota(jnp.int32, (c, 4 * SSD_P), 1)
    head_masks = [(lane2 >= hh * SSD_P) & (lane2 < (hh + 1) * SSD_P) for hh in range(4)]

    def narrow(per_head, r):
        lo = lax.broadcasted_iota(jnp.int32, (r, LANES), 1) < SSD_P
        return jnp.concatenate([jnp.where(lo, per_head[0], per_head[1]),
                                jnp.where(lo, per_head[2], per_head[3])], axis=1)

    lane1 = lax.broadcasted_iota(jnp.int32, (c, LANES), 1)
    group = pl.program_id(1)

    def chunk(dirn, ci, x_ref, b_ref, c_ref, dt_ref, ac_ref, at_ref, e5_ref, y_ref):
        rs = slice(ci * c, (ci + 1) * c)
        dt5 = _dot(dt_ref[rs, :].astype(BF16), e5_ref[...])
        ac = ac_ref[rs, :]
        first = SSD_HEADS * dirn + 4 * group
        cols = [jnp.sum(jnp.where(lane1 == first + hh, ac, 0.0), axis=1, keepdims=True)
                for hh in range(4)]
        acr = at_ref[ci]
        tots = [col[c - 1:c] if dirn == 0 else col[0:1] for col in cols]
        dt2 = narrow([dt5[:, hh * LANES:(hh + 1) * LANES] for hh in range(4)], c)
        ac2 = narrow(cols, c)
        tot2 = narrow(tots, 1)
        x = x_ref[rs, :].astype(F32)
        bm = b_ref[rs, :]
        cmat = c_ref[rs, :]
        xdt = x * dt2
        cbm = jnp.where(masks[dirn], _dot_nt(cmat, bm), 0.0)
        st = st_ref[dirn]
        ms, xs = [], []
        for hh in range(4):
            col = cols[hh]
            row = acr[hh:hh + 1, :]
            ms.append((cbm * jnp.exp(jnp.minimum(col - row, 0.0))).astype(BF16))
            xs.append(jnp.where(head_masks[hh], xdt, 0.0).astype(BF16))
        y = (_dot(cmat, st.astype(BF16)) * jnp.exp(ac2)
             + _dot(jnp.concatenate(ms, axis=1), jnp.concatenate(xs, axis=0)))
        y_ref[rs, :] = y.astype(y_ref.dtype)
        xw = (xdt * jnp.exp(tot2 - ac2)).astype(BF16)
        st_ref[dirn] = st * jnp.exp(tot2) + _dot_tn(bm, xw)

    for cc in range(nchunk):
        chunk(0, cc, xf_ref, bf_ref, cf_ref, dtf_ref, acf_ref, atf_ref, e5f_ref, yf_ref)
        chunk(1, nchunk - 1 - cc, xb_ref, bb_ref, cb_ref, dtb_ref, acb_ref, atb_ref, e5b_ref, yb_ref)


def ssd_core(xbc, dt, ac, act, bsz, seq, rows=512):
    t = bsz * seq
    nblk = seq // rows
    c = SSD_CHUNK
    hp = 4 * SSD_P
    src = 4 * jnp.arange(2 * SSD_GROUPS)[:, None] + jnp.arange(4)[None, :]
    onehot = (jnp.arange(LANES)[None, None, :] == src[..., None]).astype(BF16)
    e5 = jnp.repeat(onehot, LANES, axis=1).transpose(0, 2, 1)

    def side(bwd):
        def rb(b, i):
            return b * nblk + (nblk - 1 - i if bwd else i)
        d = int(bwd)
        return [pl.BlockSpec((rows, hp), lambda b, g, i: (rb(b, i), g)),
                pl.BlockSpec((rows, SSD_N), lambda b, g, i: (rb(b, i), 16 + g)),
                pl.BlockSpec((rows, SSD_N), lambda b, g, i: (rb(b, i), 24 + g)),
                pl.BlockSpec((rows, LANES), lambda b, g, i: (rb(b, i), 0)),
                pl.BlockSpec((rows, LANES), lambda b, g, i: (rb(b, i), 0)),
                pl.BlockSpec((rows // c, None, 4, c), lambda b, g, i: (rb(b, i), d * SSD_GROUPS + g, 0, 0)),
                pl.BlockSpec((None, LANES, 4 * LANES), lambda b, g, i: (d * SSD_GROUPS + g, 0, 0))]

    out_f = pl.BlockSpec((rows, hp), lambda b, g, i: (b * nblk + i, g))
    out_b = pl.BlockSpec((rows, hp), lambda b, g, i: (b * nblk + nblk - 1 - i, g))
    oshape = jax.ShapeDtypeStruct((t, SSD_INNER), BF16)
    kern = functools.partial(_ssd_kernel, nchunk=rows // c)
    args = (xbc, xbc, xbc, dt, ac, act, e5)
    return pl.pallas_call(
        kern,
        grid=(bsz, SSD_GROUPS, nblk),
        in_specs=side(False) + side(True),
        out_specs=[out_f, out_b],
        out_shape=[oshape, oshape],
        scratch_shapes=[pltpu.VMEM((2, SSD_N, hp), F32)],
        compiler_params=_cparams(("parallel", "parallel", "arbitrary")),
        name="ssd_core",
    )(*args, *args)


def _ssd_out_kernel(yf_ref, yb_ref, x_ref, z_ref, dsk_ref, g_ref, w_ref, h_ref, o_ref):
    y = yf_ref[...].astype(F32) + yb_ref[...].astype(F32) + x_ref[...].astype(F32) * dsk_ref[...]
    y = y * _silu(z_ref[...].astype(F32))
    gw = SSD_INNER // SSD_GROUPS
    parts = []
    for gg in range(SSD_GROUPS):
        yg = y[:, gg * gw:(gg + 1) * gw]
        ms = jnp.mean(yg * yg, axis=-1, keepdims=True)
        parts.append(yg * lax.rsqrt(ms + EPS))
    a = (jnp.concatenate(parts, axis=1) * g_ref[...]).astype(BF16)
    o_ref[...] = h_ref[...] + _dot(a, w_ref[...])


def ssd_out(y_f, y_b, xbc, zx, d_skip, g_norm, w_out, h, tm=512):
    t, n = h.shape
    wide = pl.BlockSpec((tm, SSD_INNER), lambda i: (i, 0))
    vec = pl.BlockSpec((1, SSD_INNER), lambda i: (0, 0))
    return pl.pallas_call(
        _ssd_out_kernel,
        grid=(t // tm,),
        in_specs=[wide, wide, wide, wide, vec, vec,
                  pl.BlockSpec((SSD_INNER, n), lambda i: (0, 0)),
                  pl.BlockSpec((tm, n), lambda i: (i, 0))],
        out_specs=pl.BlockSpec((tm, n), lambda i: (i, 0)),
        out_shape=jax.ShapeDtypeStruct((t, n), F32),
        compiler_params=_cparams(("parallel",)),
        name="ssd_out",
    )(y_f, y_b, xbc, zx, jnp.repeat(d_skip, SSD_P).reshape(1, SSD_INNER),
      g_norm.reshape(1, SSD_INNER), w_out, h)


def ssd_layer(h, norm_g, w_in, conv_w, conv_b, dt_bias_f, a_log_f, dt_bias_b, a_log_b, d_skip,
              g_norm, w_out, bsz, seq):
    n_main = SSD_INNER + SSD_CONV_DIM
    zx = norm_matmul(h, norm_g, w_in[:, :n_main].astype(BF16), BF16, tm=1024, tn=2048)
    wdt = jnp.pad(w_in[:, n_main:], ((0, 0), (0, LANES - 2 * SSD_HEADS))).astype(BF16)
    dtr = norm_matmul(h, norm_g, wdt, F32, tm=1024)
    xbc = ssd_conv(zx, conv_w, conv_b, bsz, seq)
    dt_bias = jnp.concatenate([dt_bias_f, dt_bias_b])
    a_neg = -jnp.exp(jnp.concatenate([a_log_f, a_log_b]))
    dt, ac, act = ssd_prep(dtr, dt_bias, a_neg)
    y_f, y_b = ssd_core(xbc, dt, ac, act, bsz, seq)
    return ssd_out(y_f, y_b, xbc, zx, d_skip, g_norm, w_out.astype(BF16), h)


def _dil_kernel(q_ref, kp_ref, kc_ref, kn_ref, vp_ref, vc_ref, vn_ref, o_ref, lse_ref, *, bounded):
    ai = pl.program_id(2)
    na = pl.num_programs(2)
    tq = q_ref.shape[0]
    side = DIL_SIDE
    kcat = jnp.concatenate([kp_ref[...], kc_ref[...], kn_ref[...]], axis=0)
    vcat = jnp.concatenate([vp_ref[...], vc_ref[...], vn_ref[...]], axis=0)
    nk = tq + 2 * side
    ii = lax.broadcasted_iota(jnp.int32, (4 * tq, nk), 0) & (tq - 1)
    jj = lax.broadcasted_iota(jnp.int32, (4 * tq, nk), 1)
    kpos = ai * tq - side + jj
    valid = (jj >= ii) & (jj <= ii + 2 * side) & (kpos >= 0) & (kpos < na * tq)
    lane = lax.broadcasted_iota(jnp.int32, (tq, ROPE_GROUP), 1)
    for g4 in range(DIL_HEADS // 4):
        gsl = slice(g4 * ROPE_GROUP, (g4 + 1) * ROPE_GROUP)
        q4 = q_ref[:, gsl]
        zq = jnp.zeros_like(q4)
        qst = jnp.concatenate([jnp.where(_head_lane_mask(q4.shape, u), q4, zq) for u in range(4)], axis=0)
        s = jnp.where(valid, _dot_nt(qst, kcat[:, gsl]), NEG_BIG)
        if bounded:
            p = jnp.exp2(s)
        else:
            m = jnp.max(s, axis=1, keepdims=True)
            p = jnp.exp2(s - m)
        den = jnp.sum(p, axis=1, keepdims=True)
        r = _dot(p.astype(BF16), vcat[:, gsl])
        lse = jnp.log(den) if bounded else m * (1.0 / LOG2E) + jnp.log(den)
        out = jnp.zeros((tq, ROPE_GROUP), F32)
        lse4 = jnp.zeros((tq, ROPE_GROUP), F32)
        for u in range(4):
            rows = slice(u * tq, (u + 1) * tq)
            mine = (lane >= u * DIL_HD) & (lane < (u + 1) * DIL_HD)
            out = jnp.where(mine, r[rows] / den[rows], out)
            lse4 = jnp.where(mine, lse[rows], lse4)
        o_ref[:, gsl] = out.astype(o_ref.dtype)
        lse_ref[:, gsl] = lse4


def dil_core(pg, dilation, score_bound, bsz, seq, tq=128):
    a = seq // dilation
    na = a // tq
    side = DIL_SIDE
    nst = seq // DIL_SUPER
    per = DIL_SUPER // dilation // tq
    per_h = DIL_SUPER // dilation // side
    nhb = a // side

    def cur(col):
        return pl.BlockSpec((None, None, tq, DIL_WIDTH),
                            lambda b, r, i: (b * nst + i // per, r, i % per, col))

    def halo(col, which):
        def imap(b, r, i):
            hb = (jnp.maximum(2 * i - 1, 0) if which == 0 else jnp.minimum(2 * i + 2, nhb - 1))
            return (b * nst + hb // per_h, r, hb % per_h, col)
        return pl.BlockSpec((None, None, side, DIL_WIDTH), imap)

    out_spec = pl.BlockSpec((None, None, tq, DIL_WIDTH), lambda b, r, i: (b * nst + i // per, r, i % per, 0))
    oshape = (bsz * nst, dilation, DIL_SUPER // dilation, DIL_WIDTH)

    def call(bounded):
        return lambda x: pl.pallas_call(
            functools.partial(_dil_kernel, bounded=bounded),
            grid=(bsz, dilation, na),
            in_specs=[cur(0), halo(1, 0), cur(1), halo(1, 1), halo(2, 0), cur(2), halo(2, 1)],
            out_specs=[out_spec, out_spec],
            out_shape=[jax.ShapeDtypeStruct(oshape, BF16), jax.ShapeDtypeStruct(oshape, F32)],
            compiler_params=_cparams(("parallel", "parallel", "parallel")),
            name=f"dil_core_{dilation}" + ("_bounded" if bounded else "_online"),
        )(x, x, x, x, x, x, x)

    return lax.cond(score_bound <= SAFE_LOG2, call(True), call(False), pg)


def _dil_out_kernel(o0_ref, o1_ref, o2_ref, l0_ref, l1_ref, l2_ref, w_ref, h_ref, o_ref,
                    so_ref, sl_ref):
    tm = h_ref.shape[0]
    nch = DIL_WIDTH // LANES
    for gi, (og_ref, lg_ref) in enumerate(((o1_ref, l1_ref), (o2_ref, l2_ref))):
        d = DIL_PAIRS[gi + 1][1]
        n = tm // d
        for r in range(d):
            ov = og_ref[r].astype(F32)
            lv = lg_ref[r]
            for c in range(nch):
                so_ref[gi, c, pl.ds(r, n, stride=d), :] = ov[:, c * LANES:(c + 1) * LANES]
                sl_ref[gi, c, pl.ds(r, n, stride=d), :] = lv[:, c * LANES:(c + 1) * LANES]
    parts = []
    for c in range(nch):
        sl = slice(c * LANES, (c + 1) * LANES)
        l0 = l0_ref[0, :, sl]
        l1 = sl_ref[0, c]
        l2 = sl_ref[1, c]
        mx = jnp.maximum(jnp.maximum(l0, l1), l2)
        e0 = jnp.exp(l0 - mx)
        e1 = jnp.exp(l1 - mx)
        e2 = jnp.exp(l2 - mx)
        num = e0 * o0_ref[0, :, sl].astype(F32) + e1 * so_ref[0, c] + e2 * so_ref[1, c]
        parts.append((num / (e0 + e1 + e2)).astype(BF16))
    o_ref[...] = h_ref[...] + _dot(jnp.concatenate(parts, axis=1), w_ref[...])


def dil_out(os, lses, w_out, h, tm=512):
    t, n = h.shape
    per = DIL_SUPER // tm

    def grp(d):
        return pl.BlockSpec((None, d, tm // d, DIL_WIDTH), lambda i: (i // per, 0, i % per, 0))

    specs = [grp(d) for _, d in DIL_PAIRS]
    row = pl.BlockSpec((tm, n), lambda i: (i, 0))
    return pl.pallas_call(
        _dil_out_kernel,
        grid=(t // tm,),
        in_specs=specs + specs + [pl.BlockSpec((DIL_WIDTH, n), lambda i: (0, 0)), row],
        out_specs=row,
        out_shape=jax.ShapeDtypeStruct((t, n), F32),
        scratch_shapes=[pltpu.VMEM((2, DIL_WIDTH // LANES, tm, LANES), F32),
                        pltpu.VMEM((2, DIL_WIDTH // LANES, tm, LANES), F32)],
        compiler_params=_cparams(("parallel",)),
        name="dil_out",
    )(*os, *lses, w_out, h)


def dil_layer(h, norm_g, w_in, g_q, g_k, w_out, bsz, seq):
    qscale = (DIL_HD ** -0.5) * LOG2E
    w4 = w_in.reshape(D_MODEL, 3, len(DIL_PAIRS), DIL_WIDTH)
    os, lses = [], []
    for gi, (_, dilation) in enumerate(DIL_PAIRS):
        wg = w4[:, :, gi, :].reshape(D_MODEL, 3 * DIL_WIDTH).astype(BF16)
        head_gain = jnp.concatenate([jnp.tile(g_q[gi], DIL_HEADS) * qscale,
                                     jnp.tile(g_k[gi], DIL_HEADS),
                                     jnp.ones((DIL_WIDTH,), F32)])
        pg = norm_matmul_rope(h, norm_g, wg, head_gain, 2 * DIL_WIDTH, seq, dil=dilation)
        o, lse = dil_core(pg, dilation, _score_bound(g_q[gi], g_k[gi], qscale), bsz, seq)
        os.append(o)
        lses.append(lse)
    return dil_out(os, lses, w_out.astype(BF16), h)


def kernel(x, p, norm_mix, norm_ffn, ffn_w_in, ffn_w_out, ple_norm, ple_w_gate, ple_w_proj, gla_w_in, gla_w_gate_f, gla_b_gate_f, gla_w_gate_b, gla_b_gate_b, gla_g_out, gla_w_out, diff_w_in, diff_g_q, diff_g_k, diff_lam_q1, diff_lam_k1, diff_lam_q2, diff_lam_k2, diff_g_sub, diff_w_out, ssd_w_in, ssd_conv_w, ssd_conv_b, ssd_dt_bias_f, ssd_a_log_f, ssd_dt_bias_b, ssd_a_log_b, ssd_d, ssd_g_norm, ssd_w_out, dil_w_in, dil_g_q, dil_g_k, dil_w_out):
    bsz, seq, dm = x.shape
    t = bsz * seq
    h = x.reshape(t, dm)
    for i in range(DEPTH):
        kind = i % 4
        j = i // 4
        if kind == 0:
            h = gla_layer(h, norm_mix[i], gla_w_in[j], gla_w_gate_f[j], gla_b_gate_f[j],
                          gla_w_gate_b[j], gla_b_gate_b[j], gla_g_out[j], gla_w_out[j], bsz, seq)
        elif kind == 1:
            h = diff_layer(h, norm_mix[i], i, diff_w_in[j], diff_g_q[j], diff_g_k[j], diff_lam_q1[j],
                           diff_lam_k1[j], diff_lam_q2[j], diff_lam_k2[j], diff_g_sub[j],
                           diff_w_out[j], bsz, seq)
        elif kind == 2:
            h = ssd_layer(h, norm_mix[i], ssd_w_in[j], ssd_conv_w[j], ssd_conv_b[j],
                          ssd_dt_bias_f[j], ssd_a_log_f[j], ssd_dt_bias_b[j], ssd_a_log_b[j],
                          ssd_d[j], ssd_g_norm[j], ssd_w_out[j], bsz, seq)
        else:
            h = dil_layer(h, norm_mix[i], dil_w_in[j], dil_g_q[j], dil_g_k[j], dil_w_out[j], bsz, seq)
        h = ffn_ple_residual(h, norm_ffn[i], ffn_w_in[i].astype(BF16), ffn_w_out[i].astype(BF16),
                             ple_norm[i], ple_w_gate[i].astype(BF16), p[i].reshape(t, PLE_DIM),
                             ple_w_proj[i].astype(BF16))
    return h.reshape(bsz, seq, dm)
```

```python
import functools
import math

import jax
import jax.numpy as jnp
from jax import lax
from jax.experimental import pallas as pl
from jax.experimental.pallas import tpu as pltpu

F32 = jnp.float32
BF16 = jnp.bfloat16

D_MODEL = 1024
DEPTH = 4
PLE_DIM = 256
ROPE_THETA = 10000.0
EPS = 1e-6
D_FF = 2816
LOG2E = math.log2(math.e)
NEG_BIG = -1e30

GLA_HEADS = 4
GLA_DK = 512
GLA_DV = 1024
GLA_HK = 128
GLA_HV = 256
GLA_RANK = 16
GLA_TAU = 16.0
GLA_CHUNK = 64

DIFF_HEADS = 8
DIFF_HD = 64

SSD_INNER = 2048
SSD_P = 64
SSD_HEADS = 32
SSD_GROUPS = 8
SSD_N = 128
SSD_CONV = 5
SSD_CHUNK = 128
SSD_CONV_DIM = 4096

DIL_PAIRS = ((128, 1), (512, 4), (2048, 16))
DIL_HEADS = 16
DIL_HD = 64
DIL_WIDTH = 1024
DIL_SIDE = 64
DIL_SUPER = 2048

LANES = 128
ROPE_GROUP = 256
SAFE_LOG2 = 60.0


def _cparams(sem, vmem_mb=None):
    kw = dict(dimension_semantics=sem)
    if vmem_mb is not None:
        kw["vmem_limit_bytes"] = vmem_mb << 20
    return pltpu.CompilerParams(**kw)


def _dot(a, b):
    return jnp.dot(a, b, preferred_element_type=F32)


def _dot_nt(a, b):
    return lax.dot_general(a, b, (((1,), (1,)), ((), ())), preferred_element_type=F32)


def _dot_tn(a, b):
    return lax.dot_general(a, b, (((0,), (0,)), ((), ())), preferred_element_type=F32)


def _split(x):
    hi = x.astype(BF16)
    return hi, (x - hi.astype(F32)).astype(BF16)


def _dot_split(a_bf16, x):
    hi, lo = _split(x)
    return _dot(a_bf16, hi) + _dot(a_bf16, lo)


def _split_dot(x, b_bf16):
    hi, lo = _split(x)
    return _dot(hi, b_bf16) + _dot(lo, b_bf16)


def _rms_rows(x, g):
    ms = jnp.mean(x * x, axis=-1, keepdims=True)
    return x * lax.rsqrt(ms + EPS) * g


def _silu(x):
    return x * jax.nn.sigmoid(x)


def _softplus(x):
    return jnp.maximum(x, 0.0) + jnp.log(1.0 + jnp.exp(-jnp.abs(x)))


def _tri(c, upper=False):
    ii = lax.broadcasted_iota(jnp.int32, (c, c), 0)
    jj = lax.broadcasted_iota(jnp.int32, (c, c), 1)
    return (jj >= ii) if upper else (jj <= ii)


def _resident(shape):
    return pl.BlockSpec(shape, lambda i: (0,) * len(shape), pipeline_mode=pl.Buffered(1))


PROJ_COLS = 1024


def _nm_kernel(x_ref, g_ref, w_ref, ws_ref, o_ref, os_ref):
    xn = _rms_rows(x_ref[...], g_ref[...]).astype(BF16)
    for c in range(w_ref.shape[1] // PROJ_COLS):
        sl = slice(c * PROJ_COLS, (c + 1) * PROJ_COLS)
        o_ref[:, sl] = _dot(xn, w_ref[:, sl]).astype(o_ref.dtype)
    os_ref[...] = _dot(xn, ws_ref[...])


def norm_matmul(x, g, w, w_side, tm=512):
    t, k = x.shape
    n = w.shape[1]
    return pl.pallas_call(
        _nm_kernel,
        grid=(t // tm,),
        in_specs=[pl.BlockSpec((tm, k), lambda i: (i, 0)),
                  _resident((1, k)), _resident((k, n)), _resident((k, LANES))],
        out_specs=[pl.BlockSpec((tm, n), lambda i: (i, 0)), pl.BlockSpec((tm, LANES), lambda i: (i, 0))],
        out_shape=[jax.ShapeDtypeStruct((t, n), BF16), jax.ShapeDtypeStruct((t, LANES), F32)],
        compiler_params=_cparams(("parallel",)),
        name="norm_matmul",
    )(x, g.reshape(1, k), w, w_side)


def _rope_tables(seq):
    half = DIFF_HD // 2
    inv = ROPE_THETA ** (-jnp.arange(half, dtype=F32) * 2.0 / DIFF_HD)
    ang = jnp.arange(seq, dtype=jnp.int32).astype(F32)[:, None] * inv[None, :]
    return jnp.tile(jnp.cos(ang), (1, 4)), jnp.tile(jnp.sin(ang), (1, 4))


def _rope_col_perm(n_cols):
    half = DIFF_HD // 2
    g = jnp.arange(n_cols // ROPE_GROUP)[:, None, None, None] * ROPE_GROUP
    part = jnp.arange(2)[None, :, None, None] * half
    head = jnp.arange(4)[None, None, :, None] * DIFF_HD
    d = jnp.arange(half)[None, None, None, :]
    return (g + head + part + d).reshape(-1)


def _seg32_ones():
    r = jnp.arange(LANES)
    return (r[:, None] // 32 == r[None, :] // 32).astype(BF16)


def _head_lane_mask(shape, u):
    lane = lax.broadcasted_iota(jnp.int32, shape, len(shape) - 1)
    return (lane % LANES) // (DIFF_HD // 2) == u


def _nm_rope_kernel(x_ref, g_ref, w_ref, hg_ref, cos_ref, sin_ref, bd_ref, o_ref, xn_ref, *xs_refs,
                    n_rope_cols, dil):
    tm = x_ref.shape[0]
    n = tm // dil
    if dil == 1:
        xn_ref[...] = _rms_rows(x_ref[...], g_ref[...]).astype(BF16)
    else:
        xs_ref = xs_refs[0]
        nch = x_ref.shape[1] // LANES
        for c in range(nch):
            xs_ref[c] = x_ref[:, c * LANES:(c + 1) * LANES]
        for r in range(dil):
            xr = jnp.concatenate([xs_ref[c, pl.ds(r, n, stride=dil), :] for c in range(nch)], axis=1)
            xn_ref[r * n:(r + 1) * n, :] = _rms_rows(xr, g_ref[...]).astype(BF16)

    def put(sl, val):
        if len(o_ref.shape) == 2:
            o_ref[:, sl] = val
        else:
            for r in range(dil):
                o_ref[r, :, sl] = val[r * n:(r + 1) * n]

    cos = cos_ref[...]
    sin = sin_ref[...]
    bd = bd_ref[...]
    xn = xn_ref[...]
    wide = 2 * ROPE_GROUP
    for cw in range(n_rope_cols // wide):
        acc2 = _dot(xn, w_ref[:, cw * wide:(cw + 1) * wide])
        for c2 in range(2):
            c = 2 * cw + c2
            sl = slice(c * ROPE_GROUP, (c + 1) * ROPE_GROUP)
            a = acc2[:, c2 * ROPE_GROUP:c2 * ROPE_GROUP + LANES]
            b = acc2[:, c2 * ROPE_GROUP + LANES:(c2 + 1) * ROPE_GROUP]
            ss = _dot((a * a + b * b).astype(BF16), bd)
            inv = lax.rsqrt(ss * (1.0 / DIFF_HD) + EPS)
            a = a * inv * hg_ref[:, c * ROPE_GROUP:c * ROPE_GROUP + LANES]
            b = b * inv * hg_ref[:, c * ROPE_GROUP + LANES:(c + 1) * ROPE_GROUP]
            put(sl, jnp.concatenate([a * cos - b * sin, b * cos + a * sin], axis=1).astype(o_ref.dtype))
    for c in range(n_rope_cols // PROJ_COLS, w_ref.shape[1] // PROJ_COLS):
        sl = slice(c * PROJ_COLS, (c + 1) * PROJ_COLS)
        put(sl, _dot(xn, w_ref[:, sl]).astype(o_ref.dtype))


def norm_matmul_rope(x, g, w, rope_gain, tables, seq, dil=None, tm=512):
    t, k = x.shape
    n = w.shape[1]
    n_rope_cols = rope_gain.shape[0]
    perm = _rope_col_perm(n_rope_cols)
    w = jnp.concatenate([w[:, perm], w[:, n_rope_cols:]], axis=1)
    rope_gain = rope_gain[perm]
    cos, sin = tables
    nsb = seq // tm
    d = 1 if dil is None else dil
    if d > 1:
        cos, sin = (tb.reshape(nsb, tm // d, d, LANES).transpose(0, 2, 1, 3).reshape(seq, LANES)
                    for tb in (cos, sin))
    kern = functools.partial(_nm_rope_kernel, n_rope_cols=n_rope_cols, dil=d)
    if dil is None:
        out_spec = pl.BlockSpec((tm, n), lambda i: (i, 0))
        out_shape = jax.ShapeDtypeStruct((t, n), BF16)
    else:
        per = DIL_SUPER // tm
        out_spec = pl.BlockSpec((None, d, tm // d, n), lambda i: (i // per, 0, i % per, 0))
        out_shape = jax.ShapeDtypeStruct((t // DIL_SUPER, d, DIL_SUPER // d, n), BF16)
    scratch = [pltpu.VMEM((tm, k), BF16)]
    if d > 1:
        scratch.append(pltpu.VMEM((k // LANES, tm, LANES), F32))
    return pl.pallas_call(
        kern,
        grid=(t // tm,),
        in_specs=[pl.BlockSpec((tm, k), lambda i: (i, 0)),
                  _resident((1, k)), _resident((k, n)), _resident((1, n_rope_cols)),
                  pl.BlockSpec((tm, LANES), lambda i: (i % nsb, 0)),
                  pl.BlockSpec((tm, LANES), lambda i: (i % nsb, 0)),
                  _resident((LANES, LANES))],
        out_specs=out_spec,
        out_shape=out_shape,
        scratch_shapes=scratch,
        compiler_params=_cparams(("parallel",)),
        name="norm_matmul_rope",
    )(x, g.reshape(1, k), w, rope_gain.reshape(1, n_rope_cols), cos, sin, _seg32_ones())


def _score_bound(g_q, g_k, qscale):
    return 1.05 * DIFF_HD * qscale * jnp.max(jnp.abs(g_q)) * jnp.max(jnp.abs(g_k))


FFN_SPLIT = 1536


def _ffn_ple_kernel(*refs, mixer_act):
    x_ref, gf_ref, win_ref, wout_ref, gp_ref, wg_ref, p_ref, wp_ref, o_ref = refs[-9:]
    x = x_ref[...]
    if mixer_act is not None:
        x = x + _dot(mixer_act(*refs[:-10]), refs[-10][...])
    xn = _rms_rows(x, gf_ref[...]).astype(BF16)
    h2 = x
    for lo, hi in ((0, FFN_SPLIT), (FFN_SPLIT, D_FF)):
        gate = _dot(xn, win_ref[:, lo:hi])
        up = _dot(xn, win_ref[:, D_FF + lo:D_FF + hi])
        a = (_silu(gate) * up).astype(BF16)
        h2 = h2 + _dot(a, wout_ref[lo:hi, :])
    hn = _rms_rows(h2, gp_ref[...]).astype(BF16)
    gate2 = jax.nn.sigmoid(_dot(hn, wg_ref[...]))
    o_ref[...] = h2 + gate2 * _dot(p_ref[...].astype(BF16), wp_ref[...])


FFN_ROWS = 512


def ffn_ple_residual(h, ffn_ple_params, mixer=None):
    g_ffn, w_in, w_out, g_ple, w_gate, p, w_proj = ffn_ple_params
    t, k = h.shape
    tm = FFN_ROWS
    act, m_ops, m_specs = None, (), []
    if mixer is not None:
        act, m_ops, m_specs, w_mix = mixer
        m_ops = tuple(m_ops) + (w_mix,)
        m_specs = list(m_specs) + [_resident(w_mix.shape)]
    return pl.pallas_call(
        functools.partial(_ffn_ple_kernel, mixer_act=act),
        grid=(t // tm,),
        in_specs=m_specs + [pl.BlockSpec((tm, k), lambda i: (i, 0)),
                            _resident((1, k)),
                            _resident((k, 2 * D_FF)),
                            _resident((D_FF, k)),
                            _resident((1, k)),
                            _resident((k, k)),
                            pl.BlockSpec((tm, PLE_DIM), lambda i: (i, 0)),
                            _resident((PLE_DIM, k))],
        out_specs=pl.BlockSpec((tm, k), lambda i: (i, 0)),
        out_shape=jax.ShapeDtypeStruct((t, k), F32),
        compiler_params=_cparams(("parallel",)),
        name="ffn_ple",
    )(*m_ops, h, g_ffn.reshape(1, k), w_in, w_out, g_ple.reshape(1, k), w_gate, p, w_proj)


def _gla_kernel(qf_ref, kf_ref, vf_ref, zf_ref, qb_ref, kb_ref, vb_ref, zb_ref, wg_ref, bg_ref,
                of_ref, ob_ref, st_ref, la_ref, *, nchunk):
    @pl.when(pl.program_id(1) == 0)
    def _():
        st_ref[...] = jnp.zeros_like(st_ref)

    c = GLA_CHUNK
    for dirn, z_ref in enumerate((zf_ref, zb_ref)):
        x = _dot(z_ref[...].astype(BF16), wg_ref[dirn]) + bg_ref[dirn]
        la_ref[dirn] = (jnp.minimum(x, 0.0) - jnp.log(1.0 + jnp.exp(-jnp.abs(x)))) * (1.0 / GLA_TAU)
    cms = (_tri(c).astype(BF16), _tri(c, upper=True).astype(BF16))
    mks = (_tri(c), jnp.logical_not(_tri(c)))

    def chunk(dirn, hh, ci, q_ref, k_ref, v_ref, o_ref):
        rs = slice(ci * c, (ci + 1) * c)
        ks = slice(hh * GLA_HK, (hh + 1) * GLA_HK)
        vs = slice(hh * GLA_HV, (hh + 1) * GLA_HV)
        la = la_ref[dirn, rs, ks]
        b = _dot_split(cms[dirn], la)
        tot = jnp.sum(la, axis=0, keepdims=True)
        qc = q_ref[rs, ks].astype(F32)
        kc = k_ref[rs, ks].astype(F32)
        vc = v_ref[rs, vs]
        q_in = (qc * jnp.exp(b) * (GLA_HK ** -0.5)).astype(BF16)
        k_out = (kc * jnp.exp(-b)).astype(BF16)
        k_dec = (kc * jnp.exp(tot - b)).astype(BF16)
        sc = jnp.where(mks[dirn], _dot_nt(q_in, k_out), 0.0).astype(BF16)
        st = st_ref[dirn, hh]
        o = _dot(sc, vc) + _dot_nt(q_in, st.astype(BF16))
        o_ref[rs, vs] = o.astype(o_ref.dtype)
        st_ref[dirn, hh] = st * jnp.exp(tot) + _dot_tn(vc, k_dec)

    for cc in range(nchunk):
        for hh in range(GLA_HEADS):
            chunk(0, hh, cc, qf_ref, kf_ref, vf_ref, of_ref)
            chunk(1, hh, nchunk - 1 - cc, qb_ref, kb_ref, vb_ref, ob_ref)


def gla_core(proj, z, wg, bg, bsz, seq, rows=512):
    t = bsz * seq
    nblk = seq // rows

    def side(bwd):
        def rb(b, i):
            return b * nblk + (nblk - 1 - i if bwd else i)
        return [pl.BlockSpec((rows, GLA_DK), lambda b, i: (rb(b, i), 0)),
                pl.BlockSpec((rows, GLA_DK), lambda b, i: (rb(b, i), 1)),
                pl.BlockSpec((rows, GLA_DV), lambda b, i: (rb(b, i), 1)),
                pl.BlockSpec((rows, LANES), lambda b, i: (rb(b, i), 0))]

    out_f = pl.BlockSpec((rows, GLA_DV), lambda b, i: (b * nblk + i, 0))
    out_b = pl.BlockSpec((rows, GLA_DV), lambda b, i: (b * nblk + nblk - 1 - i, 0))
    oshape = jax.ShapeDtypeStruct((t, GLA_DV), BF16)
    kern = functools.partial(_gla_kernel, nchunk=rows // GLA_CHUNK)
    return pl.pallas_call(
        kern,
        grid=(bsz, nblk),
        in_specs=side(False) + side(True) + [pl.BlockSpec((2, LANES, GLA_DK), lambda b, i: (0, 0, 0)),
                                             pl.BlockSpec((2, 1, GLA_DK), lambda b, i: (0, 0, 0))],
        out_specs=[out_f, out_b],
        out_shape=[oshape, oshape],
        scratch_shapes=[pltpu.VMEM((2, GLA_HEADS, GLA_HV, GLA_HK), F32),
                        pltpu.VMEM((2, rows, GLA_DK), F32)],
        compiler_params=_cparams(("parallel", "arbitrary")),
        name="gla_core",
    )(proj, proj, proj, z, proj, proj, proj, z, wg, bg)


def _gla_act(of_ref, ob_ref, r_ref, g_ref):
    o = of_ref[...].astype(F32) + ob_ref[...].astype(F32)
    parts = []
    for hh in range(GLA_HEADS):
        oh = o[:, hh * GLA_HV:(hh + 1) * GLA_HV]
        ms = jnp.mean(oh * oh, axis=-1, keepdims=True)
        parts.append(oh * lax.rsqrt(ms + EPS))
    on = jnp.concatenate(parts, axis=1) * g_ref[...]
    return (on * _silu(r_ref[...].astype(F32))).astype(BF16)


def gla_mixer_out(o_f, o_b, proj, g_out, w_out):
    row = pl.BlockSpec((FFN_ROWS, GLA_DV), lambda i: (i, 0))
    specs = [row, row, pl.BlockSpec((FFN_ROWS, GLA_DV), lambda i: (i, 2)), _resident((1, GLA_DV))]
    return _gla_act, (o_f, o_b, proj, jnp.tile(g_out, GLA_HEADS).reshape(1, GLA_DV)), specs, w_out


def gla_layer(h, norm_g, w_in, w_gate_f, b_gate_f, w_gate_b, b_gate_b, g_out, w_out, bsz, seq):
    n_main = 2 * GLA_DK + 2 * GLA_DV
    wz = jnp.pad(w_in[:, n_main:], ((0, 0), (0, LANES - 2 * GLA_RANK))).astype(BF16)
    proj, z = norm_matmul(h, norm_g, w_in[:, :n_main].astype(BF16), wz)
    wg = jnp.zeros((2, LANES, GLA_DK), F32)
    wg = wg.at[0, :GLA_RANK].set(w_gate_f).at[1, GLA_RANK:2 * GLA_RANK].set(w_gate_b).astype(BF16)
    bg = jnp.stack([b_gate_f, b_gate_b]).reshape(2, 1, GLA_DK)
    o_f, o_b = gla_core(proj, z, wg, bg, bsz, seq)
    return h, gla_mixer_out(o_f, o_b, proj, g_out, w_out.astype(BF16))


def _diff_kernel(q_ref, k_ref, v0_ref, v1_ref, lam_ref, gs_ref, o_ref, qz_ref, acc_ref, *m_refs,
                 tk, lambda_init, bounded):
    tq = q_ref.shape[0]
    q = q_ref[...]
    zero = jnp.zeros_like(q)
    for u in range(4):
        qz_ref[u] = jnp.where(_head_lane_mask(q.shape, u), q, zero)
    acc_ref[...] = jnp.zeros_like(acc_ref)
    if not bounded:
        m_ref = m_refs[0]
        m_ref[...] = jnp.full_like(m_ref, NEG_BIG)
    ones = jnp.ones((tk, LANES), BF16)

    def body(ki, carry):
        rows = pl.ds(pl.multiple_of(ki * tk, tk), tk)
        kb = k_ref[rows, :]
        vaug = [jnp.concatenate([v_ref[rows, :], ones], axis=1) for v_ref in (v0_ref, v1_ref)]
        for u in range(4):
            s = _dot_nt(qz_ref[u], kb)
            if bounded:
                acc_ref[u] += _dot(jnp.exp2(s).astype(BF16), vaug[u // 2])
            else:
                m_prev = m_ref[u]
                m_new = jnp.maximum(m_prev, jnp.max(s, axis=1, keepdims=True))
                alpha = jnp.exp2(m_prev - m_new)
                p = jnp.exp2(s - jnp.tile(m_new, (1, tk // LANES)))
                acc_ref[u] = jnp.tile(alpha, (1, 2)) * acc_ref[u] + _dot(p.astype(BF16), vaug[u // 2])
                m_ref[u] = m_new
        return carry

    lax.fori_loop(0, k_ref.shape[0] // tk, body, 0, unroll=4 if bounded else 1)

    lam4 = lam_ref[...]
    lam = (jnp.exp(jnp.sum(lam4[0:1] * lam4[1:2], axis=1, keepdims=True))
           - jnp.exp(jnp.sum(lam4[2:3] * lam4[3:4], axis=1, keepdims=True)) + lambda_init)
    for hh in range(2):
        a1 = acc_ref[2 * hh]
        a2 = acc_ref[2 * hh + 1]
        o = a1[:, :LANES] / a1[:, LANES:] - lam * (a2[:, :LANES] / a2[:, LANES:])
        ms = jnp.mean(o * o, axis=-1, keepdims=True)
        o_ref[:, hh * LANES:(hh + 1) * LANES] = (
            o * lax.rsqrt(ms + EPS) * gs_ref[...] * (1.0 - lambda_init)).astype(o_ref.dtype)


def diff_core(proj, lam4, g_sub, lambda_init, score_bound, bsz, seq, tq=512, tk=512):
    t = bsz * seq
    nq = seq // tq
    npair = DIFF_HEADS // 2
    in_specs = [pl.BlockSpec((tq, ROPE_GROUP), lambda b, h, i: (b * nq + i, h)),
                pl.BlockSpec((seq, ROPE_GROUP), lambda b, h, i: (b, npair + h)),
                pl.BlockSpec((seq, LANES), lambda b, h, i: (b, 16 + 2 * h)),
                pl.BlockSpec((seq, LANES), lambda b, h, i: (b, 17 + 2 * h)),
                pl.BlockSpec((4, DIFF_HD), lambda b, h, i: (0, 0)),
                pl.BlockSpec((1, LANES), lambda b, h, i: (0, 0))]
    out_spec = pl.BlockSpec((tq, ROPE_GROUP), lambda b, h, i: (b * nq + i, h))
    out_shape = jax.ShapeDtypeStruct((t, D_MODEL), BF16)
    args = (proj, proj, proj, proj, lam4, g_sub.reshape(1, LANES))

    def call(bounded):
        scratch = [pltpu.VMEM((4, tq, ROPE_GROUP), BF16), pltpu.VMEM((4, tq, 2 * LANES), F32)]
        if not bounded:
            scratch.append(pltpu.VMEM((4, tq, LANES), F32))
        return lambda *a: pl.pallas_call(
            functools.partial(_diff_kernel, tk=tk, lambda_init=lambda_init, bounded=bounded),
            grid=(bsz, npair, nq), in_specs=in_specs, out_specs=out_spec, out_shape=out_shape,
            scratch_shapes=scratch,
            compiler_params=_cparams(("parallel", "parallel", "arbitrary")),
            name="diff_core_bounded" if bounded else "diff_core_online",
        )(*a)

    return lax.cond(score_bound <= SAFE_LOG2, call(True), call(False), *args)


def diff_layer(h, norm_g, layer_idx, w_in, g_q, g_k, lam_q1, lam_k1, lam_q2, lam_k2, g_sub, w_out,
               bsz, seq):
    lambda_init = 0.8 - 0.6 * math.exp(-0.3 * layer_idx)
    qscale = (DIFF_HD ** -0.5) * LOG2E
    rope_gain = jnp.concatenate([jnp.tile(g_q.reshape(-1), DIFF_HEADS) * qscale,
                                 jnp.tile(g_k.reshape(-1), DIFF_HEADS)])
    proj = norm_matmul_rope(h, norm_g, w_in.astype(BF16), rope_gain, _rope_tables(seq), seq)
    lam4 = jnp.stack([lam_q1, lam_k1, lam_q2, lam_k2])
    o = diff_core(proj, lam4, g_sub, lambda_init, _score_bound(g_q, g_k, qscale), bsz, seq)
    row = pl.BlockSpec((FFN_ROWS, D_MODEL), lambda i: (i, 0))
    return h, (lambda o_ref: o_ref[...], (o,), [row], w_out.astype(BF16))


def _conv_kernel(prev_ref, cur_ref, next_ref, w_ref, b_ref, o_ref, buf_ref, *, halo):
    i = pl.program_id(1)
    ts = cur_ref.shape[0]
    prev = prev_ref[...].astype(F32)
    nxt = next_ref[...].astype(F32)
    buf_ref[0:halo, :] = jnp.where(i == 0, 0.0, prev)
    buf_ref[halo:halo + ts, :] = cur_ref[...].astype(F32)
    buf_ref[halo + ts:, :] = jnp.where(i == pl.num_programs(1) - 1, 0.0, nxt)
    w = w_ref[...]
    y = b_ref[...] + w[2:3] * buf_ref[halo:halo + ts, :]
    for k in (0, 1, 3, 4):
        y = y + w[k:k + 1] * buf_ref[pl.ds(halo + k - 2, ts), :]
    o_ref[...] = _silu(y).astype(o_ref.dtype)


def ssd_conv(zx, conv_w, conv_b, bsz, seq, ts=512, tc=512):
    t = bsz * seq
    halo = 16
    nrb = seq // ts
    hb = ts // halo
    nhb = seq // halo
    off = SSD_INNER // tc
    kern = functools.partial(_conv_kernel, halo=halo)
    wpad = jnp.pad(conv_w, ((0, 8 - SSD_CONV), (0, 0)))
    return pl.pallas_call(
        kern,
        grid=(bsz, nrb, SSD_CONV_DIM // tc),
        in_specs=[pl.BlockSpec((halo, tc), lambda b, i, c: (b * nhb + jnp.maximum(i * hb - 1, 0), off + c)),
                  pl.BlockSpec((ts, tc), lambda b, i, c: (b * nrb + i, off + c)),
                  pl.BlockSpec((halo, tc), lambda b, i, c: (b * nhb + jnp.minimum((i + 1) * hb, nhb - 1), off + c)),
                  pl.BlockSpec((8, tc), lambda b, i, c: (0, c)),
                  pl.BlockSpec((1, tc), lambda b, i, c: (0, c))],
        out_specs=pl.BlockSpec((ts, tc), lambda b, i, c: (b * nrb + i, c)),
        out_shape=jax.ShapeDtypeStruct((t, SSD_CONV_DIM), BF16),
        scratch_shapes=[pltpu.VMEM((ts + 2 * halo, tc), F32)],
        compiler_params=_cparams(("parallel", "parallel", "parallel")),
        name="ssd_conv",
    )(zx, zx, zx, wpad, conv_b.reshape(1, -1))


def _ssd_prep_kernel(dtr_ref, bias_ref, aneg_ref, dt_ref, ac_ref, at_ref):
    c = SSD_CHUNK
    lower = _tri(c).astype(BF16)
    upper = _tri(c, upper=True).astype(BF16)
    lane = lax.broadcasted_iota(jnp.int32, (c, LANES), 1)
    for cc in range(dtr_ref.shape[0] // c):
        rs = slice(cc * c, (cc + 1) * c)
        dt = _softplus(dtr_ref[rs, :] + bias_ref[...])
        adt = dt * aneg_ref[...]
        hi, lo = _split(adt)
        ac = jnp.where(lane < SSD_HEADS, _dot(lower, hi) + _dot(lower, lo), _dot(upper, hi) + _dot(upper, lo))
        dt_ref[rs, :] = dt
        ac_ref[rs, :] = ac
        act = ac.T
        for k in range(2 * SSD_GROUPS):
            at_ref[cc, k] = act[4 * k:4 * k + 4, :]


def ssd_prep(dtr, dt_bias, a_neg, rows=512):
    t = dtr.shape[0]
    c = SSD_CHUNK
    bias = jnp.pad(dt_bias, (0, LANES - 2 * SSD_HEADS)).reshape(1, LANES)
    aneg = jnp.pad(a_neg, (0, LANES - 2 * SSD_HEADS)).reshape(1, LANES)
    row = pl.BlockSpec((rows, LANES), lambda i: (i, 0))
    vec = pl.BlockSpec((1, LANES), lambda i: (0, 0))
    return pl.pallas_call(
        _ssd_prep_kernel,
        grid=(t // rows,),
        in_specs=[row, vec, vec],
        out_specs=[row, row, pl.BlockSpec((rows // c, 2 * SSD_GROUPS, 4, c), lambda i: (i, 0, 0, 0))],
        out_shape=[jax.ShapeDtypeStruct((t, LANES), F32), jax.ShapeDtypeStruct((t, LANES), F32),
                   jax.ShapeDtypeStruct((t // c, 2 * SSD_GROUPS, 4, c), F32)],
        compiler_params=_cparams(("parallel",)),
        name="ssd_prep",
    )(dtr, bias, aneg)


def _ssd_kernel(xf_ref, bf_ref, cf_ref, dtf_ref, acf_ref, atf_ref, e5f_ref,
                xb_ref, bb_ref, cb_ref, dtb_ref, acb_ref, atb_ref, e5b_ref,
                yf_ref, yb_ref, st_ref, *, nchunk):
    @pl.when(pl.program_id(2) == 0)
    def _():
        st_ref[...] = jnp.zeros_like(st_ref)

    c = SSD_CHUNK
    masks = (_tri(c), jnp.logical_not(_tri(c)))
    lane2 = lax.broadcasted_iota(jnp.int32, (c, 4 * SSD_P), 1)
    head_masks = [(lane2 >= hh * SSD_P) & (lane2 < (hh + 1) * SSD_P) for hh in range(4)]

    def narrow(per_head, r):
        lo = lax.broadcasted_iota(jnp.int32, (r, LANES), 1) < SSD_P
        return jnp.concatenate([jnp.where(lo, per_head[0], per_head[1]),
                                jnp.where(lo, per_head[2], per_head[3])], axis=1)

    lane1 = lax.broadcasted_iota(jnp.int32, (c, LANES), 1)
    group = pl.program_id(1)

    def chunk(dirn, ci, x_ref, b_ref, c_ref, dt_ref, ac_ref, at_ref, e5_ref, y_ref):
        rs = slice(ci * c, (ci + 1) * c)
        dt5 = _dot(dt_ref[rs, :].astype(BF16), e5_ref[...])
        ac = ac_ref[rs, :]
        first = SSD_HEADS * dirn + 4 * group
        cols = [jnp.sum(jnp.where(lane1 == first + hh, ac, 0.0), axis=1, keepdims=True)
                for hh in range(4)]
        acr = at_ref[ci]
        tots = [col[c - 1:c] if dirn == 0 else col[0:1] for col in cols]
        dt2 = narrow([dt5[:, hh * LANES:(hh + 1) * LANES] for hh in range(4)], c)
        ac2 = narrow(cols, c)
        tot2 = narrow(tots, 1)
        x = x_ref[rs, :].astype(F32)
        bm = b_ref[rs, :]
        cmat = c_ref[rs, :]
        xdt = x * dt2
        cbm = jnp.where(masks[dirn], _dot_nt(cmat, bm), 0.0)
        st = st_ref[dirn]
        ms, xs = [], []
        for hh in range(4):
            col = cols[hh]
            row = acr[hh:hh + 1, :]
            ms.append((cbm * jnp.exp(jnp.minimum(col - row, 0.0))).astype(BF16))
            xs.append(jnp.where(head_masks[hh], xdt, 0.0).astype(BF16))
        y = (_dot(cmat, st.astype(BF16)) * jnp.exp(ac2)
             + _dot(jnp.concatenate(ms, axis=1), jnp.concatenate(xs, axis=0)))
        y_ref[rs, :] = y.astype(y_ref.dtype)
        xw = (xdt * jnp.exp(tot2 - ac2)).astype(BF16)
        st_ref[dirn] = st * jnp.exp(tot2) + _dot_tn(bm, xw)

    for cc in range(nchunk):
        chunk(0, cc, xf_ref, bf_ref, cf_ref, dtf_ref, acf_ref, atf_ref, e5f_ref, yf_ref)
        chunk(1, nchunk - 1 - cc, xb_ref, bb_ref, cb_ref, dtb_ref, acb_ref, atb_ref, e5b_ref, yb_ref)


def ssd_core(xbc, dt, ac, act, bsz, seq, rows=512):
    t = bsz * seq
    nblk = seq // rows
    c = SSD_CHUNK
    hp = 4 * SSD_P
    src = 4 * jnp.arange(2 * SSD_GROUPS)[:, None] + jnp.arange(4)[None, :]
    onehot = (jnp.arange(LANES)[None, None, :] == src[..., None]).astype(BF16)
    e5 = jnp.repeat(onehot, LANES, axis=1).transpose(0, 2, 1)

    def side(bwd):
        def rb(b, i):
            return b * nblk + (nblk - 1 - i if bwd else i)
        d = int(bwd)
        return [pl.BlockSpec((rows, hp), lambda b, g, i: (rb(b, i), g)),
                pl.BlockSpec((rows, SSD_N), lambda b, g, i: (rb(b, i), 16 + g)),
                pl.BlockSpec((rows, SSD_N), lambda b, g, i: (rb(b, i), 24 + g)),
                pl.BlockSpec((rows, LANES), lambda b, g, i: (rb(b, i), 0)),
                pl.BlockSpec((rows, LANES), lambda b, g, i: (rb(b, i), 0)),
                pl.BlockSpec((rows // c, None, 4, c), lambda b, g, i: (rb(b, i), d * SSD_GROUPS + g, 0, 0)),
                pl.BlockSpec((None, LANES, 4 * LANES), lambda b, g, i: (d * SSD_GROUPS + g, 0, 0))]

    out_f = pl.BlockSpec((rows, hp), lambda b, g, i: (b * nblk + i, g))
    out_b = pl.BlockSpec((rows, hp), lambda b, g, i: (b * nblk + nblk - 1 - i, g))
    oshape = jax.ShapeDtypeStruct((t, SSD_INNER), BF16)
    kern = functools.partial(_ssd_kernel, nchunk=rows // c)
    args = (xbc, xbc, xbc, dt, ac, act, e5)
    return pl.pallas_call(
        kern,
        grid=(bsz, SSD_GROUPS, nblk),
        in_specs=side(False) + side(True),
        out_specs=[out_f, out_b],
        out_shape=[oshape, oshape],
        scratch_shapes=[pltpu.VMEM((2, SSD_N, hp), F32)],
        compiler_params=_cparams(("parallel", "parallel", "arbitrary")),
        name="ssd_core",
    )(*args, *args)


def _ssd_out_kernel(yf_ref, yb_ref, x_ref, z_ref, dsk_ref, g_ref, w_ref, h_ref, o_ref):
    y = yf_ref[...].astype(F32) + yb_ref[...].astype(F32) + x_ref[...].astype(F32) * dsk_ref[...]
    y = y * _silu(z_ref[...].astype(F32))
    gw = SSD_INNER // SSD_GROUPS
    parts = []
    for gg in range(SSD_GROUPS):
        yg = y[:, gg * gw:(gg + 1) * gw]
        ms = jnp.mean(yg * yg, axis=-1, keepdims=True)
        parts.append(yg * lax.rsqrt(ms + EPS))
    a = (jnp.concatenate(parts, axis=1) * g_ref[...]).astype(BF16)
    o_ref[...] = h_ref[...] + _dot(a, w_ref[...])


def ssd_out(y_f, y_b, xbc, zx, d_skip, g_norm, w_out, h, tm=512):
    t, n = h.shape
    wide = pl.BlockSpec((tm, SSD_INNER), lambda i: (i, 0))
    vec = pl.BlockSpec((1, SSD_INNER), lambda i: (0, 0))
    return pl.pallas_call(
        _ssd_out_kernel,
        grid=(t // tm,),
        in_specs=[wide, wide, wide, wide, vec, vec,
                  pl.BlockSpec((SSD_INNER, n), lambda i: (0, 0)),
                  pl.BlockSpec((tm, n), lambda i: (i, 0))],
        out_specs=pl.BlockSpec((tm, n), lambda i: (i, 0)),
        out_shape=jax.ShapeDtypeStruct((t, n), F32),
        compiler_params=_cparams(("parallel",)),
        name="ssd_out",
    )(y_f, y_b, xbc, zx, jnp.repeat(d_skip, SSD_P).reshape(1, SSD_INNER),
      g_norm.reshape(1, SSD_INNER), w_out, h)


def ssd_layer(h, norm_g, w_in, conv_w, conv_b, dt_bias_f, a_log_f, dt_bias_b, a_log_b, d_skip,
              g_norm, w_out, bsz, seq):
    n_main = SSD_INNER + SSD_CONV_DIM
    wdt = jnp.pad(w_in[:, n_main:], ((0, 0), (0, LANES - 2 * SSD_HEADS))).astype(BF16)
    zx, dtr = norm_matmul(h, norm_g, w_in[:, :n_main].astype(BF16), wdt)
    xbc = ssd_conv(zx, conv_w, conv_b, bsz, seq)
    dt_bias = jnp.concatenate([dt_bias_f, dt_bias_b])
    a_neg = -jnp.exp(jnp.concatenate([a_log_f, a_log_b]))
    dt, ac, act = ssd_prep(dtr, dt_bias, a_neg)
    y_f, y_b = ssd_core(xbc, dt, ac, act, bsz, seq)
    return ssd_out(y_f, y_b, xbc, zx, d_skip, g_norm, w_out.astype(BF16), h), None


def _dil_kernel(q_ref, kp_ref, kc_ref, kn_ref, vp_ref, vc_ref, vn_ref, o_ref, lse_ref, *, bounded):
    ai = pl.program_id(2)
    na = pl.num_programs(2)
    tq = q_ref.shape[0]
    side = DIL_SIDE
    kcat = jnp.concatenate([kp_ref[...], kc_ref[...], kn_ref[...]], axis=0)
    vcat = jnp.concatenate([vp_ref[...], vc_ref[...], vn_ref[...]], axis=0)
    nk = tq + 2 * side
    ii = lax.broadcasted_iota(jnp.int32, (4 * tq, nk), 0) & (tq - 1)
    jj = lax.broadcasted_iota(jnp.int32, (4 * tq, nk), 1)
    kpos = ai * tq - side + jj
    valid = (jj >= ii) & (jj <= ii + 2 * side) & (kpos >= 0) & (kpos < na * tq)
    lane = lax.broadcasted_iota(jnp.int32, (tq, ROPE_GROUP), 1)
    for g4 in range(DIL_HEADS // 4):
        gsl = slice(g4 * ROPE_GROUP, (g4 + 1) * ROPE_GROUP)
        q4 = q_ref[:, gsl]
        zq = jnp.zeros_like(q4)
        qst = jnp.concatenate([jnp.where(_head_lane_mask(q4.shape, u), q4, zq) for u in range(4)], axis=0)
        s = jnp.where(valid, _dot_nt(qst, kcat[:, gsl]), NEG_BIG)
        if bounded:
            p = jnp.exp2(s)
        else:
            m = jnp.max(s, axis=1, keepdims=True)
            p = jnp.exp2(s - m)
        den = jnp.sum(p, axis=1, keepdims=True)
        r = _dot(p.astype(BF16), vcat[:, gsl])
        lse = jnp.log(den) if bounded else m * (1.0 / LOG2E) + jnp.log(den)
        out = jnp.zeros((tq, ROPE_GROUP), F32)
        lse4 = jnp.zeros((tq, ROPE_GROUP), F32)
        for u in range(4):
            rows = slice(u * tq, (u + 1) * tq)
            mine = (lane >= u * DIL_HD) & (lane < (u + 1) * DIL_HD)
            out = jnp.where(mine, r[rows] / den[rows], out)
            lse4 = jnp.where(mine, lse[rows], lse4)
        o_ref[:, gsl] = out.astype(o_ref.dtype)
        lse_ref[:, gsl] = lse4


def dil_core(pg, dilation, score_bound, bsz, seq, tq=128):
    a = seq // dilation
    na = a // tq
    side = DIL_SIDE
    nst = seq // DIL_SUPER
    per = DIL_SUPER // dilation // tq
    per_h = DIL_SUPER // dilation // side
    nhb = a // side

    def cur(col):
        return pl.BlockSpec((None, None, tq, DIL_WIDTH),
                            lambda b, r, i: (b * nst + i // per, r, i % per, col))

    def halo(col, which):
        def imap(b, r, i):
            hb = (jnp.maximum(2 * i - 1, 0) if which == 0 else jnp.minimum(2 * i + 2, nhb - 1))
            return (b * nst + hb // per_h, r, hb % per_h, col)
        return pl.BlockSpec((None, None, side, DIL_WIDTH), imap)

    out_spec = pl.BlockSpec((None, None, tq, DIL_WIDTH), lambda b, r, i: (b * nst + i // per, r, i % per, 0))
    oshape = (bsz * nst, dilation, DIL_SUPER // dilation, DIL_WIDTH)

    def call(bounded):
        return lambda x: pl.pallas_call(
            functools.partial(_dil_kernel, bounded=bounded),
            grid=(bsz, dilation, na),
            in_specs=[cur(0), halo(1, 0), cur(1), halo(1, 1), halo(2, 0), cur(2), halo(2, 1)],
            out_specs=[out_spec, out_spec],
            out_shape=[jax.ShapeDtypeStruct(oshape, BF16), jax.ShapeDtypeStruct(oshape, F32)],
            compiler_params=_cparams(("parallel", "parallel", "parallel")),
            name=f"dil_core_{dilation}" + ("_bounded" if bounded else "_online"),
        )(x, x, x, x, x, x, x)

    return lax.cond(score_bound <= SAFE_LOG2, call(True), call(False), pg)


def _dil_out_kernel(o0_ref, o1_ref, o2_ref, l0_ref, l1_ref, l2_ref, w_ref, h_ref, o_ref,
                    so_ref, sl_ref):
    tm = h_ref.shape[0]
    nch = DIL_WIDTH // LANES
    for gi, (og_ref, lg_ref) in enumerate(((o1_ref, l1_ref), (o2_ref, l2_ref))):
        d = DIL_PAIRS[gi + 1][1]
        n = tm // d
        for r in range(d):
            ov = og_ref[r].astype(F32)
            lv = lg_ref[r]
            for c in range(nch):
                so_ref[gi, c, pl.ds(r, n, stride=d), :] = ov[:, c * LANES:(c + 1) * LANES]
                sl_ref[gi, c, pl.ds(r, n, stride=d), :] = lv[:, c * LANES:(c + 1) * LANES]
    parts = []
    for c in range(nch):
        sl = slice(c * LANES, (c + 1) * LANES)
        l0 = l0_ref[0, :, sl]
        l1 = sl_ref[0, c]
        l2 = sl_ref[1, c]
        mx = jnp.maximum(jnp.maximum(l0, l1), l2)
        e0 = jnp.exp(l0 - mx)
        e1 = jnp.exp(l1 - mx)
        e2 = jnp.exp(l2 - mx)
        num = e0 * o0_ref[0, :, sl].astype(F32) + e1 * so_ref[0, c] + e2 * so_ref[1, c]
        parts.append((num / (e0 + e1 + e2)).astype(BF16))
    o_ref[...] = h_ref[...] + _dot(jnp.concatenate(parts, axis=1), w_ref[...])


def dil_out(os, lses, w_out, h, tm=512):
    t, n = h.shape
    per = DIL_SUPER // tm

    def grp(d):
        return pl.BlockSpec((None, d, tm // d, DIL_WIDTH), lambda i: (i // per, 0, i % per, 0))

    specs = [grp(d) for _, d in DIL_PAIRS]
    row = pl.BlockSpec((tm, n), lambda i: (i, 0))
    return pl.pallas_call(
        _dil_out_kernel,
        grid=(t // tm,),
        in_specs=specs + specs + [pl.BlockSpec((DIL_WIDTH, n), lambda i: (0, 0)), row],
        out_specs=row,
        out_shape=jax.ShapeDtypeStruct((t, n), F32),
        scratch_shapes=[pltpu.VMEM((2, DIL_WIDTH // LANES, tm, LANES), F32),
                        pltpu.VMEM((2, DIL_WIDTH // LANES, tm, LANES), F32)],
        compiler_params=_cparams(("parallel",)),
        name="dil_out",
    )(*os, *lses, w_out, h)


def dil_layer(h, norm_g, w_in, g_q, g_k, w_out, bsz, seq):
    qscale = (DIL_HD ** -0.5) * LOG2E
    w4 = w_in.reshape(D_MODEL, 3, len(DIL_PAIRS), DIL_WIDTH)
    os, lses = [], []
    tables = _rope_tables(seq)
    for gi, (_, dilation) in enumerate(DIL_PAIRS):
        wg = w4[:, :, gi, :].reshape(D_MODEL, 3 * DIL_WIDTH).astype(BF16)
        rope_gain = jnp.concatenate([jnp.tile(g_q[gi], DIL_HEADS) * qscale, jnp.tile(g_k[gi], DIL_HEADS)])
        pg = norm_matmul_rope(h, norm_g, wg, rope_gain, tables, seq, dil=dilation)
        o, lse = dil_core(pg, dilation, _score_bound(g_q[gi], g_k[gi], qscale), bsz, seq)
        os.append(o)
        lses.append(lse)
    return dil_out(os, lses, w_out.astype(BF16), h), None


def kernel(x, p, norm_mix, norm_ffn, ffn_w_in, ffn_w_out, ple_norm, ple_w_gate, ple_w_proj, gla_w_in, gla_w_gate_f, gla_b_gate_f, gla_w_gate_b, gla_b_gate_b, gla_g_out, gla_w_out, diff_w_in, diff_g_q, diff_g_k, diff_lam_q1, diff_lam_k1, diff_lam_q2, diff_lam_k2, diff_g_sub, diff_w_out, ssd_w_in, ssd_conv_w, ssd_conv_b, ssd_dt_bias_f, ssd_a_log_f, ssd_dt_bias_b, ssd_a_log_b, ssd_d, ssd_g_norm, ssd_w_out, dil_w_in, dil_g_q, dil_g_k, dil_w_out):
    bsz, seq, dm = x.shape
    t = bsz * seq
    h = x.reshape(t, dm)
    for i in range(DEPTH):
        kind = i % 4
        j = i // 4
        if kind == 0:
            h, mixer = gla_layer(h, norm_mix[i], gla_w_in[j], gla_w_gate_f[j], gla_b_gate_f[j],
                                 gla_w_gate_b[j], gla_b_gate_b[j], gla_g_out[j], gla_w_out[j], bsz, seq)
        elif kind == 1:
            h, mixer = diff_layer(h, norm_mix[i], i, diff_w_in[j], diff_g_q[j], diff_g_k[j],
                                  diff_lam_q1[j], diff_lam_k1[j], diff_lam_q2[j], diff_lam_k2[j],
                                  diff_g_sub[j], diff_w_out[j], bsz, seq)
        elif kind == 2:
            h, mixer = ssd_layer(h, norm_mix[i], ssd_w_in[j], ssd_conv_w[j], ssd_conv_b[j],
                                 ssd_dt_bias_f[j], ssd_a_log_f[j], ssd_dt_bias_b[j], ssd_a_log_b[j],
                                 ssd_d[j], ssd_g_norm[j], ssd_w_out[j], bsz, seq)
        else:
            h, mixer = dil_layer(h, norm_mix[i], dil_w_in[j], dil_g_q[j], dil_g_k[j], dil_w_out[j],
                                 bsz, seq)
        ffn_ple_params = (norm_ffn[i], ffn_w_in[i].astype(BF16), ffn_w_out[i].astype(BF16), ple_norm[i],
                          ple_w_gate[i].astype(BF16), p[i].reshape(t, PLE_DIM), ple_w_proj[i].astype(BF16))
        h = ffn_ple_residual(h, ffn_ple_params, mixer)
    return h.reshape(bsz, seq, dm)
```

```python
import functools
import math

import jax
import jax.numpy as jnp
from jax import lax
from jax.experimental import pallas as pl
from jax.experimental.pallas import tpu as pltpu

F32 = jnp.float32
BF16 = jnp.bfloat16

D_MODEL = 1024
DEPTH = 4
PLE_DIM = 256
ROPE_THETA = 10000.0
EPS = 1e-6
D_FF = 2816
LOG2E = math.log2(math.e)
NEG_BIG = -1e30

GLA_HEADS = 4
GLA_DK = 512
GLA_DV = 1024
GLA_HK = 128
GLA_HV = 256
GLA_RANK = 16
GLA_TAU = 16.0
GLA_CHUNK = 64

DIFF_HEADS = 8
DIFF_HD = 64

SSD_INNER = 2048
SSD_P = 64
SSD_HEADS = 32
SSD_GROUPS = 8
SSD_N = 128
SSD_CONV = 5
SSD_CHUNK = 128
SSD_CONV_DIM = 4096

DIL_PAIRS = ((128, 1), (512, 4), (2048, 16))
DIL_HEADS = 16
DIL_HD = 64
DIL_WIDTH = 1024
DIL_SIDE = 64
DIL_SUPER = 2048

LANES = 128
ROPE_GROUP = 256
SAFE_LOG2 = 60.0


def _cparams(sem, vmem_mb=None):
    kw = dict(dimension_semantics=sem)
    if vmem_mb is not None:
        kw["vmem_limit_bytes"] = vmem_mb << 20
    return pltpu.CompilerParams(**kw)


def _dot(a, b):
    return jnp.dot(a, b, preferred_element_type=F32)


def _dot_nt(a, b):
    return lax.dot_general(a, b, (((1,), (1,)), ((), ())), preferred_element_type=F32)


def _dot_tn(a, b):
    return lax.dot_general(a, b, (((0,), (0,)), ((), ())), preferred_element_type=F32)


def _split(x):
    hi = x.astype(BF16)
    return hi, (x - hi.astype(F32)).astype(BF16)


def _dot_split(a_bf16, x):
    hi, lo = _split(x)
    return _dot(a_bf16, hi) + _dot(a_bf16, lo)


def _rms_rows(x, g):
    ms = jnp.mean(x * x, axis=-1, keepdims=True)
    return x * lax.rsqrt(ms + EPS) * g


def _silu(x):
    return x * jax.nn.sigmoid(x)


def _softplus(x):
    return jnp.maximum(x, 0.0) + jnp.log(1.0 + jnp.exp(-jnp.abs(x)))


def _tri(c, upper=False):
    ii = lax.broadcasted_iota(jnp.int32, (c, c), 0)
    jj = lax.broadcasted_iota(jnp.int32, (c, c), 1)
    return (jj >= ii) if upper else (jj <= ii)


def _resident(shape):
    return pl.BlockSpec(shape, lambda i: (0,) * len(shape), pipeline_mode=pl.Buffered(1))


PROJ_COLS = 1024


def _nm_kernel(x_ref, g_ref, w_ref, ws_ref, o_ref, os_ref):
    xn = _rms_rows(x_ref[...], g_ref[...]).astype(BF16)
    for c in range(w_ref.shape[1] // PROJ_COLS):
        sl = slice(c * PROJ_COLS, (c + 1) * PROJ_COLS)
        o_ref[:, sl] = _dot(xn, w_ref[:, sl]).astype(o_ref.dtype)
    os_ref[...] = _dot(xn, ws_ref[...])


def norm_matmul(x, g, w, w_side, tm=512):
    t, k = x.shape
    n = w.shape[1]
    return pl.pallas_call(
        _nm_kernel,
        grid=(t // tm,),
        in_specs=[pl.BlockSpec((tm, k), lambda i: (i, 0)),
                  _resident((1, k)), _resident((k, n)), _resident((k, LANES))],
        out_specs=[pl.BlockSpec((tm, n), lambda i: (i, 0)), pl.BlockSpec((tm, LANES), lambda i: (i, 0))],
        out_shape=[jax.ShapeDtypeStruct((t, n), BF16), jax.ShapeDtypeStruct((t, LANES), F32)],
        compiler_params=_cparams(("parallel",)),
        name="norm_matmul",
    )(x, g.reshape(1, k), w, w_side)


def _rope_tables(seq):
    half = DIFF_HD // 2
    inv = ROPE_THETA ** (-jnp.arange(half, dtype=F32) * 2.0 / DIFF_HD)
    ang = jnp.arange(seq, dtype=jnp.int32).astype(F32)[:, None] * inv[None, :]
    return jnp.tile(jnp.cos(ang), (1, 4)), jnp.tile(jnp.sin(ang), (1, 4))


def _rope_col_perm(n_cols):
    half = DIFF_HD // 2
    g = jnp.arange(n_cols // ROPE_GROUP)[:, None, None, None] * ROPE_GROUP
    part = jnp.arange(2)[None, :, None, None] * half
    head = jnp.arange(4)[None, None, :, None] * DIFF_HD
    d = jnp.arange(half)[None, None, None, :]
    return (g + head + part + d).reshape(-1)


def _seg32_ones():
    r = jnp.arange(LANES)
    return (r[:, None] // 32 == r[None, :] // 32).astype(BF16)


def _head_lane_mask(shape, u):
    lane = lax.broadcasted_iota(jnp.int32, shape, len(shape) - 1)
    return (lane % LANES) // (DIFF_HD // 2) == u


def _nm_rope_kernel(x_ref, g_ref, w_ref, hg_ref, cos_ref, sin_ref, bd_ref, o_ref, xn_ref, *xs_refs,
                    n_rope_cols, dil):
    tm = x_ref.shape[0]
    n = tm // dil
    if dil == 1:
        xn_ref[...] = _rms_rows(x_ref[...], g_ref[...]).astype(BF16)
    else:
        xs_ref = xs_refs[0]
        nch = x_ref.shape[1] // LANES
        for c in range(nch):
            xs_ref[c] = x_ref[:, c * LANES:(c + 1) * LANES]
        for r in range(dil):
            xr = jnp.concatenate([xs_ref[c, pl.ds(r, n, stride=dil), :] for c in range(nch)], axis=1)
            xn_ref[r * n:(r + 1) * n, :] = _rms_rows(xr, g_ref[...]).astype(BF16)

    def put(sl, val):
        if len(o_ref.shape) == 2:
            o_ref[:, sl] = val
        else:
            for r in range(dil):
                o_ref[r, :, sl] = val[r * n:(r + 1) * n]

    cos = cos_ref[...]
    sin = sin_ref[...]
    bd = bd_ref[...]
    xn = xn_ref[...]
    wide = 2 * ROPE_GROUP
    for cw in range(n_rope_cols // wide):
        acc2 = _dot(xn, w_ref[:, cw * wide:(cw + 1) * wide])
        for c2 in range(2):
            c = 2 * cw + c2
            sl = slice(c * ROPE_GROUP, (c + 1) * ROPE_GROUP)
            a = acc2[:, c2 * ROPE_GROUP:c2 * ROPE_GROUP + LANES]
            b = acc2[:, c2 * ROPE_GROUP + LANES:(c2 + 1) * ROPE_GROUP]
            ss = _dot((a * a + b * b).astype(BF16), bd)
            inv = lax.rsqrt(ss * (1.0 / DIFF_HD) + EPS)
            a = a * inv * hg_ref[:, c * ROPE_GROUP:c * ROPE_GROUP + LANES]
            b = b * inv * hg_ref[:, c * ROPE_GROUP + LANES:(c + 1) * ROPE_GROUP]
            put(sl, jnp.concatenate([a * cos - b * sin, b * cos + a * sin], axis=1).astype(o_ref.dtype))
    for c in range(n_rope_cols // PROJ_COLS, w_ref.shape[1] // PROJ_COLS):
        sl = slice(c * PROJ_COLS, (c + 1) * PROJ_COLS)
        put(sl, _dot(xn, w_ref[:, sl]).astype(o_ref.dtype))


def norm_matmul_rope(x, g, w_full, cols, rope_gain, tables, seq, dil=None, tm=512):
    t, k = x.shape
    n = cols.shape[0]
    n_rope_cols = rope_gain.shape[0]
    perm = _rope_col_perm(n_rope_cols)
    w = w_full[:, jnp.concatenate([cols[:n_rope_cols][perm], cols[n_rope_cols:]])].astype(BF16)
    rope_gain = rope_gain[perm]
    cos, sin = tables
    nsb = seq // tm
    d = 1 if dil is None else dil
    if d > 1:
        cos, sin = (tb.reshape(nsb, tm // d, d, LANES).transpose(0, 2, 1, 3).reshape(seq, LANES)
                    for tb in (cos, sin))
    kern = functools.partial(_nm_rope_kernel, n_rope_cols=n_rope_cols, dil=d)
    if dil is None:
        out_spec = pl.BlockSpec((tm, n), lambda i: (i, 0))
        out_shape = jax.ShapeDtypeStruct((t, n), BF16)
    else:
        per = DIL_SUPER // tm
        out_spec = pl.BlockSpec((None, d, tm // d, n), lambda i: (i // per, 0, i % per, 0))
        out_shape = jax.ShapeDtypeStruct((t // DIL_SUPER, d, DIL_SUPER // d, n), BF16)
    scratch = [pltpu.VMEM((tm, k), BF16)]
    if d > 1:
        scratch.append(pltpu.VMEM((k // LANES, tm, LANES), F32))
    return pl.pallas_call(
        kern,
        grid=(t // tm,),
        in_specs=[pl.BlockSpec((tm, k), lambda i: (i, 0)),
                  _resident((1, k)), _resident((k, n)), _resident((1, n_rope_cols)),
                  pl.BlockSpec((tm, LANES), lambda i: (i % nsb, 0)),
                  pl.BlockSpec((tm, LANES), lambda i: (i % nsb, 0)),
                  _resident((LANES, LANES))],
        out_specs=out_spec,
        out_shape=out_shape,
        scratch_shapes=scratch,
        compiler_params=_cparams(("parallel",)),
        name="norm_matmul_rope",
    )(x, g.reshape(1, k), w, rope_gain.reshape(1, n_rope_cols), cos, sin, _seg32_ones())


def _score_bound(g_q, g_k, qscale):
    return 1.05 * DIFF_HD * qscale * jnp.max(jnp.abs(g_q)) * jnp.max(jnp.abs(g_k))


FFN_SPLIT = 1536


def _ffn_ple_kernel(*refs, mixer_act):
    x_ref, gf_ref, win_ref, wout_ref, gp_ref, wg_ref, p_ref, wp_ref, o_ref = refs[-9:]
    x = x_ref[...]
    if mixer_act is not None:
        x = x + _dot(mixer_act(*refs[:-10]), refs[-10][...])
    xn = _rms_rows(x, gf_ref[...]).astype(BF16)
    h2 = x
    for lo, hi in ((0, FFN_SPLIT), (FFN_SPLIT, D_FF)):
        gate = _dot(xn, win_ref[:, lo:hi])
        up = _dot(xn, win_ref[:, D_FF + lo:D_FF + hi])
        a = (_silu(gate) * up).astype(BF16)
        h2 = h2 + _dot(a, wout_ref[lo:hi, :])
    hn = _rms_rows(h2, gp_ref[...]).astype(BF16)
    gate2 = jax.nn.sigmoid(_dot(hn, wg_ref[...]))
    o_ref[...] = h2 + gate2 * _dot(p_ref[...].astype(BF16), wp_ref[...])


FFN_ROWS = 512


def ffn_ple_residual(h, ffn_ple_params, mixer=None):
    g_ffn, w_in, w_out, g_ple, w_gate, p, w_proj = ffn_ple_params
    t, k = h.shape
    tm = FFN_ROWS
    act, m_ops, m_specs = None, (), []
    if mixer is not None:
        act, m_ops, m_specs, w_mix = mixer
        m_ops = tuple(m_ops) + (w_mix,)
        m_specs = list(m_specs) + [_resident(w_mix.shape)]
    return pl.pallas_call(
        functools.partial(_ffn_ple_kernel, mixer_act=act),
        grid=(t // tm,),
        in_specs=m_specs + [pl.BlockSpec((tm, k), lambda i: (i, 0)),
                            _resident((1, k)),
                            _resident((k, 2 * D_FF)),
                            _resident((D_FF, k)),
                            _resident((1, k)),
                            _resident((k, k)),
                            pl.BlockSpec((tm, PLE_DIM), lambda i: (i, 0)),
                            _resident((PLE_DIM, k))],
        out_specs=pl.BlockSpec((tm, k), lambda i: (i, 0)),
        out_shape=jax.ShapeDtypeStruct((t, k), F32),
        compiler_params=_cparams(("parallel",)),
        name="ffn_ple",
    )(*m_ops, h, g_ffn.reshape(1, k), w_in, w_out, g_ple.reshape(1, k), w_gate, p, w_proj)


def _gla_kernel(qf_ref, kf_ref, vf_ref, zf_ref, qb_ref, kb_ref, vb_ref, zb_ref, wg_ref, bg_ref,
                of_ref, ob_ref, st_ref, la_ref, *, nchunk):
    @pl.when(pl.program_id(1) == 0)
    def _():
        st_ref[...] = jnp.zeros_like(st_ref)

    c = GLA_CHUNK
    for dirn, z_ref in enumerate((zf_ref, zb_ref)):
        x = _dot(z_ref[...].astype(BF16), wg_ref[dirn]) + bg_ref[dirn]
        la_ref[dirn] = (jnp.minimum(x, 0.0) - jnp.log(1.0 + jnp.exp(-jnp.abs(x)))) * (1.0 / GLA_TAU)
    cms = (_tri(c).astype(BF16), _tri(c, upper=True).astype(BF16))
    mks = (_tri(c), jnp.logical_not(_tri(c)))

    def chunk(dirn, hh, ci, q_ref, k_ref, v_ref, o_ref):
        rs = slice(ci * c, (ci + 1) * c)
        ks = slice(hh * GLA_HK, (hh + 1) * GLA_HK)
        vs = slice(hh * GLA_HV, (hh + 1) * GLA_HV)
        la = la_ref[dirn, rs, ks]
        b = _dot_split(cms[dirn], la)
        tot = jnp.sum(la, axis=0, keepdims=True)
        qc = q_ref[rs, ks].astype(F32)
        kc = k_ref[rs, ks].astype(F32)
        vc = v_ref[rs, vs]
        q_in = (qc * jnp.exp(b) * (GLA_HK ** -0.5)).astype(BF16)
        k_out = (kc * jnp.exp(-b)).astype(BF16)
        k_dec = (kc * jnp.exp(tot - b)).astype(BF16)
        sc = jnp.where(mks[dirn], _dot_nt(q_in, k_out), 0.0).astype(BF16)
        st = st_ref[dirn, hh]
        o = _dot(sc, vc) + _dot_nt(q_in, st.astype(BF16))
        o_ref[rs, vs] = o.astype(o_ref.dtype)
        st_ref[dirn, hh] = st * jnp.exp(tot) + _dot_tn(vc, k_dec)

    for cc in range(nchunk):
        for hh in range(GLA_HEADS):
            chunk(0, hh, cc, qf_ref, kf_ref, vf_ref, of_ref)
            chunk(1, hh, nchunk - 1 - cc, qb_ref, kb_ref, vb_ref, ob_ref)


def gla_core(proj, z, wg, bg, bsz, seq, rows=512):
    t = bsz * seq
    nblk = seq // rows

    def side(bwd):
        def rb(b, i):
            return b * nblk + (nblk - 1 - i if bwd else i)
        return [pl.BlockSpec((rows, GLA_DK), lambda b, i: (rb(b, i), 0)),
                pl.BlockSpec((rows, GLA_DK), lambda b, i: (rb(b, i), 1)),
                pl.BlockSpec((rows, GLA_DV), lambda b, i: (rb(b, i), 1)),
                pl.BlockSpec((rows, LANES), lambda b, i: (rb(b, i), 0))]

    out_f = pl.BlockSpec((rows, GLA_DV), lambda b, i: (b * nblk + i, 0))
    out_b = pl.BlockSpec((rows, GLA_DV), lambda b, i: (b * nblk + nblk - 1 - i, 0))
    oshape = jax.ShapeDtypeStruct((t, GLA_DV), BF16)
    kern = functools.partial(_gla_kernel, nchunk=rows // GLA_CHUNK)
    return pl.pallas_call(
        kern,
        grid=(bsz, nblk),
        in_specs=side(False) + side(True) + [pl.BlockSpec((2, LANES, GLA_DK), lambda b, i: (0, 0, 0)),
                                             pl.BlockSpec((2, 1, GLA_DK), lambda b, i: (0, 0, 0))],
        out_specs=[out_f, out_b],
        out_shape=[oshape, oshape],
        scratch_shapes=[pltpu.VMEM((2, GLA_HEADS, GLA_HV, GLA_HK), F32),
                        pltpu.VMEM((2, rows, GLA_DK), F32)],
        compiler_params=_cparams(("parallel", "arbitrary")),
        name="gla_core",
    )(proj, proj, proj, z, proj, proj, proj, z, wg, bg)


def _gla_act(of_ref, ob_ref, r_ref, g_ref):
    o = of_ref[...].astype(F32) + ob_ref[...].astype(F32)
    parts = []
    for hh in range(GLA_HEADS):
        oh = o[:, hh * GLA_HV:(hh + 1) * GLA_HV]
        ms = jnp.mean(oh * oh, axis=-1, keepdims=True)
        parts.append(oh * lax.rsqrt(ms + EPS))
    on = jnp.concatenate(parts, axis=1) * g_ref[...]
    return (on * _silu(r_ref[...].astype(F32))).astype(BF16)


def gla_mixer_out(o_f, o_b, proj, g_out, w_out):
    row = pl.BlockSpec((FFN_ROWS, GLA_DV), lambda i: (i, 0))
    specs = [row, row, pl.BlockSpec((FFN_ROWS, GLA_DV), lambda i: (i, 2)), _resident((1, GLA_DV))]
    return _gla_act, (o_f, o_b, proj, jnp.tile(g_out, GLA_HEADS).reshape(1, GLA_DV)), specs, w_out


def gla_layer(h, norm_g, w_in, w_gate_f, b_gate_f, w_gate_b, b_gate_b, g_out, w_out, bsz, seq):
    n_main = 2 * GLA_DK + 2 * GLA_DV
    wz = jnp.pad(w_in[:, n_main:], ((0, 0), (0, LANES - 2 * GLA_RANK))).astype(BF16)
    proj, z = norm_matmul(h, norm_g, w_in[:, :n_main].astype(BF16), wz)
    wg = jnp.zeros((2, LANES, GLA_DK), F32)
    wg = wg.at[0, :GLA_RANK].set(w_gate_f).at[1, GLA_RANK:2 * GLA_RANK].set(w_gate_b).astype(BF16)
    bg = jnp.stack([b_gate_f, b_gate_b]).reshape(2, 1, GLA_DK)
    o_f, o_b = gla_core(proj, z, wg, bg, bsz, seq)
    return h, gla_mixer_out(o_f, o_b, proj, g_out, w_out.astype(BF16))


def _diff_kernel(q_ref, k_ref, v0_ref, v1_ref, lam_ref, gs_ref, o_ref, qz_ref, acc_ref, *m_refs,
                 tk, lambda_init, bounded):
    tq = q_ref.shape[0]
    q = q_ref[...]
    zero = jnp.zeros_like(q)
    for u in range(4):
        qz_ref[u] = jnp.where(_head_lane_mask(q.shape, u), q, zero)
    acc_ref[...] = jnp.zeros_like(acc_ref)
    if not bounded:
        m_ref = m_refs[0]
        m_ref[...] = jnp.full_like(m_ref, NEG_BIG)
    ones = jnp.ones((tk, LANES), BF16)

    def body(ki, carry):
        rows = pl.ds(pl.multiple_of(ki * tk, tk), tk)
        kb = k_ref[rows, :]
        vaug = [jnp.concatenate([v_ref[rows, :], ones], axis=1) for v_ref in (v0_ref, v1_ref)]
        for u in range(4):
            s = _dot_nt(qz_ref[u], kb)
            if bounded:
                acc_ref[u] += _dot(jnp.exp2(s).astype(BF16), vaug[u // 2])
            else:
                m_prev = m_ref[u]
                m_new = jnp.maximum(m_prev, jnp.max(s, axis=1, keepdims=True))
                alpha = jnp.exp2(m_prev - m_new)
                p = jnp.exp2(s - jnp.tile(m_new, (1, tk // LANES)))
                acc_ref[u] = jnp.tile(alpha, (1, 2)) * acc_ref[u] + _dot(p.astype(BF16), vaug[u // 2])
                m_ref[u] = m_new
        return carry

    lax.fori_loop(0, k_ref.shape[0] // tk, body, 0, unroll=4 if bounded else 1)

    lam4 = lam_ref[...]
    lam = (jnp.exp(jnp.sum(lam4[0:1] * lam4[1:2], axis=1, keepdims=True))
           - jnp.exp(jnp.sum(lam4[2:3] * lam4[3:4], axis=1, keepdims=True)) + lambda_init)
    for hh in range(2):
        a1 = acc_ref[2 * hh]
        a2 = acc_ref[2 * hh + 1]
        o = a1[:, :LANES] / a1[:, LANES:] - lam * (a2[:, :LANES] / a2[:, LANES:])
        ms = jnp.mean(o * o, axis=-1, keepdims=True)
        o_ref[:, hh * LANES:(hh + 1) * LANES] = (
            o * lax.rsqrt(ms + EPS) * gs_ref[...] * (1.0 - lambda_init)).astype(o_ref.dtype)


def diff_core(proj, lam4, g_sub, lambda_init, score_bound, bsz, seq, tq=512, tk=512):
    t = bsz * seq
    nq = seq // tq
    npair = DIFF_HEADS // 2
    in_specs = [pl.BlockSpec((tq, ROPE_GROUP), lambda b, h, i: (b * nq + i, h)),
                pl.BlockSpec((seq, ROPE_GROUP), lambda b, h, i: (b, npair + h)),
                pl.BlockSpec((seq, LANES), lambda b, h, i: (b, 16 + 2 * h)),
                pl.BlockSpec((seq, LANES), lambda b, h, i: (b, 17 + 2 * h)),
                pl.BlockSpec((4, DIFF_HD), lambda b, h, i: (0, 0)),
                pl.BlockSpec((1, LANES), lambda b, h, i: (0, 0))]
    out_spec = pl.BlockSpec((tq, ROPE_GROUP), lambda b, h, i: (b * nq + i, h))
    out_shape = jax.ShapeDtypeStruct((t, D_MODEL), BF16)
    args = (proj, proj, proj, proj, lam4, g_sub.reshape(1, LANES))

    def call(bounded):
        scratch = [pltpu.VMEM((4, tq, ROPE_GROUP), BF16), pltpu.VMEM((4, tq, 2 * LANES), F32)]
        if not bounded:
            scratch.append(pltpu.VMEM((4, tq, LANES), F32))
        return lambda *a: pl.pallas_call(
            functools.partial(_diff_kernel, tk=tk, lambda_init=lambda_init, bounded=bounded),
            grid=(bsz, npair, nq), in_specs=in_specs, out_specs=out_spec, out_shape=out_shape,
            scratch_shapes=scratch,
            compiler_params=_cparams(("parallel", "parallel", "arbitrary")),
            name="diff_core_bounded" if bounded else "diff_core_online",
        )(*a)

    return lax.cond(score_bound <= SAFE_LOG2, call(True), call(False), *args)


def diff_layer(h, norm_g, layer_idx, w_in, g_q, g_k, lam_q1, lam_k1, lam_q2, lam_k2, g_sub, w_out,
               bsz, seq):
    lambda_init = 0.8 - 0.6 * math.exp(-0.3 * layer_idx)
    qscale = (DIFF_HD ** -0.5) * LOG2E
    rope_gain = jnp.concatenate([jnp.tile(g_q.reshape(-1), DIFF_HEADS) * qscale,
                                 jnp.tile(g_k.reshape(-1), DIFF_HEADS)])
    proj = norm_matmul_rope(h, norm_g, w_in, jnp.arange(3 * D_MODEL), rope_gain, _rope_tables(seq), seq)
    lam4 = jnp.stack([lam_q1, lam_k1, lam_q2, lam_k2])
    o = diff_core(proj, lam4, g_sub, lambda_init, _score_bound(g_q, g_k, qscale), bsz, seq)
    row = pl.BlockSpec((FFN_ROWS, D_MODEL), lambda i: (i, 0))
    return h, (lambda o_ref: o_ref[...], (o,), [row], w_out.astype(BF16))


CONV_BLOCK = 128


def _conv_kernel(prev_ref, cur_ref, next_ref, sh_ref, w_ref, b_ref, o_ref, buf_ref, *, halo):
    i = pl.program_id(1)
    ts = cur_ref.shape[0]
    zero = jnp.zeros(prev_ref.shape, prev_ref.dtype)
    buf_ref[0:halo, :] = jnp.where(i == 0, zero, prev_ref[...])
    buf_ref[halo:halo + ts, :] = cur_ref[...]
    buf_ref[halo + ts:, :] = jnp.where(i == pl.num_programs(1) - 1, zero, next_ref[...])
    w = w_ref[...]
    sh = sh_ref[...]
    cb = CONV_BLOCK
    for r in range(ts // cb):
        win = buf_ref[r * cb:(r + 1) * cb + 2 * halo, :]
        sft = _dot(sh, win)
        y = b_ref[...] + w[2:3] * cur_ref[r * cb:(r + 1) * cb, :].astype(F32)
        for slot, k in enumerate((0, 1, 3, 4)):
            y = y + w[k:k + 1] * sft[slot * cb:(slot + 1) * cb]
        o_ref[r * cb:(r + 1) * cb, :] = _silu(y).astype(o_ref.dtype)


def ssd_conv(zx, conv_w, conv_b, bsz, seq, ts=512, tc=512):
    t = bsz * seq
    halo = 16
    nrb = seq // ts
    hb = ts // halo
    nhb = seq // halo
    off = SSD_INNER // tc
    kern = functools.partial(_conv_kernel, halo=halo)
    wpad = jnp.pad(conv_w, ((0, 8 - SSD_CONV), (0, 0)))
    cb = CONV_BLOCK
    shift = jnp.array([-2, -1, 1, 2])[:, None, None]
    sh = (jnp.arange(cb + 2 * halo)[None, None, :] == jnp.arange(cb)[None, :, None] + halo + shift)
    sh = sh.reshape(4 * cb, cb + 2 * halo).astype(BF16)
    return pl.pallas_call(
        kern,
        grid=(bsz, nrb, SSD_CONV_DIM // tc),
        in_specs=[pl.BlockSpec((halo, tc), lambda b, i, c: (b * nhb + jnp.maximum(i * hb - 1, 0), off + c)),
                  pl.BlockSpec((ts, tc), lambda b, i, c: (b * nrb + i, off + c)),
                  pl.BlockSpec((halo, tc), lambda b, i, c: (b * nhb + jnp.minimum((i + 1) * hb, nhb - 1), off + c)),
                  pl.BlockSpec((4 * cb, cb + 2 * halo), lambda b, i, c: (0, 0)),
                  pl.BlockSpec((8, tc), lambda b, i, c: (0, c)),
                  pl.BlockSpec((1, tc), lambda b, i, c: (0, c))],
        out_specs=pl.BlockSpec((ts, tc), lambda b, i, c: (b * nrb + i, c)),
        out_shape=jax.ShapeDtypeStruct((t, SSD_CONV_DIM), BF16),
        scratch_shapes=[pltpu.VMEM((ts + 2 * halo, tc), BF16)],
        compiler_params=_cparams(("parallel", "parallel", "parallel")),
        name="ssd_conv",
    )(zx, zx, zx, sh, wpad, conv_b.reshape(1, -1))


def _ssd_prep_kernel(dtr_ref, bias_ref, aneg_ref, dt_ref, ac_ref, at_ref):
    c = SSD_CHUNK
    lower = _tri(c).astype(BF16)
    upper = _tri(c, upper=True).astype(BF16)
    lane = lax.broadcasted_iota(jnp.int32, (c, LANES), 1)
    for cc in range(dtr_ref.shape[0] // c):
        rs = slice(cc * c, (cc + 1) * c)
        dt = _softplus(dtr_ref[rs, :] + bias_ref[...])
        adt = dt * aneg_ref[...]
        hi, lo = _split(adt)
        ac = jnp.where(lane < SSD_HEADS, _dot(lower, hi) + _dot(lower, lo), _dot(upper, hi) + _dot(upper, lo))
        dt_ref[rs, :] = dt
        ac_ref[rs, :] = ac
        act = ac.T
        for k in range(2 * SSD_GROUPS):
            at_ref[cc, k] = act[4 * k:4 * k + 4, :]


def ssd_prep(dtr, dt_bias, a_neg, rows=512):
    t = dtr.shape[0]
    c = SSD_CHUNK
    bias = jnp.pad(dt_bias, (0, LANES - 2 * SSD_HEADS)).reshape(1, LANES)
    aneg = jnp.pad(a_neg, (0, LANES - 2 * SSD_HEADS)).reshape(1, LANES)
    row = pl.BlockSpec((rows, LANES), lambda i: (i, 0))
    vec = pl.BlockSpec((1, LANES), lambda i: (0, 0))
    return pl.pallas_call(
        _ssd_prep_kernel,
        grid=(t // rows,),
        in_specs=[row, vec, vec],
        out_specs=[row, row, pl.BlockSpec((rows // c, 2 * SSD_GROUPS, 4, c), lambda i: (i, 0, 0, 0))],
        out_shape=[jax.ShapeDtypeStruct((t, LANES), F32), jax.ShapeDtypeStruct((t, LANES), F32),
                   jax.ShapeDtypeStruct((t // c, 2 * SSD_GROUPS, 4, c), F32)],
        compiler_params=_cparams(("parallel",)),
        name="ssd_prep",
    )(dtr, bias, aneg)


def _ssd_kernel(xf_ref, bf_ref, cf_ref, dtf_ref, acf_ref, atf_ref, e5f_ref,
                xb_ref, bb_ref, cb_ref, dtb_ref, acb_ref, atb_ref, e5b_ref,
                yf_ref, yb_ref, st_ref, *, nchunk):
    @pl.when(pl.program_id(2) == 0)
    def _():
        st_ref[...] = jnp.zeros_like(st_ref)

    c = SSD_CHUNK
    masks = (_tri(c), jnp.logical_not(_tri(c)))
    lane2 = lax.broadcasted_iota(jnp.int32, (c, 4 * SSD_P), 1)
    head_masks = [(lane2 >= hh * SSD_P) & (lane2 < (hh + 1) * SSD_P) for hh in range(4)]

    def narrow(per_head, r):
        lo = lax.broadcasted_iota(jnp.int32, (r, LANES), 1) < SSD_P
        return jnp.concatenate([jnp.where(lo, per_head[0], per_head[1]),
                                jnp.where(lo, per_head[2], per_head[3])], axis=1)

    lane1 = lax.broadcasted_iota(jnp.int32, (c, LANES), 1)
    group = pl.program_id(1)

    def chunk(dirn, ci, x_ref, b_ref, c_ref, dt_ref, ac_ref, at_ref, e5_ref, y_ref):
        rs = slice(ci * c, (ci + 1) * c)
        dt5 = _dot(dt_ref[rs, :].astype(BF16), e5_ref[...])
        ac = ac_ref[rs, :]
        first = SSD_HEADS * dirn + 4 * group
        cols = [jnp.sum(jnp.where(lane1 == first + hh, ac, 0.0), axis=1, keepdims=True)
                for hh in range(4)]
        acr = at_ref[ci]
        tots = [col[c - 1:c] if dirn == 0 else col[0:1] for col in cols]
        dt2 = narrow([dt5[:, hh * LANES:(hh + 1) * LANES] for hh in range(4)], c)
        ac2 = narrow(cols, c)
        tot2 = narrow(tots, 1)
        x = x_ref[rs, :].astype(F32)
        bm = b_ref[rs, :]
        cmat = c_ref[rs, :]
        xdt = x * dt2
        cbm = jnp.where(masks[dirn], _dot_nt(cmat, bm), 0.0)
        st = st_ref[dirn]
        ms, xs = [], []
        for hh in range(4):
            col = cols[hh]
            row = acr[hh:hh + 1, :]
            ms.append((cbm * jnp.exp(jnp.minimum(col - row, 0.0))).astype(BF16))
            xs.append(jnp.where(head_masks[hh], xdt, 0.0).astype(BF16))
        y = (_dot(cmat, st.astype(BF16)) * jnp.exp(ac2)
             + _dot(jnp.concatenate(ms, axis=1), jnp.concatenate(xs, axis=0)))
        y_ref[rs, :] = y.astype(y_ref.dtype)
        xw = (xdt * jnp.exp(tot2 - ac2)).astype(BF16)
        st_ref[dirn] = st * jnp.exp(tot2) + _dot_tn(bm, xw)

    for cc in range(nchunk):
        chunk(0, cc, xf_ref, bf_ref, cf_ref, dtf_ref, acf_ref, atf_ref, e5f_ref, yf_ref)
        chunk(1, nchunk - 1 - cc, xb_ref, bb_ref, cb_ref, dtb_ref, acb_ref, atb_ref, e5b_ref, yb_ref)


def ssd_core(xbc, dt, ac, act, bsz, seq, rows=1024):
    t = bsz * seq
    nblk = seq // rows
    c = SSD_CHUNK
    hp = 4 * SSD_P
    src = 4 * jnp.arange(2 * SSD_GROUPS)[:, None] + jnp.arange(4)[None, :]
    onehot = (jnp.arange(LANES)[None, None, :] == src[..., None]).astype(BF16)
    e5 = jnp.repeat(onehot, LANES, axis=1).transpose(0, 2, 1)

    def side(bwd):
        def rb(b, i):
            return b * nblk + (nblk - 1 - i if bwd else i)
        d = int(bwd)
        return [pl.BlockSpec((rows, hp), lambda b, g, i: (rb(b, i), g)),
                pl.BlockSpec((rows, SSD_N), lambda b, g, i: (rb(b, i), 16 + g)),
                pl.BlockSpec((rows, SSD_N), lambda b, g, i: (rb(b, i), 24 + g)),
                pl.BlockSpec((rows, LANES), lambda b, g, i: (rb(b, i), 0)),
                pl.BlockSpec((rows, LANES), lambda b, g, i: (rb(b, i), 0)),
                pl.BlockSpec((rows // c, None, 4, c), lambda b, g, i: (rb(b, i), d * SSD_GROUPS + g, 0, 0)),
                pl.BlockSpec((None, LANES, 4 * LANES), lambda b, g, i: (d * SSD_GROUPS + g, 0, 0))]

    out_f = pl.BlockSpec((rows, hp), lambda b, g, i: (b * nblk + i, g))
    out_b = pl.BlockSpec((rows, hp), lambda b, g, i: (b * nblk + nblk - 1 - i, g))
    oshape = jax.ShapeDtypeStruct((t, SSD_INNER), BF16)
    kern = functools.partial(_ssd_kernel, nchunk=rows // c)
    args = (xbc, xbc, xbc, dt, ac, act, e5)
    return pl.pallas_call(
        kern,
        grid=(bsz, SSD_GROUPS, nblk),
        in_specs=side(False) + side(True),
        out_specs=[out_f, out_b],
        out_shape=[oshape, oshape],
        scratch_shapes=[pltpu.VMEM((2, SSD_N, hp), F32)],
        compiler_params=_cparams(("parallel", "parallel", "arbitrary")),
        name="ssd_core",
    )(*args, *args)


def _ssd_out_kernel(yf_ref, yb_ref, x_ref, z_ref, dsk_ref, g_ref, w_ref, h_ref, o_ref):
    y = yf_ref[...].astype(F32) + yb_ref[...].astype(F32) + x_ref[...].astype(F32) * dsk_ref[...]
    y = y * _silu(z_ref[...].astype(F32))
    gw = SSD_INNER // SSD_GROUPS
    parts = []
    for gg in range(SSD_GROUPS):
        yg = y[:, gg * gw:(gg + 1) * gw]
        ms = jnp.mean(yg * yg, axis=-1, keepdims=True)
        parts.append(yg * lax.rsqrt(ms + EPS))
    a = (jnp.concatenate(parts, axis=1) * g_ref[...]).astype(BF16)
    o_ref[...] = h_ref[...] + _dot(a, w_ref[...])


def ssd_out(y_f, y_b, xbc, zx, d_skip, g_norm, w_out, h, tm=512):
    t, n = h.shape
    wide = pl.BlockSpec((tm, SSD_INNER), lambda i: (i, 0))
    vec = pl.BlockSpec((1, SSD_INNER), lambda i: (0, 0))
    return pl.pallas_call(
        _ssd_out_kernel,
        grid=(t // tm,),
        in_specs=[wide, wide, wide, wide, vec, vec,
                  pl.BlockSpec((SSD_INNER, n), lambda i: (0, 0)),
                  pl.BlockSpec((tm, n), lambda i: (i, 0))],
        out_specs=pl.BlockSpec((tm, n), lambda i: (i, 0)),
        out_shape=jax.ShapeDtypeStruct((t, n), F32),
        compiler_params=_cparams(("parallel",)),
        name="ssd_out",
    )(y_f, y_b, xbc, zx, jnp.repeat(d_skip, SSD_P).reshape(1, SSD_INNER),
      g_norm.reshape(1, SSD_INNER), w_out, h)


def ssd_layer(h, norm_g, w_in, conv_w, conv_b, dt_bias_f, a_log_f, dt_bias_b, a_log_b, d_skip,
              g_norm, w_out, bsz, seq):
    n_main = SSD_INNER + SSD_CONV_DIM
    wdt = jnp.pad(w_in[:, n_main:], ((0, 0), (0, LANES - 2 * SSD_HEADS))).astype(BF16)
    zx, dtr = norm_matmul(h, norm_g, w_in[:, :n_main].astype(BF16), wdt)
    xbc = ssd_conv(zx, conv_w, conv_b, bsz, seq)
    dt_bias = jnp.concatenate([dt_bias_f, dt_bias_b])
    a_neg = -jnp.exp(jnp.concatenate([a_log_f, a_log_b]))
    dt, ac, act = ssd_prep(dtr, dt_bias, a_neg)
    y_f, y_b = ssd_core(xbc, dt, ac, act, bsz, seq)
    return ssd_out(y_f, y_b, xbc, zx, d_skip, g_norm, w_out.astype(BF16), h), None


DIL_TQ = 128


def _dil_kernel(q_ref, kp_ref, kc_ref, kn_ref, vp_ref, vc_ref, vn_ref, o_ref, lse_ref, *, bounded):
    ai = pl.program_id(2)
    na = pl.num_programs(2)
    tqb = q_ref.shape[0]
    tq = DIL_TQ
    side = DIL_SIDE
    kcat = jnp.concatenate([kp_ref[...], kc_ref[...], kn_ref[...]], axis=0)
    vcat = jnp.concatenate([vp_ref[...], vc_ref[...], vn_ref[...]], axis=0)
    nk = tq + 2 * side
    ii = lax.broadcasted_iota(jnp.int32, (4 * tq, nk), 0) & (tq - 1)
    jj = lax.broadcasted_iota(jnp.int32, (4 * tq, nk), 1)
    band = (jj >= ii) & (jj <= ii + 2 * side)
    lane = lax.broadcasted_iota(jnp.int32, (tq, ROPE_GROUP), 1)
    for sb in range(tqb // tq):
        qrows = slice(sb * tq, (sb + 1) * tq)
        krows = slice(sb * tq, sb * tq + nk)
        kpos = ai * tqb + sb * tq - side + jj
        valid = band & (kpos >= 0) & (kpos < na * tqb)
        for g4 in range(DIL_HEADS // 4):
            gsl = slice(g4 * ROPE_GROUP, (g4 + 1) * ROPE_GROUP)
            q4 = q_ref[qrows, gsl]
            zq = jnp.zeros_like(q4)
            qst = jnp.concatenate([jnp.where(_head_lane_mask(q4.shape, u), q4, zq) for u in range(4)],
                                  axis=0)
            s = jnp.where(valid, _dot_nt(qst, kcat[krows, gsl]), NEG_BIG)
            if bounded:
                p = jnp.exp2(s)
            else:
                m = jnp.max(s, axis=1, keepdims=True)
                p = jnp.exp2(s - m)
            den = jnp.sum(p, axis=1, keepdims=True)
            r = _dot(p.astype(BF16), vcat[krows, gsl])
            lse = jnp.log(den) if bounded else m * (1.0 / LOG2E) + jnp.log(den)
            out = jnp.zeros((tq, ROPE_GROUP), F32)
            lse4 = jnp.zeros((tq, ROPE_GROUP), F32)
            for u in range(4):
                rows = slice(u * tq, (u + 1) * tq)
                mine = (lane >= u * DIL_HD) & (lane < (u + 1) * DIL_HD)
                out = jnp.where(mine, r[rows] / den[rows], out)
                lse4 = jnp.where(mine, lse[rows], lse4)
            o_ref[qrows, gsl] = out.astype(o_ref.dtype)
            lse_ref[qrows, gsl] = lse4


def dil_core(pg, dilation, score_bound, bsz, seq):
    a = seq // dilation
    side = DIL_SIDE
    tq = min(4 * DIL_TQ, DIL_SUPER // dilation)
    na = a // tq
    nst = seq // DIL_SUPER
    per = DIL_SUPER // dilation // tq
    per_h = DIL_SUPER // dilation // side
    hs = tq // side
    nhb = a // side

    def cur(col):
        return pl.BlockSpec((None, None, tq, DIL_WIDTH),
                            lambda b, r, i: (b * nst + i // per, r, i % per, col))

    def halo(col, which):
        def imap(b, r, i):
            hb = (jnp.maximum(hs * i - 1, 0) if which == 0 else jnp.minimum(hs * (i + 1), nhb - 1))
            return (b * nst + hb // per_h, r, hb % per_h, col)
        return pl.BlockSpec((None, None, side, DIL_WIDTH), imap)

    out_spec = pl.BlockSpec((None, None, tq, DIL_WIDTH), lambda b, r, i: (b * nst + i // per, r, i % per, 0))
    oshape = (bsz * nst, dilation, DIL_SUPER // dilation, DIL_WIDTH)

    def call(bounded):
        return lambda x: pl.pallas_call(
            functools.partial(_dil_kernel, bounded=bounded),
            grid=(bsz, dilation, na),
            in_specs=[cur(0), halo(1, 0), cur(1), halo(1, 1), halo(2, 0), cur(2), halo(2, 1)],
            out_specs=[out_spec, out_spec],
            out_shape=[jax.ShapeDtypeStruct(oshape, BF16), jax.ShapeDtypeStruct(oshape, F32)],
            compiler_params=_cparams(("parallel", "parallel", "parallel")),
            name=f"dil_core_{dilation}" + ("_bounded" if bounded else "_online"),
        )(x, x, x, x, x, x, x)

    return lax.cond(score_bound <= SAFE_LOG2, call(True), call(False), pg)


def _dil_out_kernel(o0_ref, o1_ref, o2_ref, l0_ref, l1_ref, l2_ref, w_ref, h_ref, o_ref,
                    so_ref, sl_ref):
    tm = h_ref.shape[0]
    nch = DIL_WIDTH // LANES
    for gi, (og_ref, lg_ref) in enumerate(((o1_ref, l1_ref), (o2_ref, l2_ref))):
        d = DIL_PAIRS[gi + 1][1]
        n = tm // d
        for r in range(d):
            ov = og_ref[r].astype(F32)
            lv = lg_ref[r]
            for c in range(nch):
                so_ref[gi, c, pl.ds(r, n, stride=d), :] = ov[:, c * LANES:(c + 1) * LANES]
                sl_ref[gi, c, pl.ds(r, n, stride=d), :] = lv[:, c * LANES:(c + 1) * LANES]
    parts = []
    for c in range(nch):
        sl = slice(c * LANES, (c + 1) * LANES)
        l0 = l0_ref[0, :, sl]
        l1 = sl_ref[0, c]
        l2 = sl_ref[1, c]
        mx = jnp.maximum(jnp.maximum(l0, l1), l2)
        e0 = jnp.exp(l0 - mx)
        e1 = jnp.exp(l1 - mx)
        e2 = jnp.exp(l2 - mx)
        num = e0 * o0_ref[0, :, sl].astype(F32) + e1 * so_ref[0, c] + e2 * so_ref[1, c]
        parts.append((num / (e0 + e1 + e2)).astype(BF16))
    o_ref[...] = h_ref[...] + _dot(jnp.concatenate(parts, axis=1), w_ref[...])


def dil_out(os, lses, w_out, h, tm=512):
    t, n = h.shape
    per = DIL_SUPER // tm

    def grp(d):
        return pl.BlockSpec((None, d, tm // d, DIL_WIDTH), lambda i: (i // per, 0, i % per, 0))

    specs = [grp(d) for _, d in DIL_PAIRS]
    row = pl.BlockSpec((tm, n), lambda i: (i, 0))
    return pl.pallas_call(
        _dil_out_kernel,
        grid=(t // tm,),
        in_specs=specs + specs + [pl.BlockSpec((DIL_WIDTH, n), lambda i: (0, 0)), row],
        out_specs=row,
        out_shape=jax.ShapeDtypeStruct((t, n), F32),
        scratch_shapes=[pltpu.VMEM((2, DIL_WIDTH // LANES, tm, LANES), F32),
                        pltpu.VMEM((2, DIL_WIDTH // LANES, tm, LANES), F32)],
        compiler_params=_cparams(("parallel",)),
        name="dil_out",
    )(*os, *lses, w_out, h)


def dil_layer(h, norm_g, w_in, g_q, g_k, w_out, bsz, seq):
    qscale = (DIL_HD ** -0.5) * LOG2E
    os, lses = [], []
    tables = _rope_tables(seq)
    ngrp = len(DIL_PAIRS)
    for gi, (_, dilation) in enumerate(DIL_PAIRS):
        cols = ((jnp.arange(3)[:, None] * ngrp + gi) * DIL_WIDTH + jnp.arange(DIL_WIDTH)[None, :]).reshape(-1)
        rope_gain = jnp.concatenate([jnp.tile(g_q[gi], DIL_HEADS) * qscale, jnp.tile(g_k[gi], DIL_HEADS)])
        pg = norm_matmul_rope(h, norm_g, w_in, cols, rope_gain, tables, seq, dil=dilation)
        o, lse = dil_core(pg, dilation, _score_bound(g_q[gi], g_k[gi], qscale), bsz, seq)
        os.append(o)
        lses.append(lse)
    return dil_out(os, lses, w_out.astype(BF16), h), None


def kernel(x, p, norm_mix, norm_ffn, ffn_w_in, ffn_w_out, ple_norm, ple_w_gate, ple_w_proj, gla_w_in, gla_w_gate_f, gla_b_gate_f, gla_w_gate_b, gla_b_gate_b, gla_g_out, gla_w_out, diff_w_in, diff_g_q, diff_g_k, diff_lam_q1, diff_lam_k1, diff_lam_q2, diff_lam_k2, diff_g_sub, diff_w_out, ssd_w_in, ssd_conv_w, ssd_conv_b, ssd_dt_bias_f, ssd_a_log_f, ssd_dt_bias_b, ssd_a_log_b, ssd_d, ssd_g_norm, ssd_w_out, dil_w_in, dil_g_q, dil_g_k, dil_w_out):
    bsz, seq, dm = x.shape
    t = bsz * seq
    h = x.reshape(t, dm)
    for i in range(DEPTH):
        kind = i % 4
        j = i // 4
        if kind == 0:
            h, mixer = gla_layer(h, norm_mix[i], gla_w_in[j], gla_w_gate_f[j], gla_b_gate_f[j],
                                 gla_w_gate_b[j], gla_b_gate_b[j], gla_g_out[j], gla_w_out[j], bsz, seq)
        elif kind == 1:
            h, mixer = diff_layer(h, norm_mix[i], i, diff_w_in[j], diff_g_q[j], diff_g_k[j],
                                  diff_lam_q1[j], diff_lam_k1[j], diff_lam_q2[j], diff_lam_k2[j],
                                  diff_g_sub[j], diff_w_out[j], bsz, seq)
        elif kind == 2:
            h, mixer = ssd_layer(h, norm_mix[i], ssd_w_in[j], ssd_conv_w[j], ssd_conv_b[j],
                                 ssd_dt_bias_f[j], ssd_a_log_f[j], ssd_dt_bias_b[j], ssd_a_log_b[j],
                                 ssd_d[j], ssd_g_norm[j], ssd_w_out[j], bsz, seq)
        else:
            h, mixer = dil_layer(h, norm_mix[i], dil_w_in[j], dil_g_q[j], dil_g_k[j], dil_w_out[j],
                                 bsz, seq)
        ffn_ple_params = (norm_ffn[i], ffn_w_in[i].astype(BF16), ffn_w_out[i].astype(BF16), ple_norm[i],
                          ple_w_gate[i].astype(BF16), p[i].reshape(t, PLE_DIM), ple_w_proj[i].astype(BF16))
        h = ffn_ple_residual(h, ffn_ple_params, mixer)
    return h.reshape(bsz, seq, dm)
```

```python
import functools
import math

import jax
import jax.numpy as jnp
from jax import lax
from jax.experimental import pallas as pl
from jax.experimental.pallas import tpu as pltpu

F32 = jnp.float32
BF16 = jnp.bfloat16

D_MODEL = 1024
DEPTH = 4
PLE_DIM = 256
ROPE_THETA = 10000.0
EPS = 1e-6
D_FF = 2816
LOG2E = math.log2(math.e)
NEG_BIG = -1e30

GLA_HEADS = 4
GLA_DK = 512
GLA_DV = 1024
GLA_HK = 128
GLA_HV = 256
GLA_RANK = 16
GLA_TAU = 16.0
GLA_CHUNK = 64

DIFF_HEADS = 8
DIFF_HD = 64

SSD_INNER = 2048
SSD_P = 64
SSD_HEADS = 32
SSD_GROUPS = 8
SSD_N = 128
SSD_CONV = 5
SSD_CHUNK = 128
SSD_CONV_DIM = 4096

DIL_PAIRS = ((128, 1), (512, 4), (2048, 16))
DIL_HEADS = 16
DIL_HD = 64
DIL_WIDTH = 1024
DIL_SIDE = 64
DIL_SUPER = 2048

LANES = 128
ROPE_GROUP = 256
SAFE_LOG2 = 60.0


def _cparams(sem, vmem_mb=None):
    kw = dict(dimension_semantics=sem)
    if vmem_mb is not None:
        kw["vmem_limit_bytes"] = vmem_mb << 20
    return pltpu.CompilerParams(**kw)


def _dot(a, b):
    return jnp.dot(a, b, preferred_element_type=F32)


def _dot_nt(a, b):
    return lax.dot_general(a, b, (((1,), (1,)), ((), ())), preferred_element_type=F32)


def _dot_tn(a, b):
    return lax.dot_general(a, b, (((0,), (0,)), ((), ())), preferred_element_type=F32)


def _split(x):
    hi = x.astype(BF16)
    return hi, (x - hi.astype(F32)).astype(BF16)


def _dot_split(a_bf16, x):
    hi, lo = _split(x)
    return _dot(a_bf16, hi) + _dot(a_bf16, lo)


def _rms_rows(x, g):
    ms = jnp.mean(x * x, axis=-1, keepdims=True)
    return x * lax.rsqrt(ms + EPS) * g


def _silu(x):
    return x * jax.nn.sigmoid(x)


def _softplus(x):
    return jnp.maximum(x, 0.0) + jnp.log(1.0 + jnp.exp(-jnp.abs(x)))


def _tri(c, upper=False):
    ii = lax.broadcasted_iota(jnp.int32, (c, c), 0)
    jj = lax.broadcasted_iota(jnp.int32, (c, c), 1)
    return (jj >= ii) if upper else (jj <= ii)


def _resident(shape):
    return pl.BlockSpec(shape, lambda i: (0,) * len(shape), pipeline_mode=pl.Buffered(1))


PROJ_COLS = 1024


def _nm_kernel(x_ref, g_ref, w_ref, ws_ref, o_ref, os_ref):
    xn = _rms_rows(x_ref[...], g_ref[...]).astype(BF16)
    for c in range(w_ref.shape[1] // PROJ_COLS):
        sl = slice(c * PROJ_COLS, (c + 1) * PROJ_COLS)
        o_ref[:, sl] = _dot(xn, w_ref[:, sl]).astype(o_ref.dtype)
    os_ref[...] = _dot(xn, ws_ref[...])


def norm_matmul(x, g, w, w_side, tm=512):
    t, k = x.shape
    n = w.shape[1]
    return pl.pallas_call(
        _nm_kernel,
        grid=(t // tm,),
        in_specs=[pl.BlockSpec((tm, k), lambda i: (i, 0)),
                  _resident((1, k)), _resident((k, n)), _resident((k, LANES))],
        out_specs=[pl.BlockSpec((tm, n), lambda i: (i, 0)), pl.BlockSpec((tm, LANES), lambda i: (i, 0))],
        out_shape=[jax.ShapeDtypeStruct((t, n), BF16), jax.ShapeDtypeStruct((t, LANES), F32)],
        compiler_params=_cparams(("parallel",)),
        name="norm_matmul",
    )(x, g.reshape(1, k), w, w_side)


def _rope_tables(seq):
    half = DIFF_HD // 2
    inv = ROPE_THETA ** (-jnp.arange(half, dtype=F32) * 2.0 / DIFF_HD)
    ang = jnp.arange(seq, dtype=jnp.int32).astype(F32)[:, None] * inv[None, :]
    return jnp.tile(jnp.cos(ang), (1, 4)), jnp.tile(jnp.sin(ang), (1, 4))


def _rope_col_perm(n_cols):
    half = DIFF_HD // 2
    g = jnp.arange(n_cols // ROPE_GROUP)[:, None, None, None] * ROPE_GROUP
    part = jnp.arange(2)[None, :, None, None] * half
    head = jnp.arange(4)[None, None, :, None] * DIFF_HD
    d = jnp.arange(half)[None, None, None, :]
    return (g + head + part + d).reshape(-1)


def _seg32_ones():
    r = jnp.arange(LANES)
    return (r[:, None] // 32 == r[None, :] // 32).astype(BF16)


def _head_lane_mask(shape, u):
    lane = lax.broadcasted_iota(jnp.int32, shape, len(shape) - 1)
    return (lane % LANES) // (DIFF_HD // 2) == u


def _nm_rope_kernel(x_ref, g_ref, w_ref, hg_ref, cos_ref, sin_ref, bd_ref, o_ref, xn_ref, *xs_refs,
                    n_rope_cols, dil):
    tm = x_ref.shape[0]
    n = tm // dil
    if dil == 1:
        xn_ref[...] = _rms_rows(x_ref[...], g_ref[...]).astype(BF16)
    else:
        xs_ref = xs_refs[0]
        nch = x_ref.shape[1] // LANES
        for c in range(nch):
            xs_ref[c] = x_ref[:, c * LANES:(c + 1) * LANES]
        for r in range(dil):
            xr = jnp.concatenate([xs_ref[c, pl.ds(r, n, stride=dil), :] for c in range(nch)], axis=1)
            xn_ref[r * n:(r + 1) * n, :] = _rms_rows(xr, g_ref[...]).astype(BF16)

    def put(sl, val):
        if len(o_ref.shape) == 2:
            o_ref[:, sl] = val
        else:
            for r in range(dil):
                o_ref[r, :, sl] = val[r * n:(r + 1) * n]

    cos = cos_ref[...]
    sin = sin_ref[...]
    bd = bd_ref[...]
    xn = xn_ref[...]
    wide = 2 * ROPE_GROUP
    for cw in range(n_rope_cols // wide):
        acc2 = _dot(xn, w_ref[:, cw * wide:(cw + 1) * wide])
        for c2 in range(2):
            c = 2 * cw + c2
            sl = slice(c * ROPE_GROUP, (c + 1) * ROPE_GROUP)
            a = acc2[:, c2 * ROPE_GROUP:c2 * ROPE_GROUP + LANES]
            b = acc2[:, c2 * ROPE_GROUP + LANES:(c2 + 1) * ROPE_GROUP]
            ss = _dot((a * a + b * b).astype(BF16), bd)
            inv = lax.rsqrt(ss * (1.0 / DIFF_HD) + EPS)
            a = a * inv * hg_ref[:, c * ROPE_GROUP:c * ROPE_GROUP + LANES]
            b = b * inv * hg_ref[:, c * ROPE_GROUP + LANES:(c + 1) * ROPE_GROUP]
            put(sl, jnp.concatenate([a * cos - b * sin, b * cos + a * sin], axis=1).astype(o_ref.dtype))
    for c in range(n_rope_cols // PROJ_COLS, w_ref.shape[1] // PROJ_COLS):
        sl = slice(c * PROJ_COLS, (c + 1) * PROJ_COLS)
        put(sl, _dot(xn, w_ref[:, sl]).astype(o_ref.dtype))


def norm_matmul_rope(x, g, w_full, cols, rope_gain, tables, seq, dil=None, tm=512):
    t, k = x.shape
    n = cols.shape[0]
    n_rope_cols = rope_gain.shape[0]
    perm = _rope_col_perm(n_rope_cols)
    w = w_full[:, jnp.concatenate([cols[:n_rope_cols][perm], cols[n_rope_cols:]])].astype(BF16)
    rope_gain = rope_gain[perm]
    cos, sin = tables
    nsb = seq // tm
    d = 1 if dil is None else dil
    if d > 1:
        cos, sin = (tb.reshape(nsb, tm // d, d, LANES).transpose(0, 2, 1, 3).reshape(seq, LANES)
                    for tb in (cos, sin))
    kern = functools.partial(_nm_rope_kernel, n_rope_cols=n_rope_cols, dil=d)
    if dil is None:
        out_spec = pl.BlockSpec((tm, n), lambda i: (i, 0))
        out_shape = jax.ShapeDtypeStruct((t, n), BF16)
    else:
        per = DIL_SUPER // tm
        out_spec = pl.BlockSpec((None, d, tm // d, n), lambda i: (i // per, 0, i % per, 0))
        out_shape = jax.ShapeDtypeStruct((t // DIL_SUPER, d, DIL_SUPER // d, n), BF16)
    scratch = [pltpu.VMEM((tm, k), BF16)]
    if d > 1:
        scratch.append(pltpu.VMEM((k // LANES, tm, LANES), F32))
    return pl.pallas_call(
        kern,
        grid=(t // tm,),
        in_specs=[pl.BlockSpec((tm, k), lambda i: (i, 0)),
                  _resident((1, k)), _resident((k, n)), _resident((1, n_rope_cols)),
                  pl.BlockSpec((tm, LANES), lambda i: (i % nsb, 0)),
                  pl.BlockSpec((tm, LANES), lambda i: (i % nsb, 0)),
                  _resident((LANES, LANES))],
        out_specs=out_spec,
        out_shape=out_shape,
        scratch_shapes=scratch,
        compiler_params=_cparams(("parallel",)),
        name="norm_matmul_rope",
    )(x, g.reshape(1, k), w, rope_gain.reshape(1, n_rope_cols), cos, sin, _seg32_ones())


def _score_bound(g_q, g_k, qscale):
    return 1.05 * DIFF_HD * qscale * jnp.max(jnp.abs(g_q)) * jnp.max(jnp.abs(g_k))


FFN_SPLIT = 1536


def _ffn_ple_kernel(*refs, mixer_act):
    x_ref, gf_ref, win_ref, wout_ref, gp_ref, wg_ref, p_ref, wp_ref, o_ref = refs[-9:]
    x = x_ref[...]
    if mixer_act is not None:
        x = x + _dot(mixer_act(*refs[:-10]), refs[-10][...])
    xn = _rms_rows(x, gf_ref[...]).astype(BF16)
    h2 = x
    for lo, hi in ((0, FFN_SPLIT), (FFN_SPLIT, D_FF)):
        gate = _dot(xn, win_ref[:, lo:hi])
        up = _dot(xn, win_ref[:, D_FF + lo:D_FF + hi])
        a = (_silu(gate) * up).astype(BF16)
        h2 = h2 + _dot(a, wout_ref[lo:hi, :])
    hn = _rms_rows(h2, gp_ref[...]).astype(BF16)
    gate2 = jax.nn.sigmoid(_dot(hn, wg_ref[...]))
    o_ref[...] = h2 + gate2 * _dot(p_ref[...].astype(BF16), wp_ref[...])


FFN_ROWS = 512


def ffn_ple_residual(h, ffn_ple_params, mixer=None):
    g_ffn, w_in, w_out, g_ple, w_gate, p, w_proj = ffn_ple_params
    t, k = h.shape
    tm = FFN_ROWS
    act, m_ops, m_specs = None, (), []
    if mixer is not None:
        act, m_ops, m_specs, w_mix = mixer
        m_ops = tuple(m_ops) + (w_mix,)
        m_specs = list(m_specs) + [_resident(w_mix.shape)]
    return pl.pallas_call(
        functools.partial(_ffn_ple_kernel, mixer_act=act),
        grid=(t // tm,),
        in_specs=m_specs + [pl.BlockSpec((tm, k), lambda i: (i, 0)),
                            _resident((1, k)),
                            _resident((k, 2 * D_FF)),
                            _resident((D_FF, k)),
                            _resident((1, k)),
                            _resident((k, k)),
                            pl.BlockSpec((tm, PLE_DIM), lambda i: (i, 0)),
                            _resident((PLE_DIM, k))],
        out_specs=pl.BlockSpec((tm, k), lambda i: (i, 0)),
        out_shape=jax.ShapeDtypeStruct((t, k), F32),
        compiler_params=_cparams(("parallel",)),
        name="ffn_ple",
    )(*m_ops, h, g_ffn.reshape(1, k), w_in, w_out, g_ple.reshape(1, k), w_gate, p, w_proj)


def _gla_kernel(qf_ref, kf_ref, vf_ref, zf_ref, qb_ref, kb_ref, vb_ref, zb_ref, wg_ref, bg_ref,
                of_ref, ob_ref, st_ref, la_ref, *, nchunk):
    @pl.when(pl.program_id(1) == 0)
    def _():
        st_ref[...] = jnp.zeros_like(st_ref)

    c = GLA_CHUNK
    for dirn, z_ref in enumerate((zf_ref, zb_ref)):
        x = _dot(z_ref[...].astype(BF16), wg_ref[dirn]) + bg_ref[dirn]
        la_ref[dirn] = (jnp.minimum(x, 0.0) - jnp.log(1.0 + jnp.exp(-jnp.abs(x)))) * (1.0 / GLA_TAU)
    cms = (_tri(c).astype(BF16), _tri(c, upper=True).astype(BF16))
    mks = (_tri(c), jnp.logical_not(_tri(c)))

    def chunk(dirn, hh, ci, q_ref, k_ref, v_ref, o_ref):
        rs = slice(ci * c, (ci + 1) * c)
        ks = slice(hh * GLA_HK, (hh + 1) * GLA_HK)
        vs = slice(hh * GLA_HV, (hh + 1) * GLA_HV)
        la = la_ref[dirn, rs, ks]
        b = _dot_split(cms[dirn], la)
        tot = jnp.sum(la, axis=0, keepdims=True)
        qc = q_ref[rs, ks].astype(F32)
        kc = k_ref[rs, ks].astype(F32)
        vc = v_ref[rs, vs]
        q_in = (qc * jnp.exp(b) * (GLA_HK ** -0.5)).astype(BF16)
        k_out = (kc * jnp.exp(-b)).astype(BF16)
        k_dec = (kc * jnp.exp(tot - b)).astype(BF16)
        sc = jnp.where(mks[dirn], _dot_nt(q_in, k_out), 0.0).astype(BF16)
        st = st_ref[dirn, hh]
        o = _dot(sc, vc) + _dot_nt(q_in, st.astype(BF16))
        o_ref[rs, vs] = o.astype(o_ref.dtype)
        st_ref[dirn, hh] = st * jnp.exp(tot) + _dot_tn(vc, k_dec)

    for cc in range(nchunk):
        for hh in range(GLA_HEADS):
            chunk(0, hh, cc, qf_ref, kf_ref, vf_ref, of_ref)
            chunk(1, hh, nchunk - 1 - cc, qb_ref, kb_ref, vb_ref, ob_ref)


def gla_core(proj, z, wg, bg, bsz, seq, rows=512):
    t = bsz * seq
    nblk = seq // rows

    def side(bwd):
        def rb(b, i):
            return b * nblk + (nblk - 1 - i if bwd else i)
        return [pl.BlockSpec((rows, GLA_DK), lambda b, i: (rb(b, i), 0)),
                pl.BlockSpec((rows, GLA_DK), lambda b, i: (rb(b, i), 1)),
                pl.BlockSpec((rows, GLA_DV), lambda b, i: (rb(b, i), 1)),
                pl.BlockSpec((rows, LANES), lambda b, i: (rb(b, i), 0))]

    out_f = pl.BlockSpec((rows, GLA_DV), lambda b, i: (b * nblk + i, 0))
    out_b = pl.BlockSpec((rows, GLA_DV), lambda b, i: (b * nblk + nblk - 1 - i, 0))
    oshape = jax.ShapeDtypeStruct((t, GLA_DV), BF16)
    kern = functools.partial(_gla_kernel, nchunk=rows // GLA_CHUNK)
    return pl.pallas_call(
        kern,
        grid=(bsz, nblk),
        in_specs=side(False) + side(True) + [pl.BlockSpec((2, LANES, GLA_DK), lambda b, i: (0, 0, 0)),
                                             pl.BlockSpec((2, 1, GLA_DK), lambda b, i: (0, 0, 0))],
        out_specs=[out_f, out_b],
        out_shape=[oshape, oshape],
        scratch_shapes=[pltpu.VMEM((2, GLA_HEADS, GLA_HV, GLA_HK), F32),
                        pltpu.VMEM((2, rows, GLA_DK), F32)],
        compiler_params=_cparams(("parallel", "arbitrary")),
        name="gla_core",
    )(proj, proj, proj, z, proj, proj, proj, z, wg, bg)


def _gla_act(of_ref, ob_ref, r_ref, g_ref):
    o = of_ref[...].astype(F32) + ob_ref[...].astype(F32)
    parts = []
    for hh in range(GLA_HEADS):
        oh = o[:, hh * GLA_HV:(hh + 1) * GLA_HV]
        ms = jnp.mean(oh * oh, axis=-1, keepdims=True)
        parts.append(oh * lax.rsqrt(ms + EPS))
    on = jnp.concatenate(parts, axis=1) * g_ref[...]
    return (on * _silu(r_ref[...].astype(F32))).astype(BF16)


def gla_mixer_out(o_f, o_b, proj, g_out, w_out):
    row = pl.BlockSpec((FFN_ROWS, GLA_DV), lambda i: (i, 0))
    specs = [row, row, pl.BlockSpec((FFN_ROWS, GLA_DV), lambda i: (i, 2)), _resident((1, GLA_DV))]
    return _gla_act, (o_f, o_b, proj, jnp.tile(g_out, GLA_HEADS).reshape(1, GLA_DV)), specs, w_out


def gla_layer(h, norm_g, w_in, w_gate_f, b_gate_f, w_gate_b, b_gate_b, g_out, w_out, bsz, seq):
    n_main = 2 * GLA_DK + 2 * GLA_DV
    wz = jnp.pad(w_in[:, n_main:], ((0, 0), (0, LANES - 2 * GLA_RANK))).astype(BF16)
    proj, z = norm_matmul(h, norm_g, w_in[:, :n_main].astype(BF16), wz)
    wg = jnp.zeros((2, LANES, GLA_DK), F32)
    wg = wg.at[0, :GLA_RANK].set(w_gate_f).at[1, GLA_RANK:2 * GLA_RANK].set(w_gate_b).astype(BF16)
    bg = jnp.stack([b_gate_f, b_gate_b]).reshape(2, 1, GLA_DK)
    o_f, o_b = gla_core(proj, z, wg, bg, bsz, seq)
    return h, gla_mixer_out(o_f, o_b, proj, g_out, w_out.astype(BF16))


def _diff_kernel(q_ref, k_ref, v0_ref, v1_ref, lam_ref, gs_ref, o_ref, qz_ref, acc_ref, *m_refs,
                 tk, lambda_init, bounded):
    tq = q_ref.shape[0]
    q = q_ref[...]
    zero = jnp.zeros_like(q)
    for u in range(4):
        qz_ref[u] = jnp.where(_head_lane_mask(q.shape, u), q, zero)
    acc_ref[...] = jnp.zeros_like(acc_ref)
    if not bounded:
        m_ref = m_refs[0]
        m_ref[...] = jnp.full_like(m_ref, NEG_BIG)
    ones = jnp.ones((tk, LANES), BF16)

    def body(ki, carry):
        rows = pl.ds(pl.multiple_of(ki * tk, tk), tk)
        kb = k_ref[rows, :]
        vaug = [jnp.concatenate([v_ref[rows, :], ones], axis=1) for v_ref in (v0_ref, v1_ref)]
        for u in range(4):
            s = _dot_nt(qz_ref[u], kb)
            if bounded:
                acc_ref[u] += _dot(jnp.exp2(s).astype(BF16), vaug[u // 2])
            else:
                m_prev = m_ref[u]
                m_new = jnp.maximum(m_prev, jnp.max(s, axis=1, keepdims=True))
                alpha = jnp.exp2(m_prev - m_new)
                p = jnp.exp2(s - jnp.tile(m_new, (1, tk // LANES)))
                acc_ref[u] = jnp.tile(alpha, (1, 2)) * acc_ref[u] + _dot(p.astype(BF16), vaug[u // 2])
                m_ref[u] = m_new
        return carry

    lax.fori_loop(0, k_ref.shape[0] // tk, body, 0, unroll=4 if bounded else 1)

    lam4 = lam_ref[...]
    lam = (jnp.exp(jnp.sum(lam4[0:1] * lam4[1:2], axis=1, keepdims=True))
           - jnp.exp(jnp.sum(lam4[2:3] * lam4[3:4], axis=1, keepdims=True)) + lambda_init)
    for hh in range(2):
        a1 = acc_ref[2 * hh]
        a2 = acc_ref[2 * hh + 1]
        o = a1[:, :LANES] / a1[:, LANES:] - lam * (a2[:, :LANES] / a2[:, LANES:])
        ms = jnp.mean(o * o, axis=-1, keepdims=True)
        o_ref[:, hh * LANES:(hh + 1) * LANES] = (
            o * lax.rsqrt(ms + EPS) * gs_ref[...] * (1.0 - lambda_init)).astype(o_ref.dtype)


def diff_core(proj, lam4, g_sub, lambda_init, score_bound, bsz, seq, tq=1024, tk=512):
    t = bsz * seq
    nq = seq // tq
    npair = DIFF_HEADS // 2
    in_specs = [pl.BlockSpec((tq, ROPE_GROUP), lambda b, h, i: (b * nq + i, h)),
                pl.BlockSpec((seq, ROPE_GROUP), lambda b, h, i: (b, npair + h)),
                pl.BlockSpec((seq, LANES), lambda b, h, i: (b, 16 + 2 * h)),
                pl.BlockSpec((seq, LANES), lambda b, h, i: (b, 17 + 2 * h)),
                pl.BlockSpec((4, DIFF_HD), lambda b, h, i: (0, 0)),
                pl.BlockSpec((1, LANES), lambda b, h, i: (0, 0))]
    out_spec = pl.BlockSpec((tq, ROPE_GROUP), lambda b, h, i: (b * nq + i, h))
    out_shape = jax.ShapeDtypeStruct((t, D_MODEL), BF16)
    args = (proj, proj, proj, proj, lam4, g_sub.reshape(1, LANES))

    def call(bounded):
        scratch = [pltpu.VMEM((4, tq, ROPE_GROUP), BF16), pltpu.VMEM((4, tq, 2 * LANES), F32)]
        if not bounded:
            scratch.append(pltpu.VMEM((4, tq, LANES), F32))
        return lambda *a: pl.pallas_call(
            functools.partial(_diff_kernel, tk=tk, lambda_init=lambda_init, bounded=bounded),
            grid=(bsz, npair, nq), in_specs=in_specs, out_specs=out_spec, out_shape=out_shape,
            scratch_shapes=scratch,
            compiler_params=_cparams(("parallel", "parallel", "arbitrary")),
            name="diff_core_bounded" if bounded else "diff_core_online",
        )(*a)

    return lax.cond(score_bound <= SAFE_LOG2, call(True), call(False), *args)


def diff_layer(h, norm_g, layer_idx, w_in, g_q, g_k, lam_q1, lam_k1, lam_q2, lam_k2, g_sub, w_out,
               bsz, seq):
    lambda_init = 0.8 - 0.6 * math.exp(-0.3 * layer_idx)
    qscale = (DIFF_HD ** -0.5) * LOG2E
    rope_gain = jnp.concatenate([jnp.tile(g_q.reshape(-1), DIFF_HEADS) * qscale,
                                 jnp.tile(g_k.reshape(-1), DIFF_HEADS)])
    proj = norm_matmul_rope(h, norm_g, w_in, jnp.arange(3 * D_MODEL), rope_gain, _rope_tables(seq), seq)
    lam4 = jnp.stack([lam_q1, lam_k1, lam_q2, lam_k2])
    o = diff_core(proj, lam4, g_sub, lambda_init, _score_bound(g_q, g_k, qscale), bsz, seq)
    row = pl.BlockSpec((FFN_ROWS, D_MODEL), lambda i: (i, 0))
    return h, (lambda o_ref: o_ref[...], (o,), [row], w_out.astype(BF16))


CONV_BLOCK = 128


def _conv_kernel(prev_ref, cur_ref, next_ref, sh_ref, w_ref, b_ref, o_ref, buf_ref, *, halo):
    i = pl.program_id(1)
    ts = cur_ref.shape[0]
    zero = jnp.zeros(prev_ref.shape, prev_ref.dtype)
    buf_ref[0:halo, :] = jnp.where(i == 0, zero, prev_ref[...])
    buf_ref[halo:halo + ts, :] = cur_ref[...]
    buf_ref[halo + ts:, :] = jnp.where(i == pl.num_programs(1) - 1, zero, next_ref[...])
    w = w_ref[...]
    sh = sh_ref[...]
    cb = CONV_BLOCK
    for r in range(ts // cb):
        win = buf_ref[r * cb:(r + 1) * cb + 2 * halo, :]
        sft = _dot(sh, win)
        y = b_ref[...] + w[2:3] * cur_ref[r * cb:(r + 1) * cb, :].astype(F32)
        for slot, k in enumerate((0, 1, 3, 4)):
            y = y + w[k:k + 1] * sft[slot * cb:(slot + 1) * cb]
        o_ref[r * cb:(r + 1) * cb, :] = _silu(y).astype(o_ref.dtype)


def ssd_conv(zx, conv_w, conv_b, bsz, seq, ts=1024, tc=1024):
    t = bsz * seq
    halo = 16
    nrb = seq // ts
    hb = ts // halo
    nhb = seq // halo
    off = SSD_INNER // tc
    kern = functools.partial(_conv_kernel, halo=halo)
    wpad = jnp.pad(conv_w, ((0, 8 - SSD_CONV), (0, 0)))
    cb = CONV_BLOCK
    shift = jnp.array([-2, -1, 1, 2])[:, None, None]
    sh = (jnp.arange(cb + 2 * halo)[None, None, :] == jnp.arange(cb)[None, :, None] + halo + shift)
    sh = sh.reshape(4 * cb, cb + 2 * halo).astype(BF16)
    return pl.pallas_call(
        kern,
        grid=(bsz, nrb, SSD_CONV_DIM // tc),
        in_specs=[pl.BlockSpec((halo, tc), lambda b, i, c: (b * nhb + jnp.maximum(i * hb - 1, 0), off + c)),
                  pl.BlockSpec((ts, tc), lambda b, i, c: (b * nrb + i, off + c)),
                  pl.BlockSpec((halo, tc), lambda b, i, c: (b * nhb + jnp.minimum((i + 1) * hb, nhb - 1), off + c)),
                  pl.BlockSpec((4 * cb, cb + 2 * halo), lambda b, i, c: (0, 0)),
                  pl.BlockSpec((8, tc), lambda b, i, c: (0, c)),
                  pl.BlockSpec((1, tc), lambda b, i, c: (0, c))],
        out_specs=pl.BlockSpec((ts, tc), lambda b, i, c: (b * nrb + i, c)),
        out_shape=jax.ShapeDtypeStruct((t, SSD_CONV_DIM), BF16),
        scratch_shapes=[pltpu.VMEM((ts + 2 * halo, tc), BF16)],
        compiler_params=_cparams(("parallel", "parallel", "parallel")),
        name="ssd_conv",
    )(zx, zx, zx, sh, wpad, conv_b.reshape(1, -1))


def _ssd_prep_kernel(dtr_ref, bias_ref, aneg_ref, dt_ref, ac_ref, at_ref):
    c = SSD_CHUNK
    lower = _tri(c).astype(BF16)
    upper = _tri(c, upper=True).astype(BF16)
    lane = lax.broadcasted_iota(jnp.int32, (c, LANES), 1)
    for cc in range(dtr_ref.shape[0] // c):
        rs = slice(cc * c, (cc + 1) * c)
        dt = _softplus(dtr_ref[rs, :] + bias_ref[...])
        adt = dt * aneg_ref[...]
        hi, lo = _split(adt)
        ac = jnp.where(lane < SSD_HEADS, _dot(lower, hi) + _dot(lower, lo), _dot(upper, hi) + _dot(upper, lo))
        dt_ref[rs, :] = dt
        ac_ref[rs, :] = ac
        act = ac.T
        for k in range(2 * SSD_GROUPS):
            at_ref[cc, k] = act[4 * k:4 * k + 4, :]


def ssd_prep(dtr, dt_bias, a_neg, rows=512):
    t = dtr.shape[0]
    c = SSD_CHUNK
    bias = jnp.pad(dt_bias, (0, LANES - 2 * SSD_HEADS)).reshape(1, LANES)
    aneg = jnp.pad(a_neg, (0, LANES - 2 * SSD_HEADS)).reshape(1, LANES)
    row = pl.BlockSpec((rows, LANES), lambda i: (i, 0))
    vec = pl.BlockSpec((1, LANES), lambda i: (0, 0))
    return pl.pallas_call(
        _ssd_prep_kernel,
        grid=(t // rows,),
        in_specs=[row, vec, vec],
        out_specs=[row, row, pl.BlockSpec((rows // c, 2 * SSD_GROUPS, 4, c), lambda i: (i, 0, 0, 0))],
        out_shape=[jax.ShapeDtypeStruct((t, LANES), F32), jax.ShapeDtypeStruct((t, LANES), F32),
                   jax.ShapeDtypeStruct((t // c, 2 * SSD_GROUPS, 4, c), F32)],
        compiler_params=_cparams(("parallel",)),
        name="ssd_prep",
    )(dtr, bias, aneg)


def _ssd_kernel(xf_ref, bf_ref, cf_ref, dtf_ref, acf_ref, atf_ref, e5f_ref,
                xb_ref, bb_ref, cb_ref, dtb_ref, acb_ref, atb_ref, e5b_ref,
                yf_ref, yb_ref, st_ref, *, nchunk):
    @pl.when(pl.program_id(2) == 0)
    def _():
        st_ref[...] = jnp.zeros_like(st_ref)

    c = SSD_CHUNK
    masks = (_tri(c), jnp.logical_not(_tri(c)))
    lane2 = lax.broadcasted_iota(jnp.int32, (c, 4 * SSD_P), 1)
    head_masks = [(lane2 >= hh * SSD_P) & (lane2 < (hh + 1) * SSD_P) for hh in range(4)]

    def narrow(per_head, r):
        lo = lax.broadcasted_iota(jnp.int32, (r, LANES), 1) < SSD_P
        return jnp.concatenate([jnp.where(lo, per_head[0], per_head[1]),
                                jnp.where(lo, per_head[2], per_head[3])], axis=1)

    lane1 = lax.broadcasted_iota(jnp.int32, (c, LANES), 1)
    group = pl.program_id(1)

    def chunk(dirn, ci, x_ref, b_ref, c_ref, dt_ref, ac_ref, at_ref, e5_ref, y_ref):
        rs = slice(ci * c, (ci + 1) * c)
        dt5 = _dot(dt_ref[rs, :].astype(BF16), e5_ref[...])
        ac = ac_ref[rs, :]
        first = SSD_HEADS * dirn + 4 * group
        cols = [jnp.sum(jnp.where(lane1 == first + hh, ac, 0.0), axis=1, keepdims=True)
                for hh in range(4)]
        acr = at_ref[ci]
        tots = [col[c - 1:c] if dirn == 0 else col[0:1] for col in cols]
        dt2 = narrow([dt5[:, hh * LANES:(hh + 1) * LANES] for hh in range(4)], c)
        ac2 = narrow(cols, c)
        tot2 = narrow(tots, 1)
        x = x_ref[rs, :].astype(F32)
        bm = b_ref[rs, :]
        cmat = c_ref[rs, :]
        xdt = x * dt2
        cbm = jnp.where(masks[dirn], _dot_nt(cmat, bm), 0.0)
        st = st_ref[dirn]
        ms, xs = [], []
        for hh in range(4):
            col = cols[hh]
            row = acr[hh:hh + 1, :]
            ms.append((cbm * jnp.exp(jnp.minimum(col - row, 0.0))).astype(BF16))
            xs.append(jnp.where(head_masks[hh], xdt, 0.0).astype(BF16))
        y = (_dot(cmat, st.astype(BF16)) * jnp.exp(ac2)
             + _dot(jnp.concatenate(ms, axis=1), jnp.concatenate(xs, axis=0)))
        y_ref[rs, :] = y.astype(y_ref.dtype)
        xw = (xdt * jnp.exp(tot2 - ac2)).astype(BF16)
        st_ref[dirn] = st * jnp.exp(tot2) + _dot_tn(bm, xw)

    for cc in range(nchunk):
        chunk(0, cc, xf_ref, bf_ref, cf_ref, dtf_ref, acf_ref, atf_ref, e5f_ref, yf_ref)
        chunk(1, nchunk - 1 - cc, xb_ref, bb_ref, cb_ref, dtb_ref, acb_ref, atb_ref, e5b_ref, yb_ref)


def ssd_core(xbc, dt, ac, act, bsz, seq, rows=1024):
    t = bsz * seq
    nblk = seq // rows
    c = SSD_CHUNK
    hp = 4 * SSD_P
    src = 4 * jnp.arange(2 * SSD_GROUPS)[:, None] + jnp.arange(4)[None, :]
    onehot = (jnp.arange(LANES)[None, None, :] == src[..., None]).astype(BF16)
    e5 = jnp.repeat(onehot, LANES, axis=1).transpose(0, 2, 1)

    def side(bwd):
        def rb(b, i):
            return b * nblk + (nblk - 1 - i if bwd else i)
        d = int(bwd)
        return [pl.BlockSpec((rows, hp), lambda b, g, i: (rb(b, i), g)),
                pl.BlockSpec((rows, SSD_N), lambda b, g, i: (rb(b, i), 16 + g)),
                pl.BlockSpec((rows, SSD_N), lambda b, g, i: (rb(b, i), 24 + g)),
                pl.BlockSpec((rows, LANES), lambda b, g, i: (rb(b, i), 0)),
                pl.BlockSpec((rows, LANES), lambda b, g, i: (rb(b, i), 0)),
                pl.BlockSpec((rows // c, None, 4, c), lambda b, g, i: (rb(b, i), d * SSD_GROUPS + g, 0, 0)),
                pl.BlockSpec((None, LANES, 4 * LANES), lambda b, g, i: (d * SSD_GROUPS + g, 0, 0))]

    out_f = pl.BlockSpec((rows, hp), lambda b, g, i: (b * nblk + i, g))
    out_b = pl.BlockSpec((rows, hp), lambda b, g, i: (b * nblk + nblk - 1 - i, g))
    oshape = jax.ShapeDtypeStruct((t, SSD_INNER), BF16)
    kern = functools.partial(_ssd_kernel, nchunk=rows // c)
    args = (xbc, xbc, xbc, dt, ac, act, e5)
    return pl.pallas_call(
        kern,
        grid=(bsz, SSD_GROUPS, nblk),
        in_specs=side(False) + side(True),
        out_specs=[out_f, out_b],
        out_shape=[oshape, oshape],
        scratch_shapes=[pltpu.VMEM((2, SSD_N, hp), F32)],
        compiler_params=_cparams(("parallel", "parallel", "arbitrary")),
        name="ssd_core",
    )(*args, *args)


def _ssd_out_kernel(yf_ref, yb_ref, x_ref, z_ref, dsk_ref, g_ref, w_ref, h_ref, o_ref):
    y = yf_ref[...].astype(F32) + yb_ref[...].astype(F32) + x_ref[...].astype(F32) * dsk_ref[...]
    y = y * _silu(z_ref[...].astype(F32))
    gw = SSD_INNER // SSD_GROUPS
    parts = []
    for gg in range(SSD_GROUPS):
        yg = y[:, gg * gw:(gg + 1) * gw]
        ms = jnp.mean(yg * yg, axis=-1, keepdims=True)
        parts.append(yg * lax.rsqrt(ms + EPS))
    a = (jnp.concatenate(parts, axis=1) * g_ref[...]).astype(BF16)
    o_ref[...] = h_ref[...] + _dot(a, w_ref[...])


def ssd_out(y_f, y_b, xbc, zx, d_skip, g_norm, w_out, h, tm=512):
    t, n = h.shape
    wide = pl.BlockSpec((tm, SSD_INNER), lambda i: (i, 0))
    vec = pl.BlockSpec((1, SSD_INNER), lambda i: (0, 0))
    return pl.pallas_call(
        _ssd_out_kernel,
        grid=(t // tm,),
        in_specs=[wide, wide, wide, wide, vec, vec,
                  pl.BlockSpec((SSD_INNER, n), lambda i: (0, 0)),
                  pl.BlockSpec((tm, n), lambda i: (i, 0))],
        out_specs=pl.BlockSpec((tm, n), lambda i: (i, 0)),
        out_shape=jax.ShapeDtypeStruct((t, n), F32),
        compiler_params=_cparams(("parallel",)),
        name="ssd_out",
    )(y_f, y_b, xbc, zx, jnp.repeat(d_skip, SSD_P).reshape(1, SSD_INNER),
      g_norm.reshape(1, SSD_INNER), w_out, h)


def ssd_layer(h, norm_g, w_in, conv_w, conv_b, dt_bias_f, a_log_f, dt_bias_b, a_log_b, d_skip,
              g_norm, w_out, bsz, seq):
    n_main = SSD_INNER + SSD_CONV_DIM
    wdt = jnp.pad(w_in[:, n_main:], ((0, 0), (0, LANES - 2 * SSD_HEADS))).astype(BF16)
    zx, dtr = norm_matmul(h, norm_g, w_in[:, :n_main].astype(BF16), wdt)
    xbc = ssd_conv(zx, conv_w, conv_b, bsz, seq)
    dt_bias = jnp.concatenate([dt_bias_f, dt_bias_b])
    a_neg = -jnp.exp(jnp.concatenate([a_log_f, a_log_b]))
    dt, ac, act = ssd_prep(dtr, dt_bias, a_neg)
    y_f, y_b = ssd_core(xbc, dt, ac, act, bsz, seq)
    return ssd_out(y_f, y_b, xbc, zx, d_skip, g_norm, w_out.astype(BF16), h), None


DIL_TQ = 128


def _dil_kernel(q_ref, kp_ref, kc_ref, kn_ref, vp_ref, vc_ref, vn_ref, o_ref, lse_ref, *, bounded):
    ai = pl.program_id(2)
    na = pl.num_programs(2)
    nres, tqb = q_ref.shape[0], q_ref.shape[1]
    tq = DIL_TQ
    side = DIL_SIDE
    nk = tq + 2 * side
    ii = lax.broadcasted_iota(jnp.int32, (4 * tq, nk), 0) & (tq - 1)
    jj = lax.broadcasted_iota(jnp.int32, (4 * tq, nk), 1)
    band = (jj >= ii) & (jj <= ii + 2 * side)
    lane = lax.broadcasted_iota(jnp.int32, (tq, ROPE_GROUP), 1)
    for rr, sb in ((rr, sb) for rr in range(nres) for sb in range(tqb // tq)):
        kcat = jnp.concatenate([kp_ref[rr], kc_ref[rr], kn_ref[rr]], axis=0)
        vcat = jnp.concatenate([vp_ref[rr], vc_ref[rr], vn_ref[rr]], axis=0)
        qrows = slice(sb * tq, (sb + 1) * tq)
        krows = slice(sb * tq, sb * tq + nk)
        kpos = ai * tqb + sb * tq - side + jj
        valid = band & (kpos >= 0) & (kpos < na * tqb)
        for g4 in range(DIL_HEADS // 4):
            gsl = slice(g4 * ROPE_GROUP, (g4 + 1) * ROPE_GROUP)
            q4 = q_ref[rr, qrows, gsl]
            zq = jnp.zeros_like(q4)
            qst = jnp.concatenate([jnp.where(_head_lane_mask(q4.shape, u), q4, zq) for u in range(4)],
                                  axis=0)
            s = jnp.where(valid, _dot_nt(qst, kcat[krows, gsl]), NEG_BIG)
            if bounded:
                p = jnp.exp2(s)
            else:
                m = jnp.max(s, axis=1, keepdims=True)
                p = jnp.exp2(s - m)
            den = jnp.sum(p, axis=1, keepdims=True)
            r = _dot(p.astype(BF16), vcat[krows, gsl])
            lse = jnp.log(den) if bounded else m * (1.0 / LOG2E) + jnp.log(den)
            out = jnp.zeros((tq, ROPE_GROUP), F32)
            lse4 = jnp.zeros((tq, ROPE_GROUP), F32)
            for u in range(4):
                rows = slice(u * tq, (u + 1) * tq)
                mine = (lane >= u * DIL_HD) & (lane < (u + 1) * DIL_HD)
                out = jnp.where(mine, r[rows] / den[rows], out)
                lse4 = jnp.where(mine, lse[rows], lse4)
            o_ref[rr, qrows, gsl] = out.astype(o_ref.dtype)
            lse_ref[rr, qrows, gsl] = lse4


def dil_core(pg, dilation, score_bound, bsz, seq):
    a = seq // dilation
    side = DIL_SIDE
    tq = min(4 * DIL_TQ, DIL_SUPER // dilation)
    nres = 4 * DIL_TQ // tq
    na = a // tq
    nst = seq // DIL_SUPER
    per = DIL_SUPER // dilation // tq
    per_h = DIL_SUPER // dilation // side
    hs = tq // side
    nhb = a // side

    def cur(col):
        return pl.BlockSpec((None, nres, tq, DIL_WIDTH),
                            lambda b, r, i: (b * nst + i // per, r, i % per, col))

    def halo(col, which):
        def imap(b, r, i):
            hb = (jnp.maximum(hs * i - 1, 0) if which == 0 else jnp.minimum(hs * (i + 1), nhb - 1))
            return (b * nst + hb // per_h, r, hb % per_h, col)
        return pl.BlockSpec((None, nres, side, DIL_WIDTH), imap)

    out_spec = pl.BlockSpec((None, nres, tq, DIL_WIDTH), lambda b, r, i: (b * nst + i // per, r, i % per, 0))
    oshape = (bsz * nst, dilation, DIL_SUPER // dilation, DIL_WIDTH)

    def call(bounded):
        return lambda x: pl.pallas_call(
            functools.partial(_dil_kernel, bounded=bounded),
            grid=(bsz, dilation // nres, na),
            in_specs=[cur(0), halo(1, 0), cur(1), halo(1, 1), halo(2, 0), cur(2), halo(2, 1)],
            out_specs=[out_spec, out_spec],
            out_shape=[jax.ShapeDtypeStruct(oshape, BF16), jax.ShapeDtypeStruct(oshape, F32)],
            compiler_params=_cparams(("parallel", "parallel", "parallel")),
            name=f"dil_core_{dilation}" + ("_bounded" if bounded else "_online"),
        )(x, x, x, x, x, x, x)

    return lax.cond(score_bound <= SAFE_LOG2, call(True), call(False), pg)


def _dil_out_kernel(o0_ref, o1_ref, o2_ref, l0_ref, l1_ref, l2_ref, w_ref, h_ref, o_ref,
                    so_ref, sl_ref):
    tm = h_ref.shape[0]
    nch = DIL_WIDTH // LANES
    for gi, (og_ref, lg_ref) in enumerate(((o1_ref, l1_ref), (o2_ref, l2_ref))):
        d = DIL_PAIRS[gi + 1][1]
        n = tm // d
        for r in range(d):
            ov = og_ref[r].astype(F32)
            lv = lg_ref[r]
            for c in range(nch):
                so_ref[gi, c, pl.ds(r, n, stride=d), :] = ov[:, c * LANES:(c + 1) * LANES]
                sl_ref[gi, c, pl.ds(r, n, stride=d), :] = lv[:, c * LANES:(c + 1) * LANES]
    parts = []
    for c in range(nch):
        sl = slice(c * LANES, (c + 1) * LANES)
        l0 = l0_ref[0, :, sl]
        l1 = sl_ref[0, c]
        l2 = sl_ref[1, c]
        mx = jnp.maximum(jnp.maximum(l0, l1), l2)
        e0 = jnp.exp(l0 - mx)
        e1 = jnp.exp(l1 - mx)
        e2 = jnp.exp(l2 - mx)
        num = e0 * o0_ref[0, :, sl].astype(F32) + e1 * so_ref[0, c] + e2 * so_ref[1, c]
        parts.append((num / (e0 + e1 + e2)).astype(BF16))
    o_ref[...] = h_ref[...] + _dot(jnp.concatenate(parts, axis=1), w_ref[...])


def dil_out(os, lses, w_out, h, tm=512):
    t, n = h.shape
    per = DIL_SUPER // tm

    def grp(d):
        return pl.BlockSpec((None, d, tm // d, DIL_WIDTH), lambda i: (i // per, 0, i % per, 0))

    specs = [grp(d) for _, d in DIL_PAIRS]
    row = pl.BlockSpec((tm, n), lambda i: (i, 0))
    return pl.pallas_call(
        _dil_out_kernel,
        grid=(t // tm,),
        in_specs=specs + specs + [pl.BlockSpec((DIL_WIDTH, n), lambda i: (0, 0)), row],
        out_specs=row,
        out_shape=jax.ShapeDtypeStruct((t, n), F32),
        scratch_shapes=[pltpu.VMEM((2, DIL_WIDTH // LANES, tm, LANES), F32),
                        pltpu.VMEM((2, DIL_WIDTH // LANES, tm, LANES), F32)],
        compiler_params=_cparams(("parallel",)),
        name="dil_out",
    )(*os, *lses, w_out, h)


def dil_layer(h, norm_g, w_in, g_q, g_k, w_out, bsz, seq):
    qscale = (DIL_HD ** -0.5) * LOG2E
    os, lses = [], []
    tables = _rope_tables(seq)
    ngrp = len(DIL_PAIRS)
    for gi, (_, dilation) in enumerate(DIL_PAIRS):
        cols = ((jnp.arange(3)[:, None] * ngrp + gi) * DIL_WIDTH + jnp.arange(DIL_WIDTH)[None, :]).reshape(-1)
        rope_gain = jnp.concatenate([jnp.tile(g_q[gi], DIL_HEADS) * qscale, jnp.tile(g_k[gi], DIL_HEADS)])
        pg = norm_matmul_rope(h, norm_g, w_in, cols, rope_gain, tables, seq, dil=dilation)
        o, lse = dil_core(pg, dilation, _score_bound(g_q[gi], g_k[gi], qscale), bsz, seq)
        os.append(o)
        lses.append(lse)
    return dil_out(os, lses, w_out.astype(BF16), h), None


def kernel(x, p, norm_mix, norm_ffn, ffn_w_in, ffn_w_out, ple_norm, ple_w_gate, ple_w_proj, gla_w_in, gla_w_gate_f, gla_b_gate_f, gla_w_gate_b, gla_b_gate_b, gla_g_out, gla_w_out, diff_w_in, diff_g_q, diff_g_k, diff_lam_q1, diff_lam_k1, diff_lam_q2, diff_lam_k2, diff_g_sub, diff_w_out, ssd_w_in, ssd_conv_w, ssd_conv_b, ssd_dt_bias_f, ssd_a_log_f, ssd_dt_bias_b, ssd_a_log_b, ssd_d, ssd_g_norm, ssd_w_out, dil_w_in, dil_g_q, dil_g_k, dil_w_out):
    bsz, seq, dm = x.shape
    t = bsz * seq
    h = x.reshape(t, dm)
    for i in range(DEPTH):
        kind = i % 4
        j = i // 4
        if kind == 0:
            h, mixer = gla_layer(h, norm_mix[i], gla_w_in[j], gla_w_gate_f[j], gla_b_gate_f[j],
                                 gla_w_gate_b[j], gla_b_gate_b[j], gla_g_out[j], gla_w_out[j], bsz, seq)
        elif kind == 1:
            h, mixer = diff_layer(h, norm_mix[i], i, diff_w_in[j], diff_g_q[j], diff_g_k[j],
                                  diff_lam_q1[j], diff_lam_k1[j], diff_lam_q2[j], diff_lam_k2[j],
                                  diff_g_sub[j], diff_w_out[j], bsz, seq)
        elif kind == 2:
            h, mixer = ssd_layer(h, norm_mix[i], ssd_w_in[j], ssd_conv_w[j], ssd_conv_b[j],
                                 ssd_dt_bias_f[j], ssd_a_log_f[j], ssd_dt_bias_b[j], ssd_a_log_b[j],
                                 ssd_d[j], ssd_g_norm[j], ssd_w_out[j], bsz, seq)
        else:
            h, mixer = dil_layer(h, norm_mix[i], dil_w_in[j], dil_g_q[j], dil_g_k[j], dil_w_out[j],
                                 bsz, seq)
        ffn_ple_params = (norm_ffn[i], ffn_w_in[i].astype(BF16), ffn_w_out[i].astype(BF16), ple_norm[i],
                          ple_w_gate[i].astype(BF16), p[i].reshape(t, PLE_DIM), ple_w_proj[i].astype(BF16))
        h = ffn_ple_residual(h, ffn_ple_params, mixer)
    return h.reshape(bsz, seq, dm)
```

```python
import functools
import math

import jax
import jax.numpy as jnp
from jax import lax
from jax.experimental import pallas as pl
from jax.experimental.pallas import tpu as pltpu

F32 = jnp.float32
BF16 = jnp.bfloat16

D_MODEL = 1024
DEPTH = 4
PLE_DIM = 256
ROPE_THETA = 10000.0
EPS = 1e-6
D_FF = 2816
LOG2E = math.log2(math.e)
NEG_BIG = -1e30

GLA_HEADS = 4
GLA_DK = 512
GLA_DV = 1024
GLA_HK = 128
GLA_HV = 256
GLA_RANK = 16
GLA_TAU = 16.0
GLA_CHUNK = 64

DIFF_HEADS = 8
DIFF_HD = 64

SSD_INNER = 2048
SSD_P = 64
SSD_HEADS = 32
SSD_GROUPS = 8
SSD_N = 128
SSD_CONV = 5
SSD_CHUNK = 128
SSD_CONV_DIM = 4096

DIL_PAIRS = ((128, 1), (512, 4), (2048, 16))
DIL_HEADS = 16
DIL_HD = 64
DIL_WIDTH = 1024
DIL_SIDE = 64
DIL_SUPER = 2048

LANES = 128
F32_ROWS = 8
BF16_ROWS = 16
ROPE_GROUP = 256
SAFE_LOG2 = 60.0


def _cparams(sem):
    return pltpu.CompilerParams(dimension_semantics=sem)


def _dot(a, b):
    return jnp.dot(a, b, preferred_element_type=F32)


def _dot_nt(a, b):
    return lax.dot_general(a, b, (((1,), (1,)), ((), ())), preferred_element_type=F32)


def _dot_tn(a, b):
    return lax.dot_general(a, b, (((0,), (0,)), ((), ())), preferred_element_type=F32)


def _split(x):
    hi = x.astype(BF16)
    return hi, (x - hi.astype(F32)).astype(BF16)


def _dot_split(a_bf16, x):
    hi, lo = _split(x)
    return _dot(a_bf16, hi) + _dot(a_bf16, lo)


def _rms_rows(x, g):
    ms = jnp.mean(x * x, axis=-1, keepdims=True)
    return x * lax.rsqrt(ms + EPS) * g


def _silu(x):
    return x * jax.nn.sigmoid(x)


def _softplus(x):
    return jnp.maximum(x, 0.0) + jnp.log(1.0 + jnp.exp(-jnp.abs(x)))


def _tri(c, upper=False):
    ii = lax.broadcasted_iota(jnp.int32, (c, c), 0)
    jj = lax.broadcasted_iota(jnp.int32, (c, c), 1)
    return (jj >= ii) if upper else (jj <= ii)


def _resident(shape):
    return pl.BlockSpec(shape, lambda i: (0,) * len(shape), pipeline_mode=pl.Buffered(1))


PROJ_COLS = 1024


def _nm_kernel(x_ref, g_ref, w_ref, ws_ref, o_ref, os_ref):
    xn = _rms_rows(x_ref[...], g_ref[...]).astype(BF16)
    for c in range(w_ref.shape[1] // PROJ_COLS):
        sl = slice(c * PROJ_COLS, (c + 1) * PROJ_COLS)
        o_ref[:, sl] = _dot(xn, w_ref[:, sl]).astype(o_ref.dtype)
    os_ref[...] = _dot(xn, ws_ref[...])


def norm_matmul(x, g, w, w_side, tm=512):
    t, k = x.shape
    n = w.shape[1]
    return pl.pallas_call(
        _nm_kernel,
        grid=(t // tm,),
        in_specs=[pl.BlockSpec((tm, k), lambda i: (i, 0)),
                  _resident((1, k)), _resident((k, n)), _resident((k, LANES))],
        out_specs=[pl.BlockSpec((tm, n), lambda i: (i, 0)), pl.BlockSpec((tm, LANES), lambda i: (i, 0))],
        out_shape=[jax.ShapeDtypeStruct((t, n), BF16), jax.ShapeDtypeStruct((t, LANES), F32)],
        compiler_params=_cparams(("parallel",)),
        name="norm_matmul",
    )(x, g.reshape(1, k), w, w_side)


def _rope_tables(seq):
    half = DIFF_HD // 2
    inv = ROPE_THETA ** (-jnp.arange(half, dtype=F32) * 2.0 / DIFF_HD)
    ang = jnp.arange(seq, dtype=jnp.int32).astype(F32)[:, None] * inv[None, :]
    return jnp.tile(jnp.cos(ang), (1, 4)), jnp.tile(jnp.sin(ang), (1, 4))


def _rope_col_perm(n_cols):
    half = DIFF_HD // 2
    g = jnp.arange(n_cols // ROPE_GROUP)[:, None, None, None] * ROPE_GROUP
    part = jnp.arange(2)[None, :, None, None] * half
    head = jnp.arange(4)[None, None, :, None] * DIFF_HD
    d = jnp.arange(half)[None, None, None, :]
    return (g + head + part + d).reshape(-1)


def _seg32_ones():
    r = jnp.arange(LANES)
    return (r[:, None] // 32 == r[None, :] // 32).astype(BF16)


def _head_lane_mask(shape, u):
    lane = lax.broadcasted_iota(jnp.int32, shape, len(shape) - 1)
    return (lane % LANES) // (DIFF_HD // 2) == u


def _nm_rope_kernel(x_ref, g_ref, w_ref, hg_ref, cos_ref, sin_ref, bd_ref, o_ref, xn_ref, *xs_refs,
                    n_rope_cols, dil):
    tm = x_ref.shape[0]
    n = tm // dil
    if dil == 1:
        xn_ref[...] = _rms_rows(x_ref[...], g_ref[...]).astype(BF16)
    else:
        xs_ref = xs_refs[0]
        nch = x_ref.shape[1] // LANES
        for c in range(nch):
            xs_ref[c] = x_ref[:, c * LANES:(c + 1) * LANES]
        for r in range(dil):
            xr = jnp.concatenate([xs_ref[c, pl.ds(r, n, stride=dil), :] for c in range(nch)], axis=1)
            xn_ref[r * n:(r + 1) * n, :] = _rms_rows(xr, g_ref[...]).astype(BF16)

    def put(sl, val):
        if len(o_ref.shape) == 2:
            o_ref[:, sl] = val
        else:
            for r in range(dil):
                o_ref[r, :, sl] = val[r * n:(r + 1) * n]

    cos = cos_ref[...]
    sin = sin_ref[...]
    bd = bd_ref[...]
    xn = xn_ref[...]
    wide = 2 * ROPE_GROUP
    for cw in range(n_rope_cols // wide):
        acc2 = _dot(xn, w_ref[:, cw * wide:(cw + 1) * wide])
        for c2 in range(2):
            c = 2 * cw + c2
            sl = slice(c * ROPE_GROUP, (c + 1) * ROPE_GROUP)
            a = acc2[:, c2 * ROPE_GROUP:c2 * ROPE_GROUP + LANES]
            b = acc2[:, c2 * ROPE_GROUP + LANES:(c2 + 1) * ROPE_GROUP]
            ss = _dot((a * a + b * b).astype(BF16), bd)
            inv = lax.rsqrt(ss * (1.0 / DIFF_HD) + EPS)
            a = a * inv * hg_ref[:, c * ROPE_GROUP:c * ROPE_GROUP + LANES]
            b = b * inv * hg_ref[:, c * ROPE_GROUP + LANES:(c + 1) * ROPE_GROUP]
            put(sl, jnp.concatenate([a * cos - b * sin, b * cos + a * sin], axis=1).astype(o_ref.dtype))
    for c in range(n_rope_cols // PROJ_COLS, w_ref.shape[1] // PROJ_COLS):
        sl = slice(c * PROJ_COLS, (c + 1) * PROJ_COLS)
        put(sl, _dot(xn, w_ref[:, sl]).astype(o_ref.dtype))


def norm_matmul_rope(x, g, w_full, cols, rope_gain, tables, seq, dil=None, tm=512):
    t, k = x.shape
    n = cols.shape[0]
    n_rope_cols = rope_gain.shape[0]
    perm = _rope_col_perm(n_rope_cols)
    w = w_full[:, jnp.concatenate([cols[:n_rope_cols][perm], cols[n_rope_cols:]])].astype(BF16)
    rope_gain = rope_gain[perm]
    cos, sin = tables
    nsb = seq // tm
    d = 1 if dil is None else dil
    if d > 1:
        cos, sin = (tb.reshape(nsb, tm // d, d, LANES).transpose(0, 2, 1, 3).reshape(seq, LANES)
                    for tb in (cos, sin))
    kern = functools.partial(_nm_rope_kernel, n_rope_cols=n_rope_cols, dil=d)
    if dil is None:
        out_spec = pl.BlockSpec((tm, n), lambda i: (i, 0))
        out_shape = jax.ShapeDtypeStruct((t, n), BF16)
    else:
        per = DIL_SUPER // tm
        out_spec = pl.BlockSpec((None, d, tm // d, n), lambda i: (i // per, 0, i % per, 0))
        out_shape = jax.ShapeDtypeStruct((t // DIL_SUPER, d, DIL_SUPER // d, n), BF16)
    scratch = [pltpu.VMEM((tm, k), BF16)]
    if d > 1:
        scratch.append(pltpu.VMEM((k // LANES, tm, LANES), F32))
    return pl.pallas_call(
        kern,
        grid=(t // tm,),
        in_specs=[pl.BlockSpec((tm, k), lambda i: (i, 0)),
                  _resident((1, k)), _resident((k, n)), _resident((1, n_rope_cols)),
                  pl.BlockSpec((tm, LANES), lambda i: (i % nsb, 0)),
                  pl.BlockSpec((tm, LANES), lambda i: (i % nsb, 0)),
                  _resident((LANES, LANES))],
        out_specs=out_spec,
        out_shape=out_shape,
        scratch_shapes=scratch,
        compiler_params=_cparams(("parallel",)),
        name="norm_matmul_rope",
    )(x, g.reshape(1, k), w, rope_gain.reshape(1, n_rope_cols), cos, sin, _seg32_ones())


def _score_bound(g_q, g_k, qscale):
    return 1.05 * DIFF_HD * qscale * jnp.max(jnp.abs(g_q)) * jnp.max(jnp.abs(g_k))


FFN_SPLIT = 1536


def _ffn_ple_kernel(*refs, mixer_act):
    x_ref, gf_ref, win_ref, wout_ref, gp_ref, wg_ref, p_ref, wp_ref, o_ref = refs[-9:]
    x = x_ref[...]
    if mixer_act is not None:
        x = x + _dot(mixer_act(*refs[:-10]), refs[-10][...])
    xn = _rms_rows(x, gf_ref[...]).astype(BF16)
    h2 = x
    for lo, hi in ((0, FFN_SPLIT), (FFN_SPLIT, D_FF)):
        gate = _dot(xn, win_ref[:, lo:hi])
        up = _dot(xn, win_ref[:, D_FF + lo:D_FF + hi])
        a = (_silu(gate) * up).astype(BF16)
        h2 = h2 + _dot(a, wout_ref[lo:hi, :])
    hn = _rms_rows(h2, gp_ref[...]).astype(BF16)
    gate2 = jax.nn.sigmoid(_dot(hn, wg_ref[...]))
    o_ref[...] = h2 + gate2 * _dot(p_ref[...].astype(BF16), wp_ref[...])


FFN_ROWS = 512


def ffn_ple_residual(h, ffn_ple_params, mixer=None):
    g_ffn, w_in, w_out, g_ple, w_gate, p, w_proj = ffn_ple_params
    t, k = h.shape
    tm = FFN_ROWS
    act, m_ops, m_specs = None, (), []
    if mixer is not None:
        act, m_ops, m_specs, w_mix = mixer
        m_ops = tuple(m_ops) + (w_mix,)
        m_specs = list(m_specs) + [_resident(w_mix.shape)]
    return pl.pallas_call(
        functools.partial(_ffn_ple_kernel, mixer_act=act),
        grid=(t // tm,),
        in_specs=m_specs + [pl.BlockSpec((tm, k), lambda i: (i, 0)),
                            _resident((1, k)),
                            _resident((k, 2 * D_FF)),
                            _resident((D_FF, k)),
                            _resident((1, k)),
                            _resident((k, k)),
                            pl.BlockSpec((tm, PLE_DIM), lambda i: (i, 0)),
                            _resident((PLE_DIM, k))],
        out_specs=pl.BlockSpec((tm, k), lambda i: (i, 0)),
        out_shape=jax.ShapeDtypeStruct((t, k), F32),
        compiler_params=_cparams(("parallel",)),
        name="ffn_ple",
    )(*m_ops, h, g_ffn.reshape(1, k), w_in, w_out, g_ple.reshape(1, k), w_gate, p, w_proj)


def _gla_kernel(qf_ref, kf_ref, vf_ref, zf_ref, qb_ref, kb_ref, vb_ref, zb_ref, wg_ref, bg_ref,
                of_ref, ob_ref, st_ref, la_ref, *, nchunk):
    @pl.when(pl.program_id(1) == 0)
    def _():
        st_ref[...] = jnp.zeros_like(st_ref)

    c = GLA_CHUNK
    for dirn, z_ref in enumerate((zf_ref, zb_ref)):
        x = _dot(z_ref[...].astype(BF16), wg_ref[dirn]) + bg_ref[dirn]
        la_ref[dirn] = (jnp.minimum(x, 0.0) - jnp.log(1.0 + jnp.exp(-jnp.abs(x)))) * (1.0 / GLA_TAU)
    cms = (_tri(c).astype(BF16), _tri(c, upper=True).astype(BF16))
    mks = (_tri(c), jnp.logical_not(_tri(c)))

    def chunk(dirn, hh, ci, q_ref, k_ref, v_ref, o_ref):
        rs = slice(ci * c, (ci + 1) * c)
        ks = slice(hh * GLA_HK, (hh + 1) * GLA_HK)
        vs = slice(hh * GLA_HV, (hh + 1) * GLA_HV)
        la = la_ref[dirn, rs, ks]
        b = _dot_split(cms[dirn], la)
        tot = jnp.sum(la, axis=0, keepdims=True)
        qc = q_ref[rs, ks].astype(F32)
        kc = k_ref[rs, ks].astype(F32)
        vc = v_ref[rs, vs]
        q_in = (qc * jnp.exp(b) * (GLA_HK ** -0.5)).astype(BF16)
        k_out = (kc * jnp.exp(-b)).astype(BF16)
        k_dec = (kc * jnp.exp(tot - b)).astype(BF16)
        sc = jnp.where(mks[dirn], _dot_nt(q_in, k_out), 0.0).astype(BF16)
        st = st_ref[dirn, hh]
        o = _dot(sc, vc) + _dot_nt(q_in, st.astype(BF16))
        o_ref[rs, vs] = o.astype(o_ref.dtype)
        st_ref[dirn, hh] = st * jnp.exp(tot) + _dot_tn(vc, k_dec)

    for cc in range(nchunk):
        for hh in range(GLA_HEADS):
            chunk(0, hh, cc, qf_ref, kf_ref, vf_ref, of_ref)
            chunk(1, hh, nchunk - 1 - cc, qb_ref, kb_ref, vb_ref, ob_ref)


def gla_core(proj, z, wg, bg, bsz, seq, rows=512):
    t = bsz * seq
    nblk = seq // rows

    def side(bwd):
        def rb(b, i):
            return b * nblk + (nblk - 1 - i if bwd else i)
        return [pl.BlockSpec((rows, GLA_DK), lambda b, i: (rb(b, i), 0)),
                pl.BlockSpec((rows, GLA_DK), lambda b, i: (rb(b, i), 1)),
                pl.BlockSpec((rows, GLA_DV), lambda b, i: (rb(b, i), 1)),
                pl.BlockSpec((rows, LANES), lambda b, i: (rb(b, i), 0))]

    out_f = pl.BlockSpec((rows, GLA_DV), lambda b, i: (b * nblk + i, 0))
    out_b = pl.BlockSpec((rows, GLA_DV), lambda b, i: (b * nblk + nblk - 1 - i, 0))
    oshape = jax.ShapeDtypeStruct((t, GLA_DV), BF16)
    kern = functools.partial(_gla_kernel, nchunk=rows // GLA_CHUNK)
    return pl.pallas_call(
        kern,
        grid=(bsz, nblk),
        in_specs=side(False) + side(True) + [pl.BlockSpec((2, LANES, GLA_DK), lambda b, i: (0, 0, 0)),
                                             pl.BlockSpec((2, 1, GLA_DK), lambda b, i: (0, 0, 0))],
        out_specs=[out_f, out_b],
        out_shape=[oshape, oshape],
        scratch_shapes=[pltpu.VMEM((2, GLA_HEADS, GLA_HV, GLA_HK), F32),
                        pltpu.VMEM((2, rows, GLA_DK), F32)],
        compiler_params=_cparams(("parallel", "arbitrary")),
        name="gla_core",
    )(proj, proj, proj, z, proj, proj, proj, z, wg, bg)


def _gla_act(of_ref, ob_ref, r_ref, g_ref):
    o = of_ref[...].astype(F32) + ob_ref[...].astype(F32)
    parts = []
    for hh in range(GLA_HEADS):
        oh = o[:, hh * GLA_HV:(hh + 1) * GLA_HV]
        ms = jnp.mean(oh * oh, axis=-1, keepdims=True)
        parts.append(oh * lax.rsqrt(ms + EPS))
    on = jnp.concatenate(parts, axis=1) * g_ref[...]
    return (on * _silu(r_ref[...].astype(F32))).astype(BF16)


def gla_mixer_out(o_f, o_b, proj, g_out, w_out):
    row = pl.BlockSpec((FFN_ROWS, GLA_DV), lambda i: (i, 0))
    specs = [row, row, pl.BlockSpec((FFN_ROWS, GLA_DV), lambda i: (i, 2)), _resident((1, GLA_DV))]
    return _gla_act, (o_f, o_b, proj, jnp.tile(g_out, GLA_HEADS).reshape(1, GLA_DV)), specs, w_out


def gla_layer(h, norm_g, w_in, w_gate_f, b_gate_f, w_gate_b, b_gate_b, g_out, w_out, bsz, seq):
    n_main = 2 * GLA_DK + 2 * GLA_DV
    wz = jnp.pad(w_in[:, n_main:], ((0, 0), (0, LANES - 2 * GLA_RANK))).astype(BF16)
    proj, z = norm_matmul(h, norm_g, w_in[:, :n_main].astype(BF16), wz)
    wg = jnp.zeros((2, LANES, GLA_DK), F32)
    wg = wg.at[0, :GLA_RANK].set(w_gate_f).at[1, GLA_RANK:2 * GLA_RANK].set(w_gate_b).astype(BF16)
    bg = jnp.stack([b_gate_f, b_gate_b]).reshape(2, 1, GLA_DK)
    o_f, o_b = gla_core(proj, z, wg, bg, bsz, seq)
    return h, gla_mixer_out(o_f, o_b, proj, g_out, w_out.astype(BF16))


def _diff_kernel(q_ref, k_ref, v0_ref, v1_ref, lam_ref, gs_ref, o_ref, qz_ref, acc_ref, *m_refs,
                 tk, lambda_init, bounded):
    tq = q_ref.shape[0]
    q = q_ref[...]
    zero = jnp.zeros_like(q)
    for u in range(4):
        qz_ref[u] = jnp.where(_head_lane_mask(q.shape, u), q, zero)
    acc_ref[...] = jnp.zeros_like(acc_ref)
    if not bounded:
        m_ref = m_refs[0]
        m_ref[...] = jnp.full_like(m_ref, NEG_BIG)
    ones = jnp.ones((tk, LANES), BF16)

    def body(ki, carry):
        rows = pl.ds(pl.multiple_of(ki * tk, tk), tk)
        kb = k_ref[rows, :]
        vaug = [jnp.concatenate([v_ref[rows, :], ones], axis=1) for v_ref in (v0_ref, v1_ref)]
        for u in range(4):
            s = _dot_nt(qz_ref[u], kb)
            if bounded:
                acc_ref[u] += _dot(jnp.exp2(s).astype(BF16), vaug[u // 2])
            else:
                m_prev = m_ref[u]
                m_new = jnp.maximum(m_prev, jnp.max(s, axis=1, keepdims=True))
                alpha = jnp.exp2(m_prev - m_new)
                p = jnp.exp2(s - jnp.tile(m_new, (1, tk // LANES)))
                acc_ref[u] = jnp.tile(alpha, (1, 2)) * acc_ref[u] + _dot(p.astype(BF16), vaug[u // 2])
                m_ref[u] = m_new
        return carry

    lax.fori_loop(0, k_ref.shape[0] // tk, body, 0, unroll=4 if bounded else 1)

    lam4 = lam_ref[...]
    lam = (jnp.exp(jnp.sum(lam4[0:1] * lam4[1:2], axis=1, keepdims=True))
           - jnp.exp(jnp.sum(lam4[2:3] * lam4[3:4], axis=1, keepdims=True)) + lambda_init)
    for hh in range(2):
        a1 = acc_ref[2 * hh]
        a2 = acc_ref[2 * hh + 1]
        o = a1[:, :LANES] / a1[:, LANES:] - lam * (a2[:, :LANES] / a2[:, LANES:])
        ms = jnp.mean(o * o, axis=-1, keepdims=True)
        o_ref[:, hh * LANES:(hh + 1) * LANES] = (
            o * lax.rsqrt(ms + EPS) * gs_ref[...] * (1.0 - lambda_init)).astype(o_ref.dtype)


def diff_core(proj, lam4, g_sub, lambda_init, score_bound, bsz, seq, tq=1024, tk=512):
    t = bsz * seq
    nq = seq // tq
    npair = DIFF_HEADS // 2
    in_specs = [pl.BlockSpec((tq, ROPE_GROUP), lambda b, h, i: (b * nq + i, h)),
                pl.BlockSpec((seq, ROPE_GROUP), lambda b, h, i: (b, npair + h)),
                pl.BlockSpec((seq, LANES), lambda b, h, i: (b, 16 + 2 * h)),
                pl.BlockSpec((seq, LANES), lambda b, h, i: (b, 17 + 2 * h)),
                pl.BlockSpec((4, DIFF_HD), lambda b, h, i: (0, 0)),
                pl.BlockSpec((1, LANES), lambda b, h, i: (0, 0))]
    out_spec = pl.BlockSpec((tq, ROPE_GROUP), lambda b, h, i: (b * nq + i, h))
    out_shape = jax.ShapeDtypeStruct((t, D_MODEL), BF16)
    args = (proj, proj, proj, proj, lam4, g_sub.reshape(1, LANES))

    def call(bounded):
        scratch = [pltpu.VMEM((4, tq, ROPE_GROUP), BF16), pltpu.VMEM((4, tq, 2 * LANES), F32)]
        if not bounded:
            scratch.append(pltpu.VMEM((4, tq, LANES), F32))
        return lambda *a: pl.pallas_call(
            functools.partial(_diff_kernel, tk=tk, lambda_init=lambda_init, bounded=bounded),
            grid=(bsz, npair, nq), in_specs=in_specs, out_specs=out_spec, out_shape=out_shape,
            scratch_shapes=scratch,
            compiler_params=_cparams(("parallel", "parallel", "arbitrary")),
            name="diff_core_bounded" if bounded else "diff_core_online",
        )(*a)

    return lax.cond(score_bound <= SAFE_LOG2, call(True), call(False), *args)


def diff_layer(h, norm_g, layer_idx, w_in, g_q, g_k, lam_q1, lam_k1, lam_q2, lam_k2, g_sub, w_out,
               bsz, seq):
    lambda_init = 0.8 - 0.6 * math.exp(-0.3 * layer_idx)
    qscale = (DIFF_HD ** -0.5) * LOG2E
    rope_gain = jnp.concatenate([jnp.tile(g_q.reshape(-1), DIFF_HEADS) * qscale,
                                 jnp.tile(g_k.reshape(-1), DIFF_HEADS)])
    proj = norm_matmul_rope(h, norm_g, w_in, jnp.arange(3 * D_MODEL), rope_gain, _rope_tables(seq), seq)
    lam4 = jnp.stack([lam_q1, lam_k1, lam_q2, lam_k2])
    o = diff_core(proj, lam4, g_sub, lambda_init, _score_bound(g_q, g_k, qscale), bsz, seq)
    row = pl.BlockSpec((FFN_ROWS, D_MODEL), lambda i: (i, 0))
    return h, (lambda o_ref: o_ref[...], (o,), [row], w_out.astype(BF16))


CONV_BLOCK = 128


def _conv_kernel(prev_ref, cur_ref, next_ref, sh_ref, w_ref, b_ref, o_ref, buf_ref, *, halo):
    i = pl.program_id(1)
    ts = cur_ref.shape[0]
    zero = jnp.zeros(prev_ref.shape, prev_ref.dtype)
    buf_ref[0:halo, :] = jnp.where(i == 0, zero, prev_ref[...])
    buf_ref[halo:halo + ts, :] = cur_ref[...]
    buf_ref[halo + ts:, :] = jnp.where(i == pl.num_programs(1) - 1, zero, next_ref[...])
    w = w_ref[...]
    sh = sh_ref[...]
    cb = CONV_BLOCK
    for r in range(ts // cb):
        win = buf_ref[r * cb:(r + 1) * cb + 2 * halo, :]
        sft = _dot(sh, win)
        y = b_ref[...] + w[2:3] * cur_ref[r * cb:(r + 1) * cb, :].astype(F32)
        for slot, k in enumerate((0, 1, 3, 4)):
            y = y + w[k:k + 1] * sft[slot * cb:(slot + 1) * cb]
        o_ref[r * cb:(r + 1) * cb, :] = _silu(y).astype(o_ref.dtype)


def ssd_conv(zx, conv_w, conv_b, bsz, seq, ts=1024, tc=1024):
    t = bsz * seq
    halo = BF16_ROWS
    nrb = seq // ts
    hb = ts // halo
    nhb = seq // halo
    off = SSD_INNER // tc
    kern = functools.partial(_conv_kernel, halo=halo)
    wpad = jnp.pad(conv_w, ((0, F32_ROWS - SSD_CONV), (0, 0)))
    cb = CONV_BLOCK
    shift = jnp.array([-2, -1, 1, 2])[:, None, None]
    sh = (jnp.arange(cb + 2 * halo)[None, None, :] == jnp.arange(cb)[None, :, None] + halo + shift)
    sh = sh.reshape(4 * cb, cb + 2 * halo).astype(BF16)
    return pl.pallas_call(
        kern,
        grid=(bsz, nrb, SSD_CONV_DIM // tc),
        in_specs=[pl.BlockSpec((halo, tc), lambda b, i, c: (b * nhb + jnp.maximum(i * hb - 1, 0), off + c)),
                  pl.BlockSpec((ts, tc), lambda b, i, c: (b * nrb + i, off + c)),
                  pl.BlockSpec((halo, tc), lambda b, i, c: (b * nhb + jnp.minimum((i + 1) * hb, nhb - 1), off + c)),
                  pl.BlockSpec((4 * cb, cb + 2 * halo), lambda b, i, c: (0, 0)),
                  pl.BlockSpec((F32_ROWS, tc), lambda b, i, c: (0, c)),
                  pl.BlockSpec((1, tc), lambda b, i, c: (0, c))],
        out_specs=pl.BlockSpec((ts, tc), lambda b, i, c: (b * nrb + i, c)),
        out_shape=jax.ShapeDtypeStruct((t, SSD_CONV_DIM), BF16),
        scratch_shapes=[pltpu.VMEM((ts + 2 * halo, tc), BF16)],
        compiler_params=_cparams(("parallel", "parallel", "parallel")),
        name="ssd_conv",
    )(zx, zx, zx, sh, wpad, conv_b.reshape(1, -1))


def _ssd_prep_kernel(dtr_ref, bias_ref, aneg_ref, dt_ref, ac_ref, at_ref):
    c = SSD_CHUNK
    lower = _tri(c).astype(BF16)
    upper = _tri(c, upper=True).astype(BF16)
    lane = lax.broadcasted_iota(jnp.int32, (c, LANES), 1)
    for cc in range(dtr_ref.shape[0] // c):
        rs = slice(cc * c, (cc + 1) * c)
        dt = _softplus(dtr_ref[rs, :] + bias_ref[...])
        adt = dt * aneg_ref[...]
        hi, lo = _split(adt)
        ac = jnp.where(lane < SSD_HEADS, _dot(lower, hi) + _dot(lower, lo), _dot(upper, hi) + _dot(upper, lo))
        dt_ref[rs, :] = dt
        ac_ref[rs, :] = ac
        act = ac.T
        for k in range(2 * SSD_GROUPS):
            at_ref[cc, k] = act[4 * k:4 * k + 4, :]


def ssd_prep(dtr, dt_bias, a_neg, rows=512):
    t = dtr.shape[0]
    c = SSD_CHUNK
    bias = jnp.pad(dt_bias, (0, LANES - 2 * SSD_HEADS)).reshape(1, LANES)
    aneg = jnp.pad(a_neg, (0, LANES - 2 * SSD_HEADS)).reshape(1, LANES)
    row = pl.BlockSpec((rows, LANES), lambda i: (i, 0))
    vec = pl.BlockSpec((1, LANES), lambda i: (0, 0))
    return pl.pallas_call(
        _ssd_prep_kernel,
        grid=(t // rows,),
        in_specs=[row, vec, vec],
        out_specs=[row, row, pl.BlockSpec((rows // c, 2 * SSD_GROUPS, 4, c), lambda i: (i, 0, 0, 0))],
        out_shape=[jax.ShapeDtypeStruct((t, LANES), F32), jax.ShapeDtypeStruct((t, LANES), F32),
                   jax.ShapeDtypeStruct((t // c, 2 * SSD_GROUPS, 4, c), F32)],
        compiler_params=_cparams(("parallel",)),
        name="ssd_prep",
    )(dtr, bias, aneg)


def _ssd_kernel(xf_ref, bf_ref, cf_ref, dtf_ref, acf_ref, atf_ref, e5f_ref,
                xb_ref, bb_ref, cb_ref, dtb_ref, acb_ref, atb_ref, e5b_ref,
                yf_ref, yb_ref, st_ref, *, nchunk):
    @pl.when(pl.program_id(2) == 0)
    def _():
        st_ref[...] = jnp.zeros_like(st_ref)

    c = SSD_CHUNK
    masks = (_tri(c), jnp.logical_not(_tri(c)))
    lane2 = lax.broadcasted_iota(jnp.int32, (c, 4 * SSD_P), 1)
    head_masks = [(lane2 >= hh * SSD_P) & (lane2 < (hh + 1) * SSD_P) for hh in range(4)]

    def narrow(per_head, r):
        lo = lax.broadcasted_iota(jnp.int32, (r, LANES), 1) < SSD_P
        return jnp.concatenate([jnp.where(lo, per_head[0], per_head[1]),
                                jnp.where(lo, per_head[2], per_head[3])], axis=1)

    lane1 = lax.broadcasted_iota(jnp.int32, (c, LANES), 1)
    group = pl.program_id(1)

    def chunk(dirn, ci, x_ref, b_ref, c_ref, dt_ref, ac_ref, at_ref, e5_ref, y_ref):
        rs = slice(ci * c, (ci + 1) * c)
        dt5 = _dot(dt_ref[rs, :].astype(BF16), e5_ref[...])
        ac = ac_ref[rs, :]
        first = SSD_HEADS * dirn + 4 * group
        cols = [jnp.sum(jnp.where(lane1 == first + hh, ac, 0.0), axis=1, keepdims=True)
                for hh in range(4)]
        acr = at_ref[ci]
        tots = [col[c - 1:c] if dirn == 0 else col[0:1] for col in cols]
        dt2 = narrow([dt5[:, hh * LANES:(hh + 1) * LANES] for hh in range(4)], c)
        ac2 = narrow(cols, c)
        tot2 = narrow(tots, 1)
        x = x_ref[rs, :].astype(F32)
        bm = b_ref[rs, :]
        cmat = c_ref[rs, :]
        xdt = x * dt2
        cbm = jnp.where(masks[dirn], _dot_nt(cmat, bm), 0.0)
        st = st_ref[dirn]
        ms, xs = [], []
        for hh in range(4):
            col = cols[hh]
            row = acr[hh:hh + 1, :]
            ms.append((cbm * jnp.exp(jnp.minimum(col - row, 0.0))).astype(BF16))
            xs.append(jnp.where(head_masks[hh], xdt, 0.0).astype(BF16))
        y = (_dot(cmat, st.astype(BF16)) * jnp.exp(ac2)
             + _dot(jnp.concatenate(ms, axis=1), jnp.concatenate(xs, axis=0)))
        y_ref[rs, :] = y.astype(y_ref.dtype)
        xw = (xdt * jnp.exp(tot2 - ac2)).astype(BF16)
        st_ref[dirn] = st * jnp.exp(tot2) + _dot_tn(bm, xw)

    for cc in range(nchunk):
        chunk(0, cc, xf_ref, bf_ref, cf_ref, dtf_ref, acf_ref, atf_ref, e5f_ref, yf_ref)
        chunk(1, nchunk - 1 - cc, xb_ref, bb_ref, cb_ref, dtb_ref, acb_ref, atb_ref, e5b_ref, yb_ref)


def ssd_core(xbc, dt, ac, act, bsz, seq, rows=1024):
    t = bsz * seq
    nblk = seq // rows
    c = SSD_CHUNK
    hp = 4 * SSD_P
    src = 4 * jnp.arange(2 * SSD_GROUPS)[:, None] + jnp.arange(4)[None, :]
    onehot = (jnp.arange(LANES)[None, None, :] == src[..., None]).astype(BF16)
    e5 = jnp.repeat(onehot, LANES, axis=1).transpose(0, 2, 1)

    def side(bwd):
        def rb(b, i):
            return b * nblk + (nblk - 1 - i if bwd else i)
        d = int(bwd)
        return [pl.BlockSpec((rows, hp), lambda b, g, i: (rb(b, i), g)),
                pl.BlockSpec((rows, SSD_N), lambda b, g, i: (rb(b, i), 16 + g)),
                pl.BlockSpec((rows, SSD_N), lambda b, g, i: (rb(b, i), 24 + g)),
                pl.BlockSpec((rows, LANES), lambda b, g, i: (rb(b, i), 0)),
                pl.BlockSpec((rows, LANES), lambda b, g, i: (rb(b, i), 0)),
                pl.BlockSpec((rows // c, None, 4, c), lambda b, g, i: (rb(b, i), d * SSD_GROUPS + g, 0, 0)),
                pl.BlockSpec((None, LANES, 4 * LANES), lambda b, g, i: (d * SSD_GROUPS + g, 0, 0))]

    out_f = pl.BlockSpec((rows, hp), lambda b, g, i: (b * nblk + i, g))
    out_b = pl.BlockSpec((rows, hp), lambda b, g, i: (b * nblk + nblk - 1 - i, g))
    oshape = jax.ShapeDtypeStruct((t, SSD_INNER), BF16)
    kern = functools.partial(_ssd_kernel, nchunk=rows // c)
    args = (xbc, xbc, xbc, dt, ac, act, e5)
    return pl.pallas_call(
        kern,
        grid=(bsz, SSD_GROUPS, nblk),
        in_specs=side(False) + side(True),
        out_specs=[out_f, out_b],
        out_shape=[oshape, oshape],
        scratch_shapes=[pltpu.VMEM((2, SSD_N, hp), F32)],
        compiler_params=_cparams(("parallel", "parallel", "arbitrary")),
        name="ssd_core",
    )(*args, *args)


def _ssd_out_kernel(yf_ref, yb_ref, x_ref, z_ref, dsk_ref, g_ref, w_ref, h_ref, o_ref):
    y = yf_ref[...].astype(F32) + yb_ref[...].astype(F32) + x_ref[...].astype(F32) * dsk_ref[...]
    y = y * _silu(z_ref[...].astype(F32))
    gw = SSD_INNER // SSD_GROUPS
    parts = []
    for gg in range(SSD_GROUPS):
        yg = y[:, gg * gw:(gg + 1) * gw]
        ms = jnp.mean(yg * yg, axis=-1, keepdims=True)
        parts.append(yg * lax.rsqrt(ms + EPS))
    a = (jnp.concatenate(parts, axis=1) * g_ref[...]).astype(BF16)
    o_ref[...] = h_ref[...] + _dot(a, w_ref[...])


def ssd_out(y_f, y_b, xbc, zx, d_skip, g_norm, w_out, h, tm=512):
    t, n = h.shape
    wide = pl.BlockSpec((tm, SSD_INNER), lambda i: (i, 0))
    vec = pl.BlockSpec((1, SSD_INNER), lambda i: (0, 0))
    return pl.pallas_call(
        _ssd_out_kernel,
        grid=(t // tm,),
        in_specs=[wide, wide, wide, wide, vec, vec,
                  pl.BlockSpec((SSD_INNER, n), lambda i: (0, 0)),
                  pl.BlockSpec((tm, n), lambda i: (i, 0))],
        out_specs=pl.BlockSpec((tm, n), lambda i: (i, 0)),
        out_shape=jax.ShapeDtypeStruct((t, n), F32),
        compiler_params=_cparams(("parallel",)),
        name="ssd_out",
    )(y_f, y_b, xbc, zx, jnp.repeat(d_skip, SSD_P).reshape(1, SSD_INNER),
      g_norm.reshape(1, SSD_INNER), w_out, h)


def ssd_layer(h, norm_g, w_in, conv_w, conv_b, dt_bias_f, a_log_f, dt_bias_b, a_log_b, d_skip,
              g_norm, w_out, bsz, seq):
    n_main = SSD_INNER + SSD_CONV_DIM
    wdt = jnp.pad(w_in[:, n_main:], ((0, 0), (0, LANES - 2 * SSD_HEADS))).astype(BF16)
    zx, dtr = norm_matmul(h, norm_g, w_in[:, :n_main].astype(BF16), wdt)
    xbc = ssd_conv(zx, conv_w, conv_b, bsz, seq)
    dt_bias = jnp.concatenate([dt_bias_f, dt_bias_b])
    a_neg = -jnp.exp(jnp.concatenate([a_log_f, a_log_b]))
    dt, ac, act = ssd_prep(dtr, dt_bias, a_neg)
    y_f, y_b = ssd_core(xbc, dt, ac, act, bsz, seq)
    return ssd_out(y_f, y_b, xbc, zx, d_skip, g_norm, w_out.astype(BF16), h), None


DIL_TQ = 128


def _dil_kernel(q_ref, kp_ref, kc_ref, kn_ref, vp_ref, vc_ref, vn_ref, o_ref, lse_ref, *, bounded):
    ai = pl.program_id(2)
    na = pl.num_programs(2)
    nres, tqb = q_ref.shape[0], q_ref.shape[1]
    tq = DIL_TQ
    side = DIL_SIDE
    nk = tq + 2 * side
    ii = lax.broadcasted_iota(jnp.int32, (4 * tq, nk), 0) & (tq - 1)
    jj = lax.broadcasted_iota(jnp.int32, (4 * tq, nk), 1)
    band = (jj >= ii) & (jj <= ii + 2 * side)
    lane = lax.broadcasted_iota(jnp.int32, (tq, ROPE_GROUP), 1)
    for rr, sb in ((rr, sb) for rr in range(nres) for sb in range(tqb // tq)):
        kcat = jnp.concatenate([kp_ref[rr], kc_ref[rr], kn_ref[rr]], axis=0)
        vcat = jnp.concatenate([vp_ref[rr], vc_ref[rr], vn_ref[rr]], axis=0)
        qrows = slice(sb * tq, (sb + 1) * tq)
        krows = slice(sb * tq, sb * tq + nk)
        kpos = ai * tqb + sb * tq - side + jj
        valid = band & (kpos >= 0) & (kpos < na * tqb)
        for g4 in range(DIL_HEADS // 4):
            gsl = slice(g4 * ROPE_GROUP, (g4 + 1) * ROPE_GROUP)
            q4 = q_ref[rr, qrows, gsl]
            zq = jnp.zeros_like(q4)
            qst = jnp.concatenate([jnp.where(_head_lane_mask(q4.shape, u), q4, zq) for u in range(4)],
                                  axis=0)
            s = jnp.where(valid, _dot_nt(qst, kcat[krows, gsl]), NEG_BIG)
            if bounded:
                p = jnp.exp2(s)
            else:
                m = jnp.max(s, axis=1, keepdims=True)
                p = jnp.exp2(s - m)
            den = jnp.sum(p, axis=1, keepdims=True)
            r = _dot(p.astype(BF16), vcat[krows, gsl])
            lse = jnp.log(den) if bounded else m * (1.0 / LOG2E) + jnp.log(den)
            out = jnp.zeros((tq, ROPE_GROUP), F32)
            lse4 = jnp.zeros((tq, ROPE_GROUP), F32)
            for u in range(4):
                rows = slice(u * tq, (u + 1) * tq)
                mine = (lane >= u * DIL_HD) & (lane < (u + 1) * DIL_HD)
                out = jnp.where(mine, r[rows] / den[rows], out)
                lse4 = jnp.where(mine, lse[rows], lse4)
            o_ref[rr, qrows, gsl] = out.astype(o_ref.dtype)
            lse_ref[rr, qrows, gsl] = lse4


def dil_core(pg, dilation, score_bound, bsz, seq):
    a = seq // dilation
    side = DIL_SIDE
    tq = min(4 * DIL_TQ, DIL_SUPER // dilation)
    nres = 4 * DIL_TQ // tq
    na = a // tq
    nst = seq // DIL_SUPER
    per = DIL_SUPER // dilation // tq
    per_h = DIL_SUPER // dilation // side
    hs = tq // side
    nhb = a // side

    def cur(col):
        return pl.BlockSpec((None, nres, tq, DIL_WIDTH),
                            lambda b, r, i: (b * nst + i // per, r, i % per, col))

    def halo(col, which):
        def imap(b, r, i):
            hb = (jnp.maximum(hs * i - 1, 0) if which == 0 else jnp.minimum(hs * (i + 1), nhb - 1))
            return (b * nst + hb // per_h, r, hb % per_h, col)
        return pl.BlockSpec((None, nres, side, DIL_WIDTH), imap)

    out_spec = pl.BlockSpec((None, nres, tq, DIL_WIDTH), lambda b, r, i: (b * nst + i // per, r, i % per, 0))
    oshape = (bsz * nst, dilation, DIL_SUPER // dilation, DIL_WIDTH)

    def call(bounded):
        return lambda x: pl.pallas_call(
            functools.partial(_dil_kernel, bounded=bounded),
            grid=(bsz, dilation // nres, na),
            in_specs=[cur(0), halo(1, 0), cur(1), halo(1, 1), halo(2, 0), cur(2), halo(2, 1)],
            out_specs=[out_spec, out_spec],
            out_shape=[jax.ShapeDtypeStruct(oshape, BF16), jax.ShapeDtypeStruct(oshape, F32)],
            compiler_params=_cparams(("parallel", "parallel", "parallel")),
            name=f"dil_core_{dilation}" + ("_bounded" if bounded else "_online"),
        )(x, x, x, x, x, x, x)

    return lax.cond(score_bound <= SAFE_LOG2, call(True), call(False), pg)


def _dil_out_kernel(o0_ref, o1_ref, o2_ref, l0_ref, l1_ref, l2_ref, w_ref, h_ref, o_ref,
                    so_ref, sl_ref):
    tm = h_ref.shape[0]
    nch = DIL_WIDTH // LANES
    for gi, (og_ref, lg_ref) in enumerate(((o1_ref, l1_ref), (o2_ref, l2_ref))):
        d = DIL_PAIRS[gi + 1][1]
        n = tm // d
        for r in range(d):
            ov = og_ref[r].astype(F32)
            lv = lg_ref[r]
            for c in range(nch):
                so_ref[gi, c, pl.ds(r, n, stride=d), :] = ov[:, c * LANES:(c + 1) * LANES]
                sl_ref[gi, c, pl.ds(r, n, stride=d), :] = lv[:, c * LANES:(c + 1) * LANES]
    parts = []
    for c in range(nch):
        sl = slice(c * LANES, (c + 1) * LANES)
        l0 = l0_ref[0, :, sl]
        l1 = sl_ref[0, c]
        l2 = sl_ref[1, c]
        mx = jnp.maximum(jnp.maximum(l0, l1), l2)
        e0 = jnp.exp(l0 - mx)
        e1 = jnp.exp(l1 - mx)
        e2 = jnp.exp(l2 - mx)
        num = e0 * o0_ref[0, :, sl].astype(F32) + e1 * so_ref[0, c] + e2 * so_ref[1, c]
        parts.append((num / (e0 + e1 + e2)).astype(BF16))
    o_ref[...] = h_ref[...] + _dot(jnp.concatenate(parts, axis=1), w_ref[...])


def dil_out(os, lses, w_out, h, tm=512):
    t, n = h.shape
    per = DIL_SUPER // tm

    def grp(d):
        return pl.BlockSpec((None, d, tm // d, DIL_WIDTH), lambda i: (i // per, 0, i % per, 0))

    specs = [grp(d) for _, d in DIL_PAIRS]
    row = pl.BlockSpec((tm, n), lambda i: (i, 0))
    return pl.pallas_call(
        _dil_out_kernel,
        grid=(t // tm,),
        in_specs=specs + specs + [pl.BlockSpec((DIL_WIDTH, n), lambda i: (0, 0)), row],
        out_specs=row,
        out_shape=jax.ShapeDtypeStruct((t, n), F32),
        scratch_shapes=[pltpu.VMEM((2, DIL_WIDTH // LANES, tm, LANES), F32),
                        pltpu.VMEM((2, DIL_WIDTH // LANES, tm, LANES), F32)],
        compiler_params=_cparams(("parallel",)),
        name="dil_out",
    )(*os, *lses, w_out, h)


def dil_layer(h, norm_g, w_in, g_q, g_k, w_out, bsz, seq):
    qscale = (DIL_HD ** -0.5) * LOG2E
    os, lses = [], []
    tables = _rope_tables(seq)
    ngrp = len(DIL_PAIRS)
    for gi, (_, dilation) in enumerate(DIL_PAIRS):
        cols = ((jnp.arange(3)[:, None] * ngrp + gi) * DIL_WIDTH + jnp.arange(DIL_WIDTH)[None, :]).reshape(-1)
        rope_gain = jnp.concatenate([jnp.tile(g_q[gi], DIL_HEADS) * qscale, jnp.tile(g_k[gi], DIL_HEADS)])
        pg = norm_matmul_rope(h, norm_g, w_in, cols, rope_gain, tables, seq, dil=dilation)
        o, lse = dil_core(pg, dilation, _score_bound(g_q[gi], g_k[gi], qscale), bsz, seq)
        os.append(o)
        lses.append(lse)
    return dil_out(os, lses, w_out.astype(BF16), h), None


def kernel(x, p, norm_mix, norm_ffn, ffn_w_in, ffn_w_out, ple_norm, ple_w_gate, ple_w_proj, gla_w_in, gla_w_gate_f, gla_b_gate_f, gla_w_gate_b, gla_b_gate_b, gla_g_out, gla_w_out, diff_w_in, diff_g_q, diff_g_k, diff_lam_q1, diff_lam_k1, diff_lam_q2, diff_lam_k2, diff_g_sub, diff_w_out, ssd_w_in, ssd_conv_w, ssd_conv_b, ssd_dt_bias_f, ssd_a_log_f, ssd_dt_bias_b, ssd_a_log_b, ssd_d, ssd_g_norm, ssd_w_out, dil_w_in, dil_g_q, dil_g_k, dil_w_out):
    bsz, seq, dm = x.shape
    t = bsz * seq
    h = x.reshape(t, dm)
    for i in range(DEPTH):
        kind = i % 4
        j = i // 4
        if kind == 0:
            h, mixer = gla_layer(h, norm_mix[i], gla_w_in[j], gla_w_gate_f[j], gla_b_gate_f[j],
                                 gla_w_gate_b[j], gla_b_gate_b[j], gla_g_out[j], gla_w_out[j], bsz, seq)
        elif kind == 1:
            h, mixer = diff_layer(h, norm_mix[i], i, diff_w_in[j], diff_g_q[j], diff_g_k[j],
                                  diff_lam_q1[j], diff_lam_k1[j], diff_lam_q2[j], diff_lam_k2[j],
                                  diff_g_sub[j], diff_w_out[j], bsz, seq)
        elif kind == 2:
            h, mixer = ssd_layer(h, norm_mix[i], ssd_w_in[j], ssd_conv_w[j], ssd_conv_b[j],
                                 ssd_dt_bias_f[j], ssd_a_log_f[j], ssd_dt_bias_b[j], ssd_a_log_b[j],
                                 ssd_d[j], ssd_g_norm[j], ssd_w_out[j], bsz, seq)
        else:
            h, mixer = dil_layer(h, norm_mix[i], dil_w_in[j], dil_g_q[j], dil_g_k[j], dil_w_out[j],
                                 bsz, seq)
        ffn_ple_params = (norm_ffn[i], ffn_w_in[i].astype(BF16), ffn_w_out[i].astype(BF16), ple_norm[i],
                          ple_w_gate[i].astype(BF16), p[i].reshape(t, PLE_DIM), ple_w_proj[i].astype(BF16))
        h = ffn_ple_residual(h, ffn_ple_params, mixer)
    return h.reshape(bsz, seq, dm)
```
